```python
import math
import jax
import jax.numpy as jnp
from jax import lax
import numpy as np

D_MODEL = 2048
BATCH = 8
SEQ = 2048
DEPTH = 1

N_META = 16
HEAD_DIM = 64
ATT_HEADS = D_MODEL // HEAD_DIM
ATT_KV_HEADS = ATT_HEADS // 8
ATT_WIDTH = ATT_HEADS * HEAD_DIM
KV_WIDTH = ATT_KV_HEADS * HEAD_DIM
WINDOW = 128
BLOCK = 128
ROPE_DIM = HEAD_DIM // 4
ROPE_THETA = 500000.0
SSM_WIDTH = 2 * D_MODEL
SSM_HEAD_DIM = 64
SSM_HEADS = SSM_WIDTH // SSM_HEAD_DIM
SSM_GROUPS = 8
SSM_STATE = 128
CONV_WIDTH = 4
CHUNK = 128
DT_MIN = 0.001
DT_MAX = 0.1
CONV_CH = SSM_WIDTH + 2 * SSM_GROUPS * SSM_STATE
MIX_WIDTH = ATT_WIDTH + SSM_WIDTH
IN_SPLITS = (ATT_WIDTH,
             ATT_WIDTH + KV_WIDTH,
             ATT_WIDTH + 2 * KV_WIDTH,
             ATT_WIDTH + 2 * KV_WIDTH + SSM_WIDTH,
             ATT_WIDTH + 2 * KV_WIDTH + SSM_WIDTH + CONV_CH)
IN_WIDTH = IN_SPLITS[-1] + SSM_HEADS
N_EXPERTS = 64
EXPERT_DIM = D_MODEL // 4
SHARED_DIM = EXPERT_DIM
TOP_K = 8
N_EXPERT_GROUPS = 8
TOPK_GROUPS = 4
ROUTED_SCALE = 2.5
MOE_BLOCK = 256
DEEPNORM_ALPHA = (2 * DEPTH) ** 0.25
DEEPNORM_BETA = (8 * DEPTH) ** -0.25
LN_EPS = 1e-5
RMS_EPS = 1e-6
NEG_INF = -1e30

kernel_name = 'hymba_swa_sink_ssd_moe_deepnorm'


def layer_norm(x, g, b):
    xf = x.astype(jnp.float32)
    mu = xf.mean(-1, keepdims=True)
    var = jnp.square(xf - mu).mean(-1, keepdims=True)
    return ((xf - mu) * lax.rsqrt(var + LN_EPS) * g.astype(jnp.float32) + b.astype(jnp.float32)).astype(x.dtype)


def rms_norm(x, g):
    xf = x.astype(jnp.float32)
    return (xf * lax.rsqrt(jnp.mean(xf * xf, -1, keepdims=True) + RMS_EPS) * g.astype(jnp.float32)).astype(x.dtype)


def partial_rotary(t, pos):
    half = ROPE_DIM // 2
    inv_freq = jnp.power(ROPE_THETA, -jnp.arange(0, ROPE_DIM, 2, dtype=jnp.float32) / ROPE_DIM)
    ang = pos.astype(jnp.float32)[:, None] * inv_freq[None, :]
    cos = jnp.cos(ang)[None, :, None, :]
    sin = jnp.sin(ang)[None, :, None, :]
    tr = t[..., :ROPE_DIM].astype(jnp.float32)
    t1, t2 = tr[..., :half], tr[..., half:]
    rot = jnp.concatenate([t1 * cos - t2 * sin, t2 * cos + t1 * sin], -1).astype(t.dtype)
    return jnp.concatenate([rot, t[..., ROPE_DIM:]], -1)


def sliding_window_attention(q, k, v, sinks):
    b, L = q.shape[0], q.shape[1]
    pad = (-N_META) % BLOCK
    lp = L + pad
    nb = lp // BLOCK
    grp = ATT_HEADS // ATT_KV_HEADS
    padf = lambda t: jnp.pad(t, ((0, 0), (pad, 0), (0, 0), (0, 0)))
    qb = padf(q).reshape(b, nb, BLOCK, ATT_KV_HEADS, grp, HEAD_DIM)
    kb = padf(k).reshape(b, nb, BLOCK, ATT_KV_HEADS, HEAD_DIM)
    vb = padf(v).reshape(b, nb, BLOCK, ATT_KV_HEADS, HEAD_DIM)
    shift = lambda t: jnp.concatenate([jnp.zeros_like(t[:, :1]), t[:, :-1]], axis=1)
    k_band = jnp.concatenate([shift(kb), kb], axis=2)
    v_band = jnp.concatenate([shift(vb), vb], axis=2)
    k_meta, v_meta = k[:, :N_META], v[:, :N_META]
    scale = HEAD_DIM ** -0.5
    s_band = jnp.einsum('bnqkgd,bnskd->bnkgqs', qb, k_band, preferred_element_type=jnp.float32) * scale
    s_meta = jnp.einsum('bnqkgd,bskd->bnkgqs', qb, k_meta, preferred_element_type=jnp.float32) * scale
    q_pos = (jnp.arange(lp) - pad).reshape(nb, BLOCK)
    k_pos = (jnp.arange(nb)[:, None] - 1) * BLOCK + jnp.arange(2 * BLOCK)[None, :] - pad
    diff = q_pos[:, :, None] - k_pos[:, None, :]
    band_ok = (k_pos[:, None, :] >= N_META) & (diff >= 0) & (diff < WINDOW)
    meta_ok = jnp.arange(N_META)[None, None, :] <= q_pos[:, :, None]
    mask = jnp.concatenate([meta_ok, band_ok], -1)[None, :, None, None]
    scores = jnp.where(mask, jnp.concatenate([s_meta, s_band], -1), NEG_INF)
    sink = sinks.astype(jnp.float32).reshape(ATT_KV_HEADS, grp)[None, None, :, :, None, None]
    sink = jnp.broadcast_to(sink, scores.shape[:-1] + (1,))
    probs = jax.nn.softmax(jnp.concatenate([scores, sink], -1), axis=-1)[..., :-1].astype(v.dtype)
    o = (jnp.einsum('bnkgqs,bskd->bnqkgd', probs[..., :N_META], v_meta)
         + jnp.einsum('bnkgqs,bnskd->bnqkgd', probs[..., N_META:], v_band))
    return o.reshape(b, lp, ATT_WIDTH)[:, pad:]


def causal_depthwise_conv(u, w, bias):
    ch = u.shape[-1]
    out = lax.conv_general_dilated(u, w[:, None, :], window_strides=(1,), padding=[(CONV_WIDTH - 1, 0)],
                                   dimension_numbers=('NWC', 'WIO', 'NWC'), feature_group_count=ch)
    return out + bias


def ssd_chunked_scan(x, dt, a, bmat, cmat):
    b, lp = x.shape[0], x.shape[1]
    nc = lp // CHUNK
    rep = SSM_HEADS // SSM_GROUPS
    xdt = (x * dt[..., None]).reshape(b, nc, CHUNK, SSM_GROUPS, rep, SSM_HEAD_DIM)
    adt = (dt * a).reshape(b, nc, CHUNK, SSM_GROUPS, rep)
    bc = bmat.reshape(b, nc, CHUNK, SSM_GROUPS, SSM_STATE)
    cc = cmat.reshape(b, nc, CHUNK, SSM_GROUPS, SSM_STATE)
    causal = jnp.tril(jnp.ones((CHUNK, CHUNK), dtype=bool))[None, :, :, None, None]

    def step(state, inp):
        xc, ac, bch, cch = inp
        a_cs = jnp.cumsum(ac, axis=1)
        seg = a_cs[:, :, None] - a_cs[:, None, :]
        decay = jnp.exp(jnp.where(causal, seg, -jnp.inf))
        cb = jnp.einsum('bqgn,bsgn->bqsg', cch, bch)
        y_diag = jnp.einsum('bqsg,bqsgr,bsgrp->bqgrp', cb, decay, xc)
        y_off = jnp.einsum('bqgn,bgrpn,bqgr->bqgrp', cch, state, jnp.exp(a_cs))
        a_last = a_cs[:, -1]
        to_end = jnp.exp(a_last[:, None] - a_cs)
        new_state = state * jnp.exp(a_last)[..., None, None] + jnp.einsum('bsgn,bsgr,bsgrp->bgrpn', bch, to_end, xc)
        return new_state, y_diag + y_off

    state0 = jnp.zeros((b, SSM_GROUPS, rep, SSM_HEAD_DIM, SSM_STATE), jnp.float32)
    cm = lambda t: jnp.moveaxis(t, 1, 0)
    _, y = lax.scan(step, state0, (cm(xdt), cm(adt), cm(bc), cm(cc)))
    return jnp.moveaxis(y, 0, 1).reshape(b, lp, SSM_HEADS, SSM_HEAD_DIM)


def mamba2_mixer(z, xbc, dt_raw, conv_w, conv_b, dt_bias, a_log, d_skip, norm_g):
    b, L = z.shape[0], z.shape[1]
    xbc = jax.nn.silu(causal_depthwise_conv(xbc, conv_w, conv_b)).astype(jnp.float32)
    xs = xbc[..., :SSM_WIDTH].reshape(b, L, SSM_HEADS, SSM_HEAD_DIM)
    bm = xbc[..., SSM_WIDTH:SSM_WIDTH + SSM_GROUPS * SSM_STATE].reshape(b, L, SSM_GROUPS, SSM_STATE)
    cm = xbc[..., SSM_WIDTH + SSM_GROUPS * SSM_STATE:].reshape(b, L, SSM_GROUPS, SSM_STATE)
    dt = jax.nn.softplus(dt_raw.astype(jnp.float32) + dt_bias.astype(jnp.float32))
    a = -jnp.exp(a_log.astype(jnp.float32))
    pad = (-N_META) % CHUNK
    padf = lambda t: jnp.pad(t, ((0, 0), (pad, 0)) + ((0, 0),) * (t.ndim - 2))
    y = ssd_chunked_scan(padf(xs), padf(dt), a, padf(bm), padf(cm))[:, pad:]
    y = y + d_skip.astype(jnp.float32)[:, None] * xs
    y = y.reshape(b, L, SSM_WIDTH) * jax.nn.silu(z.astype(jnp.float32))
    yg = y.reshape(b, L, SSM_GROUPS, SSM_WIDTH // SSM_GROUPS)
    yg = yg * lax.rsqrt(jnp.mean(yg * yg, -1, keepdims=True) + RMS_EPS)
    return (yg.reshape(b, L, SSM_WIDTH) * norm_g.astype(jnp.float32)).astype(z.dtype)


def moe_ffn(h, w_router, b_router, w_gate, w_up, w_down, ws_gate, ws_up, ws_down):
    b, L, d = h.shape
    t = h.reshape(-1, d)
    n_tok = t.shape[0]
    scores = jax.nn.sigmoid(jnp.dot(t, w_router, preferred_element_type=jnp.float32))
    sel = scores + b_router.astype(jnp.float32)
    per = N_EXPERTS // N_EXPERT_GROUPS
    grp_score = lax.top_k(sel.reshape(n_tok, N_EXPERT_GROUPS, per), 2)[0].sum(-1)
    _, gidx = lax.top_k(grp_score, TOPK_GROUPS)
    gmask = jax.nn.one_hot(gidx, N_EXPERT_GROUPS).sum(1) > 0
    sel = jnp.where(jnp.repeat(gmask, per, axis=1), sel, -jnp.inf)
    _, eidx = lax.top_k(sel, TOP_K)
    gate = jnp.take_along_axis(scores, eidx, axis=1)
    gate = gate / gate.sum(-1, keepdims=True) * ROUTED_SCALE
    n_asg = n_tok * TOP_K
    flat_e = eidx.reshape(-1)
    flat_tok = jnp.arange(n_asg, dtype=jnp.int32) // TOP_K
    flat_g = gate.reshape(-1)
    order = jnp.argsort(flat_e)
    se, stok, sg = flat_e[order], flat_tok[order], flat_g[order]
    counts = jnp.bincount(flat_e, length=N_EXPERTS)
    start = jnp.cumsum(counts) - counts
    padded = (counts + MOE_BLOCK - 1) // MOE_BLOCK * MOE_BLOCK
    pend = jnp.cumsum(padded)
    pstart = pend - padded
    dest = pstart[se] + jnp.arange(n_asg, dtype=jnp.int32) - start[se]
    n_blocks = n_asg // MOE_BLOCK + N_EXPERTS
    rows = n_blocks * MOE_BLOCK
    tok_buf = jnp.zeros((rows,), jnp.int32).at[dest].set(stok)
    gate_buf = jnp.zeros((rows,), jnp.float32).at[dest].set(sg)
    blk_e = jnp.minimum(jnp.searchsorted(pend, jnp.arange(n_blocks, dtype=jnp.int32) * MOE_BLOCK, side='right'), N_EXPERTS - 1)

    def expert_block(args):
        toks, g, e = args
        xb = t[toks]
        hb = jax.nn.silu(xb @ w_gate[e]) * (xb @ w_up[e])
        return (hb @ w_down[e]) * g[:, None]

    yb = lax.map(expert_block, (tok_buf.reshape(n_blocks, MOE_BLOCK), gate_buf.reshape(n_blocks, MOE_BLOCK), blk_e))
    routed = jax.ops.segment_sum(yb.reshape(rows, d), tok_buf, num_segments=n_tok)
    shared = (jax.nn.silu(t @ ws_gate) * (t @ ws_up)) @ ws_down
    return (routed + shared).astype(h.dtype).reshape(b, L, d)


def setup_inputs(seed: int = 0) -> dict:
    key = jax.random.key(seed)
    ks = jax.random.split(key, 26)
    f32 = jnp.float32
    nrm = lambda k, shape, s: jax.random.normal(k, shape, f32) * s
    gain = lambda k, shape: 1.0 + 0.02 * jax.random.normal(k, shape, f32)
    dt0 = jnp.exp(jax.random.uniform(ks[7], (DEPTH, SSM_HEADS), f32) * (math.log(DT_MAX) - math.log(DT_MIN)) + math.log(DT_MIN))
    return {
        'x': nrm(ks[0], (BATCH, SEQ, D_MODEL), 1.0),
        'meta_tokens': nrm(ks[1], (N_META, D_MODEL), 1.0),
        'ln_in_g': gain(ks[2], (D_MODEL,)),
        'ln_in_b': nrm(ks[3], (D_MODEL,), 0.02),
        'w_in': nrm(ks[4], (DEPTH, D_MODEL, IN_WIDTH), D_MODEL ** -0.5),
        'conv_w': nrm(ks[5], (DEPTH, CONV_WIDTH, CONV_CH), CONV_WIDTH ** -0.5),
        'conv_b': nrm(ks[6], (DEPTH, CONV_CH), 0.02),
        'dt_bias': dt0 + jnp.log(-jnp.expm1(-dt0)),
        'a_log': jnp.log(jax.random.uniform(ks[8], (DEPTH, SSM_HEADS), f32, 1.0, 16.0)),
        'd_skip': gain(ks[9], (DEPTH, SSM_HEADS)),
        'ssm_norm_g': gain(ks[10], (DEPTH, SSM_WIDTH)),
        'att_norm_g': gain(ks[11], (DEPTH, ATT_WIDTH)),
        'attn_sinks': nrm(ks[12], (DEPTH, ATT_HEADS), 0.5),
        'w_out': nrm(ks[13], (DEPTH, MIX_WIDTH, D_MODEL), MIX_WIDTH ** -0.5 * DEEPNORM_BETA),
        'ln1_g': gain(ks[14], (DEPTH, D_MODEL)),
        'ln1_b': nrm(ks[15], (DEPTH, D_MODEL), 0.02),
        'w_router': nrm(ks[16], (DEPTH, D_MODEL, N_EXPERTS), D_MODEL ** -0.5),
        'b_router': nrm(ks[17], (DEPTH, N_EXPERTS), 0.01),
        'w_gate': nrm(ks[18], (DEPTH, N_EXPERTS, D_MODEL, EXPERT_DIM), D_MODEL ** -0.5),
        'w_up': nrm(ks[19], (DEPTH, N_EXPERTS, D_MODEL, EXPERT_DIM), D_MODEL ** -0.5),
        'w_down': nrm(ks[20], (DEPTH, N_EXPERTS, EXPERT_DIM, D_MODEL), EXPERT_DIM ** -0.5 * DEEPNORM_BETA),
        'ws_gate': nrm(ks[21], (DEPTH, D_MODEL, SHARED_DIM), D_MODEL ** -0.5),
        'ws_up': nrm(ks[22], (DEPTH, D_MODEL, SHARED_DIM), D_MODEL ** -0.5),
        'ws_down': nrm(ks[23], (DEPTH, SHARED_DIM, D_MODEL), SHARED_DIM ** -0.5 * DEEPNORM_BETA),
        'ln2_g': gain(ks[24], (DEPTH, D_MODEL)),
        'ln2_b': nrm(ks[25], (DEPTH, D_MODEL), 0.02),
    }


def reference(x, meta_tokens, ln_in_g, ln_in_b, w_in, conv_w, conv_b, dt_bias, a_log, d_skip,
              ssm_norm_g, att_norm_g, attn_sinks, w_out, ln1_g, ln1_b, w_router, b_router,
              w_gate, w_up, w_down, ws_gate, ws_up, ws_down, ln2_g, ln2_b):
    b = x.shape[0]
    meta = jnp.broadcast_to(meta_tokens[None].astype(x.dtype), (b, N_META, D_MODEL))
    h = layer_norm(jnp.concatenate([meta, x], axis=1), ln_in_g, ln_in_b)
    L = h.shape[1]
    pos = jnp.arange(L)
    for l in range(DEPTH):
        proj = jnp.einsum('bld,dc->blc', h, w_in[l])
        q, k, v, z, xbc, dt_raw = jnp.split(proj, IN_SPLITS, axis=-1)
        q = partial_rotary(q.reshape(b, L, ATT_HEADS, HEAD_DIM), pos)
        k = partial_rotary(k.reshape(b, L, ATT_KV_HEADS, HEAD_DIM), pos)
        v = v.reshape(b, L, ATT_KV_HEADS, HEAD_DIM)
        att = rms_norm(sliding_window_attention(q, k, v, attn_sinks[l]), att_norm_g[l])
        ssm = mamba2_mixer(z, xbc, dt_raw, conv_w[l], conv_b[l], dt_bias[l], a_log[l], d_skip[l], ssm_norm_g[l])
        mix = jnp.einsum('blc,cd->bld', jnp.concatenate([att, ssm.astype(att.dtype)], -1), w_out[l])
        h = layer_norm(DEEPNORM_ALPHA * h + mix, ln1_g[l], ln1_b[l])
        ffn = moe_ffn(h, w_router[l], b_router[l], w_gate[l], w_up[l], w_down[l], ws_gate[l], ws_up[l], ws_down[l])
        h = layer_norm(DEEPNORM_ALPHA * h + ffn, ln2_g[l], ln2_b[l])
    return h[:, N_META:]
```

```python
import functools
import math

import jax
import jax.numpy as jnp
from jax import lax
from jax.experimental import pallas as pl
from jax.experimental.pallas import tpu as pltpu

f32 = jnp.float32
bf16 = jnp.bfloat16
i32 = jnp.int32

D_MODEL = 2048
N_META = 16
HEAD_DIM = 64
ATT_HEADS = 32
ATT_KV_HEADS = 4
ATT_GROUP = ATT_HEADS // ATT_KV_HEADS
ATT_WIDTH = 2048
KV_WIDTH = 256
BLOCK = 128
ROPE_DIM = 16
ROPE_THETA = 500000.0
SSM_WIDTH = 4096
SSM_HEADS = 64
SSM_GROUPS = 8
SSM_HEADS_PER_GROUP = SSM_HEADS // SSM_GROUPS
SSM_STATE = 128
SSM_GROUP_WIDTH = SSM_WIDTH // SSM_GROUPS
CONV_WIDTH = 4
CHUNK = 128
CONV_CH = SSM_WIDTH + 2 * SSM_GROUPS * SSM_STATE
MAIN_WIDTH = ATT_WIDTH + 2 * KV_WIDTH + SSM_WIDTH + CONV_CH
Z_COL = ATT_WIDTH + 2 * KV_WIDTH
XBC_COL = Z_COL + SSM_WIDTH
N_EXPERTS = 64
EXPERT_DIM = 512
TOP_K = 8
N_EXPERT_GROUPS = 8
EXPERTS_PER_GROUP = N_EXPERTS // N_EXPERT_GROUPS
TOPK_GROUPS = 4
ROUTED_SCALE = 2.5
MOE_BLOCK = 256
DEEPNORM_ALPHA = 2.0 ** 0.25
LN_EPS = 1e-5
RMS_EPS = 1e-6
NEG_INF = -1e30
LANES = 128
DT_WIDTH = SSM_GROUPS * LANES
VMEM_LIMIT = 56 * 1024 * 1024


def _layer_norm(x, g, b):
    mu = jnp.mean(x, axis=-1, keepdims=True)
    xc = x - mu
    var = jnp.mean(xc * xc, axis=-1, keepdims=True)
    return xc * lax.rsqrt(var + LN_EPS) * g + b


def _silu(x):
    return x * (1.0 / (1.0 + jnp.exp(-x)))


def _softplus(x):
    return jnp.maximum(x, 0.0) + jnp.log1p(jnp.exp(-jnp.abs(x)))


def _ln_inproj_body(x_ref, g_ref, b_ref, w_ref, wdt_ref, o_ref, dt_ref, h_scr):
    @pl.when(pl.program_id(1) == 0)
    def _():
        h = _layer_norm(x_ref[...], g_ref[...], b_ref[...]).astype(bf16)
        h_scr[...] = h
        dt_ref[...] = jnp.dot(h, wdt_ref[...], preferred_element_type=f32)

    o_ref[...] = jnp.dot(h_scr[...], w_ref[...], preferred_element_type=f32).astype(bf16)


def _ln_inproj(x2d, g, b, w_main, w_dt, *, tm, tn):
    m = x2d.shape[0]
    return pl.pallas_call(
        _ln_inproj_body,
        grid=(m // tm, MAIN_WIDTH // tn),
        in_specs=[
            pl.BlockSpec((tm, D_MODEL), lambda i, j: (i, 0)),
            pl.BlockSpec((1, D_MODEL), lambda i, j: (0, 0)),
            pl.BlockSpec((1, D_MODEL), lambda i, j: (0, 0)),
            pl.BlockSpec((D_MODEL, tn), lambda i, j: (0, j)),
            pl.BlockSpec((D_MODEL, DT_WIDTH), lambda i, j: (0, 0)),
        ],
        out_specs=[
            pl.BlockSpec((tm, tn), lambda i, j: (i, j)),
            pl.BlockSpec((tm, DT_WIDTH), lambda i, j: (i, 0)),
        ],
        out_shape=[
            jax.ShapeDtypeStruct((m, MAIN_WIDTH), bf16),
            jax.ShapeDtypeStruct((m, DT_WIDTH), f32),
        ],
        scratch_shapes=[pltpu.VMEM((tm, D_MODEL), bf16)],
        compiler_params=pltpu.CompilerParams(
            dimension_semantics=("arbitrary", "arbitrary"), vmem_limit_bytes=VMEM_LIMIT),
        name="ln_inproj",
    )(x2d, g, b, w_main, w_dt)


def _rotate(t, tab):
    w = t.shape[-1]
    half = ROPE_DIM // 2
    return (t * tab[0]
            + pltpu.roll(t, w - half, 1) * tab[1]
            + pltpu.roll(t, half, 1) * tab[2])


def _attention_body(sink_ref, q_ref, kvc_ref, kvp_ref, kvm_ref, tabc_ref, tabp_ref, tabm_ref,
                    g_ref, o_ref):
    j = pl.program_id(1)
    tabc = tabc_ref[...]
    kc = _rotate(kvc_ref[:, :KV_WIDTH].astype(f32), tabc[:, :, :KV_WIDTH]).astype(bf16)
    kp = _rotate(kvp_ref[:, :KV_WIDTH].astype(f32), tabp_ref[:, :, :KV_WIDTH]).astype(bf16)
    km = _rotate(kvm_ref[:, :KV_WIDTH].astype(f32), tabm_ref[:, :, :KV_WIDTH]).astype(bf16)
    k_all = jnp.concatenate([kp, kc, km], axis=0)
    v_all = jnp.concatenate([kvp_ref[:, KV_WIDTH:], kvc_ref[:, KV_WIDTH:], kvm_ref[:, KV_WIDTH:]], axis=0)

    rows = ATT_GROUP * BLOCK
    r = lax.broadcasted_iota(i32, (rows, 3 * BLOCK), 0) % BLOCK
    c = lax.broadcasted_iota(i32, (rows, 3 * BLOCK), 1)
    mask = (((c < BLOCK) & (c > r) & (j > 0))
            | ((c >= BLOCK) & (c - BLOCK <= r))
            | ((c >= 2 * BLOCK) & (c < 2 * BLOCK + N_META)))
    head_of_row = lax.broadcasted_iota(i32, (rows, 1), 0) // BLOCK

    outs = []
    for g in range(ATT_KV_HEADS):
        gw = ATT_GROUP * HEAD_DIM
        qg = _rotate(q_ref[:, g * gw:(g + 1) * gw].astype(f32), tabc) * (HEAD_DIM ** -0.5)
        qs = jnp.concatenate([qg[:, h * HEAD_DIM:(h + 1) * HEAD_DIM] for h in range(ATT_GROUP)],
                             axis=0).astype(bf16)
        kg = k_all[:, g * HEAD_DIM:(g + 1) * HEAD_DIM]
        vg = v_all[:, g * HEAD_DIM:(g + 1) * HEAD_DIM]
        s = lax.dot_general(qs, kg, (((1,), (1,)), ((), ())), preferred_element_type=f32)
        s = jnp.where(mask, s, NEG_INF)
        sink = jnp.zeros((rows, 1), f32)
        for h in range(ATT_GROUP):
            sink = jnp.where(head_of_row == h, sink_ref[g * ATT_GROUP + h], sink)
        m = jnp.maximum(jnp.max(s, axis=1, keepdims=True), sink)
        p = jnp.exp(s - m)
        denom = jnp.sum(p, axis=1, keepdims=True) + jnp.exp(sink - m)
        o = jnp.dot(p.astype(bf16), vg, preferred_element_type=f32) / denom
        outs.append(jnp.concatenate([o[h * BLOCK:(h + 1) * BLOCK] for h in range(ATT_GROUP)], axis=1))
    att = jnp.concatenate(outs, axis=1)
    att = att * lax.rsqrt(jnp.mean(att * att, axis=-1, keepdims=True) + RMS_EPS) * g_ref[...]
    o_ref[...] = att.astype(bf16)


def _attention(proj, kv_meta, tab, tab_meta, sinks, att_g, *, batch, nblk):
    m = proj.shape[0]
    kvb = ATT_WIDTH // (2 * KV_WIDTH)
    gw = ATT_GROUP * HEAD_DIM
    return pl.pallas_call(
        _attention_body,
        grid=(batch, nblk),
        in_specs=[
            pl.BlockSpec(memory_space=pltpu.SMEM),
            pl.BlockSpec((BLOCK, ATT_WIDTH), lambda b, j: (b * nblk + j, 0)),
            pl.BlockSpec((BLOCK, 2 * KV_WIDTH), lambda b, j: (b * nblk + j, kvb)),
            pl.BlockSpec((BLOCK, 2 * KV_WIDTH), lambda b, j: (b * nblk + jnp.maximum(j - 1, 0), kvb)),
            pl.BlockSpec((BLOCK, 2 * KV_WIDTH), lambda b, j: (0, 0)),
            pl.BlockSpec((3, BLOCK, gw), lambda b, j: (0, j, 0)),
            pl.BlockSpec((3, BLOCK, gw), lambda b, j: (0, jnp.maximum(j - 1, 0), 0)),
            pl.BlockSpec((3, BLOCK, gw), lambda b, j: (0, 0, 0)),
            pl.BlockSpec((1, ATT_WIDTH), lambda b, j: (0, 0)),
        ],
        out_specs=pl.BlockSpec((BLOCK, ATT_WIDTH), lambda b, j: (b * nblk + j, 0)),
        out_shape=jax.ShapeDtypeStruct((m, ATT_WIDTH), bf16),
        compiler_params=pltpu.CompilerParams(
            dimension_semantics=("arbitrary", "arbitrary"), vmem_limit_bytes=VMEM_LIMIT),
        name="swa_attention",
    )(sinks, proj, proj, proj, kv_meta, tab, tab, tab_meta, att_g)


def _cumsum_rows(x):
    n = x.shape[0]
    row = lax.broadcasted_iota(i32, x.shape, 0)
    k = 1
    while k < n:
        x = x + jnp.where(row >= k, pltpu.roll(x, k, 0), 0.0)
        k *= 2
    return x


def _expand_heads(col):
    rows = col.shape[0]
    lane_head = lax.broadcasted_iota(i32, (rows, SSM_GROUP_WIDTH), 1) // HEAD_DIM
    out = jnp.zeros((rows, SSM_GROUP_WIDTH), f32)
    for i in range(SSM_HEADS_PER_GROUP):
        out = jnp.where(lane_head == i, col[:, i:i + 1], out)
    return out


def _ssd_body(z_ref, x_ref, b_ref, c_ref, dt_ref, hx_ref, hb_ref, hc_ref, s0_ref,
              wx_ref, wb_ref, wc_ref, bx_ref, bb_ref, bc_ref, dtb_ref, alog_ref, dskip_ref, ng_ref,
              *rest, vstart, emit_state):
    if emit_state:
        y_ref, sout_ref, state, halo_x, halo_b, halo_c, work_x, work_b, work_c = rest
    else:
        y_ref, state, halo_x, halo_b, halo_c, work_x, work_b, work_c = rest
    c_idx = pl.program_id(1)
    g = pl.program_id(2)

    @pl.when(c_idx == 0)
    def _():
        state[g] = s0_ref[0]
        halo_x[g] = hx_ref[...]
        halo_b[g] = hb_ref[...]
        halo_c[g] = hc_ref[...]

    row = lax.broadcasted_iota(i32, (CHUNK, 1), 0)
    valid = row >= vstart

    def conv(u_ref, halo, work, w_ref, bias_ref):
        u = u_ref[...].astype(f32)
        work[0:8] = halo[g]
        work[8:8 + CHUNK] = u
        w = w_ref[...]
        acc = bias_ref[...] + u * w[3:4]
        for jj in range(CONV_WIDTH - 1):
            acc = acc + work[5 + jj:5 + jj + CHUNK] * w[jj:jj + 1]
        halo[g] = u[CHUNK - 8:]
        return jnp.where(valid, _silu(acc), 0.0)

    xs = conv(x_ref, halo_x, work_x, wx_ref, bx_ref)
    bm = conv(b_ref, halo_b, work_b, wb_ref, bb_ref)
    cm = conv(c_ref, halo_c, work_c, wc_ref, bc_ref)

    dt = jnp.where(valid, _softplus(dt_ref[...] + dtb_ref[...]), 0.0)
    a = -jnp.exp(alog_ref[...])
    a_cs = _cumsum_rows(dt * a)
    a_cs_t = a_cs.T
    a_last = a_cs[CHUNK - 1:CHUNK]

    xdt = xs * _expand_heads(dt)
    bmb = bm.astype(bf16)
    cmb = cm.astype(bf16)
    cb = lax.dot_general(cmb, bmb, (((1,), (1,)), ((), ())), preferred_element_type=f32)
    causal = (lax.broadcasted_iota(i32, (CHUNK, CHUNK), 0) >= lax.broadcasted_iota(i32, (CHUNK, CHUNK), 1))
    xdtb = xdt.astype(bf16)
    ys = []
    for i in range(SSM_HEADS_PER_GROUP):
        seg = a_cs[:, i:i + 1] - a_cs_t[i:i + 1, :]
        lmat = (cb * jnp.exp(jnp.where(causal, seg, -jnp.inf))).astype(bf16)
        ys.append(jnp.dot(lmat, xdtb[:, i * HEAD_DIM:(i + 1) * HEAD_DIM], preferred_element_type=f32))
    y = jnp.concatenate(ys, axis=1)

    st = state[g]
    y_off = lax.dot_general(cmb, st.astype(bf16), (((1,), (1,)), ((), ())), preferred_element_type=f32)
    y = y + y_off * _expand_heads(jnp.exp(a_cs))

    x_end = (xdt * _expand_heads(jnp.exp(a_last - a_cs))).T.astype(bf16)
    row_head = lax.broadcasted_iota(i32, (SSM_GROUP_WIDTH, 1), 0) // HEAD_DIM
    decay_rows = jnp.zeros((SSM_GROUP_WIDTH, 1), f32)
    for i in range(SSM_HEADS_PER_GROUP):
        decay_rows = jnp.where(row_head == i, a_cs_t[i:i + 1, CHUNK - 1:CHUNK], decay_rows)
    new_state = st * jnp.exp(decay_rows) + jnp.dot(x_end, bmb, preferred_element_type=f32)
    state[g] = new_state
    if emit_state:
        sout_ref[0] = new_state

    y = y + dskip_ref[...] * xs
    y = y * _silu(z_ref[...].astype(f32))
    y = y * lax.rsqrt(jnp.mean(y * y, axis=-1, keepdims=True) + RMS_EPS) * ng_ref[...]
    y_ref[...] = y.astype(bf16)


def _ssd(proj, dtx, halo, s0, conv_w, conv_b, dt_bias_x, a_log_x, d_skip_ch, norm_g,
         *, batch, nchunk, vstart, emit_state):
    m = proj.shape[0]
    gwb = SSM_GROUP_WIDTH
    zb = Z_COL // gwb
    xb = XBC_COL // gwb
    bb = (XBC_COL + SSM_WIDTH) // SSM_STATE
    cb = (XBC_COL + SSM_WIDTH + SSM_GROUPS * SSM_STATE) // SSM_STATE
    wbb = SSM_WIDTH // SSM_STATE
    wcb = wbb + SSM_GROUPS
    rowblk = lambda b, c, g: b * nchunk + c
    in_specs = [
        pl.BlockSpec((CHUNK, gwb), lambda b, c, g: (rowblk(b, c, g), zb + g)),
        pl.BlockSpec((CHUNK, gwb), lambda b, c, g: (rowblk(b, c, g), xb + g)),
        pl.BlockSpec((CHUNK, SSM_STATE), lambda b, c, g: (rowblk(b, c, g), bb + g)),
        pl.BlockSpec((CHUNK, SSM_STATE), lambda b, c, g: (rowblk(b, c, g), cb + g)),
        pl.BlockSpec((CHUNK, LANES), lambda b, c, g: (rowblk(b, c, g), g)),
        pl.BlockSpec((8, gwb), lambda b, c, g: (0, g)),
        pl.BlockSpec((8, SSM_STATE), lambda b, c, g: (0, wbb + g)),
        pl.BlockSpec((8, SSM_STATE), lambda b, c, g: (0, wcb + g)),
        pl.BlockSpec((1, gwb, SSM_STATE), lambda b, c, g: (g, 0, 0)),
        pl.BlockSpec((CONV_WIDTH, gwb), lambda b, c, g: (0, g)),
        pl.BlockSpec((CONV_WIDTH, SSM_STATE), lambda b, c, g: (0, wbb + g)),
        pl.BlockSpec((CONV_WIDTH, SSM_STATE), lambda b, c, g: (0, wcb + g)),
        pl.BlockSpec((1, gwb), lambda b, c, g: (0, g)),
        pl.BlockSpec((1, SSM_STATE), lambda b, c, g: (0, wbb + g)),
        pl.BlockSpec((1, SSM_STATE), lambda b, c, g: (0, wcb + g)),
        pl.BlockSpec((1, LANES), lambda b, c, g: (0, g)),
        pl.BlockSpec((1, LANES), lambda b, c, g: (0, g)),
        pl.BlockSpec((1, gwb), lambda b, c, g: (0, g)),
        pl.BlockSpec((1, gwb), lambda b, c, g: (0, g)),
    ]
    out_specs = [pl.BlockSpec((CHUNK, gwb), lambda b, c, g: (rowblk(b, c, g), g))]
    out_shape = [jax.ShapeDtypeStruct((m, SSM_WIDTH), bf16)]
    if emit_state:
        out_specs.append(pl.BlockSpec((1, gwb, SSM_STATE), lambda b, c, g: (g, 0, 0)))
        out_shape.append(jax.ShapeDtypeStruct((SSM_GROUPS, gwb, SSM_STATE), f32))
    return pl.pallas_call(
        functools.partial(_ssd_body, vstart=vstart, emit_state=emit_state),
        grid=(batch, nchunk, SSM_GROUPS),
        in_specs=in_specs,
        out_specs=out_specs,
        out_shape=out_shape,
        scratch_shapes=[
            pltpu.VMEM((SSM_GROUPS, gwb, SSM_STATE), f32),
            pltpu.VMEM((SSM_GROUPS, 8, gwb), f32),
            pltpu.VMEM((SSM_GROUPS, 8, SSM_STATE), f32),
            pltpu.VMEM((SSM_GROUPS, 8, SSM_STATE), f32),
            pltpu.VMEM((8 + CHUNK, gwb), f32),
            pltpu.VMEM((8 + CHUNK, SSM_STATE), f32),
            pltpu.VMEM((8 + CHUNK, SSM_STATE), f32),
        ],
        compiler_params=pltpu.CompilerParams(
            dimension_semantics=("arbitrary", "arbitrary", "arbitrary"), vmem_limit_bytes=VMEM_LIMIT),
        name="ssd_state" if emit_state else "ssd_scan",
    )(proj, proj, proj, proj, dtx, halo, halo, halo, s0, conv_w, conv_w, conv_w,
      conv_b, conv_b, conv_b, dt_bias_x, a_log_x, d_skip_ch, norm_g)


def _outproj_body(att_ref, ssm_ref, w_ref, x_ref, gi_ref, bi_ref, g1_ref, b1_ref, h_ref, hb_ref, acc,
                  *, n_att_k):
    k = pl.program_id(1)

    @pl.when(k == 0)
    def _():
        acc[...] = jnp.zeros_like(acc)

    @pl.when(k < n_att_k)
    def _():
        acc[...] += jnp.dot(att_ref[...], w_ref[...], preferred_element_type=f32)

    @pl.when(k >= n_att_k)
    def _():
        acc[...] += jnp.dot(ssm_ref[...], w_ref[...], preferred_element_type=f32)

    @pl.when(k == pl.num_programs(1) - 1)
    def _():
        h0 = _layer_norm(x_ref[...], gi_ref[...], bi_ref[...])
        h1 = _layer_norm(DEEPNORM_ALPHA * h0 + acc[...], g1_ref[...], b1_ref[...])
        h_ref[...] = h1
        hb_ref[...] = h1.astype(bf16)


def _outproj(att, ssm, w_out, x2d, gi, bi, g1, b1, *, tm, tk):
    m = att.shape[0]
    n_att_k = ATT_WIDTH // tk
    nk = (ATT_WIDTH + SSM_WIDTH) // tk
    return pl.pallas_call(
        functools.partial(_outproj_body, n_att_k=n_att_k),
        grid=(m // tm, nk),
        in_specs=[
            pl.BlockSpec((tm, tk), lambda i, k: (i, jnp.minimum(k, n_att_k - 1))),
            pl.BlockSpec((tm, tk), lambda i, k: (i, jnp.maximum(k - n_att_k, 0))),
            pl.BlockSpec((tk, D_MODEL), lambda i, k: (k, 0)),
            pl.BlockSpec((tm, D_MODEL), lambda i, k: (i, 0)),
            pl.BlockSpec((1, D_MODEL), lambda i, k: (0, 0)),
            pl.BlockSpec((1, D_MODEL), lambda i, k: (0, 0)),
            pl.BlockSpec((1, D_MODEL), lambda i, k: (0, 0)),
            pl.BlockSpec((1, D_MODEL), lambda i, k: (0, 0)),
        ],
        out_specs=[
            pl.BlockSpec((tm, D_MODEL), lambda i, k: (i, 0)),
            pl.BlockSpec((tm, D_MODEL), lambda i, k: (i, 0)),
        ],
        out_shape=[
            jax.ShapeDtypeStruct((m, D_MODEL), f32),
            jax.ShapeDtypeStruct((m, D_MODEL), bf16),
        ],
        scratch_shapes=[pltpu.VMEM((tm, D_MODEL), f32)],
        compiler_params=pltpu.CompilerParams(
            dimension_semantics=("arbitrary", "arbitrary"), vmem_limit_bytes=VMEM_LIMIT),
        name="outproj_ln1",
    )(att, ssm, w_out, x2d, gi, bi, g1, b1)


def _max01(v):
    return jnp.max(jnp.max(v, axis=1, keepdims=True), axis=0, keepdims=True)


def _router_body(h_ref, wr_ref, br_ref, tri_ref, eidx_ref, gate_ref, rank_ref, cnt_ref, running):
    tt = h_ref.shape[0]
    shape3 = (N_EXPERT_GROUPS, EXPERTS_PER_GROUP, tt)

    @pl.when(pl.program_id(0) == 0)
    def _():
        running[...] = jnp.zeros_like(running)

    logits = lax.dot_general(wr_ref[...], h_ref[...], (((1,), (1,)), ((), ())),
                             precision=lax.Precision.HIGHEST, preferred_element_type=f32)
    scores = (1.0 / (1.0 + jnp.exp(-logits)))
    sel3 = (scores + br_ref[...]).reshape(shape3)
    scores3 = scores.reshape(shape3)
    within = lax.broadcasted_iota(i32, shape3, 1).astype(f32)
    m1 = jnp.max(sel3, axis=1, keepdims=True)
    i1 = jnp.min(jnp.where(sel3 == m1, within, float(EXPERTS_PER_GROUP)), axis=1, keepdims=True)
    m2 = jnp.max(jnp.where(within == i1, -jnp.inf, sel3), axis=1, keepdims=True)
    gs = m1 + m2
    giota = lax.broadcasted_iota(i32, gs.shape, 0).astype(f32)
    gmask = jnp.zeros(gs.shape, f32)
    for _ in range(TOPK_GROUPS):
        gm = jnp.max(gs, axis=0, keepdims=True)
        gi = jnp.min(jnp.where(gs == gm, giota, float(N_EXPERT_GROUPS)), axis=0, keepdims=True)
        hit = giota == gi
        gmask = jnp.where(hit, 1.0, gmask)
        gs = jnp.where(hit, -jnp.inf, gs)
    selm = jnp.where(gmask > 0.0, sel3, -jnp.inf)
    eiota = (lax.broadcasted_iota(i32, shape3, 0) * EXPERTS_PER_GROUP
             + lax.broadcasted_iota(i32, shape3, 1)).astype(f32)
    eidx, gates, hits = [], [], []
    member = jnp.zeros(shape3, f32)
    gsum = jnp.zeros((1, 1, tt), f32)
    for _ in range(TOP_K):
        m = _max01(selm)
        ei = -_max01(-jnp.where(selm == m, eiota, float(N_EXPERTS)))
        hit = eiota == ei
        gk = jnp.sum(jnp.sum(jnp.where(hit, scores3, 0.0), axis=1, keepdims=True), axis=0, keepdims=True)
        eidx.append(ei)
        gates.append(gk)
        hits.append(hit)
        gsum = gsum + gk
        member = jnp.where(hit, 1.0, member)
        selm = jnp.where(hit, -jnp.inf, selm)
    member2 = member.reshape(N_EXPERTS, tt)
    incl = jnp.dot(member2.astype(bf16), tri_ref[...], preferred_element_type=f32)
    base = (running[...] + (incl - member2)).reshape(shape3)
    for k in range(TOP_K):
        rk = jnp.sum(jnp.sum(jnp.where(hits[k], base, 0.0), axis=1, keepdims=True), axis=0, keepdims=True)
        eidx_ref[k:k + 1, :] = eidx[k].reshape(1, tt).astype(i32)
        gate_ref[k:k + 1, :] = (gates[k] / gsum * ROUTED_SCALE).reshape(1, tt)
        rank_ref[k:k + 1, :] = rk.reshape(1, tt).astype(i32)
    total = running[...] + incl[:, tt - 1:tt]
    running[...] = total
    cnt_ref[...] = total.astype(i32)


def _router(h1, w_router_t, b_router_col, tri, *, tt):
    m = h1.shape[0]
    return pl.pallas_call(
        _router_body,
        grid=(m // tt,),
        in_specs=[
            pl.BlockSpec((tt, D_MODEL), lambda i: (i, 0)),
            pl.BlockSpec((N_EXPERTS, D_MODEL), lambda i: (0, 0)),
            pl.BlockSpec((N_EXPERTS, 1), lambda i: (0, 0)),
            pl.BlockSpec((tt, tt), lambda i: (0, 0)),
        ],
        out_specs=[
            pl.BlockSpec((TOP_K, tt), lambda i: (0, i)),
            pl.BlockSpec((TOP_K, tt), lambda i: (0, i)),
            pl.BlockSpec((TOP_K, tt), lambda i: (0, i)),
            pl.BlockSpec((N_EXPERTS, 1), lambda i: (0, 0)),
        ],
        out_shape=[
            jax.ShapeDtypeStruct((TOP_K, m), i32),
            jax.ShapeDtypeStruct((TOP_K, m), f32),
            jax.ShapeDtypeStruct((TOP_K, m), i32),
            jax.ShapeDtypeStruct((N_EXPERTS, 1), i32),
        ],
        scratch_shapes=[pltpu.VMEM((N_EXPERTS, 1), f32)],
        compiler_params=pltpu.CompilerParams(
            dimension_semantics=("arbitrary",), vmem_limit_bytes=VMEM_LIMIT),
        name="router",
    )(h1, w_router_t, b_router_col, tri)


def _experts_body(blk_e_ref, nused_ref, x_ref, wg_ref, wu_ref, wd_ref, y_ref):
    @pl.when(pl.program_id(0) < nused_ref[0])
    def _():
        x = x_ref[...]
        hg = jnp.dot(x, wg_ref[0], preferred_element_type=f32)
        hu = jnp.dot(x, wu_ref[0], preferred_element_type=f32)
        hb = (_silu(hg) * hu).astype(bf16)
        y_ref[...] = jnp.dot(hb, wd_ref[0], preferred_element_type=f32).astype(bf16)


def _experts(blk_e, nused, xs, w_gate, w_up, w_down):
    rows = xs.shape[0]
    nblk = rows // MOE_BLOCK
    grid_spec = pltpu.PrefetchScalarGridSpec(
        num_scalar_prefetch=2,
        grid=(nblk,),
        in_specs=[
            pl.BlockSpec((MOE_BLOCK, D_MODEL), lambda i, be, nu: (i, 0)),
            pl.BlockSpec((1, D_MODEL, EXPERT_DIM), lambda i, be, nu: (be[i], 0, 0)),
            pl.BlockSpec((1, D_MODEL, EXPERT_DIM), lambda i, be, nu: (be[i], 0, 0)),
            pl.BlockSpec((1, EXPERT_DIM, D_MODEL), lambda i, be, nu: (be[i], 0, 0)),
        ],
        out_specs=pl.BlockSpec((MOE_BLOCK, D_MODEL), lambda i, be, nu: (i, 0)),
    )
    return pl.pallas_call(
        _experts_body,
        grid_spec=grid_spec,
        out_shape=jax.ShapeDtypeStruct((rows, D_MODEL), bf16),
        compiler_params=pltpu.CompilerParams(
            dimension_semantics=("arbitrary",), vmem_limit_bytes=VMEM_LIMIT),
        name="routed_experts",
    )(blk_e, nused, xs, w_gate, w_up, w_down)


def _combine_body(h_ref, hb_ref, yk_ref, gate_ref, wsg_ref, wsu_ref, wsd_ref, g2_ref, b2_ref, o_ref):
    hb = hb_ref[...]
    sg = jnp.dot(hb, wsg_ref[...], preferred_element_type=f32)
    su = jnp.dot(hb, wsu_ref[...], preferred_element_type=f32)
    ffn = jnp.dot((_silu(sg) * su).astype(bf16), wsd_ref[...], preferred_element_type=f32)
    gate = gate_ref[...]
    for k in range(TOP_K):
        ffn = ffn + gate[:, k:k + 1] * yk_ref[k].astype(f32)
    o_ref[...] = _layer_norm(DEEPNORM_ALPHA * h_ref[...] + ffn, g2_ref[...], b2_ref[...])


def _combine(h1, h1b, yk, gate_tok, wsg, wsu, wsd, g2, b2, *, tm):
    m = h1.shape[0]
    return pl.pallas_call(
        _combine_body,
        grid=(m // tm,),
        in_specs=[
            pl.BlockSpec((tm, D_MODEL), lambda i: (i, 0)),
            pl.BlockSpec((tm, D_MODEL), lambda i: (i, 0)),
            pl.BlockSpec((TOP_K, tm, D_MODEL), lambda i: (0, i, 0)),
            pl.BlockSpec((tm, TOP_K), lambda i: (i, 0)),
            pl.BlockSpec((D_MODEL, EXPERT_DIM), lambda i: (0, 0)),
            pl.BlockSpec((D_MODEL, EXPERT_DIM), lambda i: (0, 0)),
            pl.BlockSpec((EXPERT_DIM, D_MODEL), lambda i: (0, 0)),
            pl.BlockSpec((1, D_MODEL), lambda i: (0, 0)),
            pl.BlockSpec((1, D_MODEL), lambda i: (0, 0)),
        ],
        out_specs=pl.BlockSpec((tm, D_MODEL), lambda i: (i, 0)),
        out_shape=jax.ShapeDtypeStruct((m, D_MODEL), f32),
        compiler_params=pltpu.CompilerParams(
            dimension_semantics=("arbitrary",), vmem_limit_bytes=VMEM_LIMIT),
        name="combine_ln2",
    )(h1, h1b, yk, gate_tok, wsg, wsu, wsd, g2, b2)


def _rope_tables(pos, width):
    half = ROPE_DIM // 2
    inv_freq = jnp.power(ROPE_THETA, -jnp.arange(0, ROPE_DIM, 2, dtype=f32) / ROPE_DIM)
    ang = pos.astype(f32)[:, None] * inv_freq[None, :]
    cos, sin = jnp.cos(ang), jnp.sin(ang)
    n = pos.shape[0]
    pad = jnp.zeros((n, HEAD_DIM - ROPE_DIM), f32)
    zero = jnp.zeros((n, half), f32)
    c = jnp.concatenate([cos, cos, pad + 1.0], axis=1)
    s1 = jnp.concatenate([-sin, zero, pad], axis=1)
    s2 = jnp.concatenate([zero, sin, pad], axis=1)
    tab = jnp.stack([c, s1, s2])
    return jnp.tile(tab, (1, 1, width // HEAD_DIM))


def _group_lanes(v):
    v = v.reshape(SSM_GROUPS, SSM_HEADS_PER_GROUP)
    return jnp.pad(v, ((0, 0), (0, LANES - SSM_HEADS_PER_GROUP))).reshape(1, DT_WIDTH)


def kernel(x, meta_tokens, ln_in_g, ln_in_b, w_in, conv_w, conv_b, dt_bias, a_log, d_skip, ssm_norm_g, att_norm_g, attn_sinks, w_out, ln1_g, ln1_b, w_router, b_router, w_gate, w_up, w_down, ws_gate, ws_up, ws_down, ln2_g, ln2_b):
    batch, seq, d = x.shape
    assert d == D_MODEL and seq % BLOCK == 0 and meta_tokens.shape == (N_META, D_MODEL)
    assert w_in.shape[0] == 1, "single layer"
    n_tok = batch * seq
    nblk = seq // BLOCK
    row = lambda v: v.reshape(1, -1).astype(f32)

    x2d = x.reshape(n_tok, D_MODEL)
    gi, bi = row(ln_in_g), row(ln_in_b)
    w_main = w_in[0, :, :MAIN_WIDTH].astype(bf16)
    w_dt = w_in[0, :, MAIN_WIDTH:].reshape(D_MODEL, SSM_GROUPS, SSM_HEADS_PER_GROUP)
    w_dt = jnp.pad(w_dt, ((0, 0), (0, 0), (0, LANES - SSM_HEADS_PER_GROUP))).reshape(D_MODEL, DT_WIDTH).astype(bf16)

    proj, dtx = _ln_inproj(x2d, gi, bi, w_main, w_dt, tm=512, tn=512)
    proj_m, dtx_m = _ln_inproj(meta_tokens.astype(f32), gi, bi, w_main, w_dt, tm=N_META, tn=512)
    proj_m = jnp.pad(proj_m, ((BLOCK - N_META, 0), (0, 0)))
    dtx_m = jnp.pad(dtx_m, ((CHUNK - N_META, 0), (0, 0)))

    gw = ATT_GROUP * HEAD_DIM
    tab = _rope_tables(N_META + jnp.arange(seq), gw)
    tab_meta = _rope_tables(jnp.arange(BLOCK), gw)
    kv_meta = jnp.roll(proj_m[:, ATT_WIDTH:ATT_WIDTH + 2 * KV_WIDTH], N_META, axis=0)
    att = _attention(proj, kv_meta, tab, tab_meta, attn_sinks[0].astype(f32), row(att_norm_g[0]),
                     batch=batch, nblk=nblk)

    conv_w0 = conv_w[0].astype(f32)
    conv_b0 = row(conv_b[0])
    dtb_x = _group_lanes(dt_bias[0].astype(f32))
    alog_x = _group_lanes(a_log[0].astype(f32))
    dskip_ch = jnp.repeat(d_skip[0].astype(f32), HEAD_DIM).reshape(1, SSM_WIDTH)
    ng = row(ssm_norm_g[0])
    zeros_halo = jnp.zeros((8, CONV_CH), f32)
    zeros_state = jnp.zeros((SSM_GROUPS, SSM_GROUP_WIDTH, SSM_STATE), f32)
    _, s_meta = _ssd(proj_m, dtx_m, zeros_halo, zeros_state, conv_w0, conv_b0, dtb_x, alog_x, dskip_ch, ng,
                     batch=1, nchunk=1, vstart=CHUNK - N_META, emit_state=True)
    halo = proj_m[CHUNK - 8:, XBC_COL:].astype(f32)
    (ssm,) = _ssd(proj, dtx, halo, s_meta, conv_w0, conv_b0, dtb_x, alog_x, dskip_ch, ng,
                  batch=batch, nchunk=nblk, vstart=0, emit_state=False)

    h1, h1b = _outproj(att, ssm, w_out[0].astype(bf16), x2d, gi, bi, row(ln1_g[0]), row(ln1_b[0]),
                       tm=512, tk=1024)

    tt = 512
    tri = (jnp.arange(tt)[:, None] <= jnp.arange(tt)[None, :]).astype(bf16)
    eidx, gate, rank, counts = _router(h1, w_router[0].T.astype(f32), b_router[0].reshape(N_EXPERTS, 1).astype(f32),
                                       tri, tt=tt)
    counts = counts.reshape(N_EXPERTS)
    padded = (counts + MOE_BLOCK - 1) // MOE_BLOCK * MOE_BLOCK
    pend = jnp.cumsum(padded)
    pstart = pend - padded
    dest = jnp.take(pstart, eidx) + rank
    n_blocks = n_tok * TOP_K // MOE_BLOCK + N_EXPERTS
    blk_e = jnp.minimum(jnp.searchsorted(pend, jnp.arange(n_blocks, dtype=i32) * MOE_BLOCK, side='right'),
                        N_EXPERTS - 1).astype(i32)
    nused = (pend[-1] // MOE_BLOCK).astype(i32).reshape(1)

    rows = n_blocks * MOE_BLOCK
    tok_of_row = jnp.zeros((rows,), i32).at[dest.reshape(-1)].set(
        jnp.tile(jnp.arange(n_tok, dtype=i32), TOP_K))
    xs = h1b[tok_of_row]
    ys = _experts(blk_e, nused, xs, w_gate[0].astype(bf16), w_up[0].astype(bf16), w_down[0].astype(bf16))
    yk = ys[dest]

    out = _combine(h1, h1b, yk, gate.T, ws_gate[0].astype(bf16), ws_up[0].astype(bf16),
                   ws_down[0].astype(bf16), row(ln2_g[0]), row(ln2_b[0]), tm=256)
    return out.reshape(batch, seq, D_MODEL)
```

```python
import functools
import math

import jax
import jax.numpy as jnp
from jax import lax
from jax.experimental import pallas as pl
from jax.experimental.pallas import tpu as pltpu

f32 = jnp.float32
bf16 = jnp.bfloat16
i32 = jnp.int32

D_MODEL = 2048
N_META = 16
HEAD_DIM = 64
ATT_HEADS = 32
ATT_KV_HEADS = 4
ATT_GROUP = ATT_HEADS // ATT_KV_HEADS
ATT_WIDTH = 2048
KV_WIDTH = 256
BLOCK = 128
ROPE_DIM = 16
ROPE_THETA = 500000.0
SSM_WIDTH = 4096
SSM_HEADS = 64
SSM_GROUPS = 8
SSM_HEADS_PER_GROUP = SSM_HEADS // SSM_GROUPS
SSM_STATE = 128
SSM_GROUP_WIDTH = SSM_WIDTH // SSM_GROUPS
CONV_WIDTH = 4
CHUNK = 128
CONV_CH = SSM_WIDTH + 2 * SSM_GROUPS * SSM_STATE
MAIN_WIDTH = ATT_WIDTH + 2 * KV_WIDTH + SSM_WIDTH + CONV_CH
Z_COL = ATT_WIDTH + 2 * KV_WIDTH
XBC_COL = Z_COL + SSM_WIDTH
N_EXPERTS = 64
EXPERT_DIM = 512
TOP_K = 8
N_EXPERT_GROUPS = 8
EXPERTS_PER_GROUP = N_EXPERTS // N_EXPERT_GROUPS
TOPK_GROUPS = 4
ROUTED_SCALE = 2.5
MOE_BLOCK = 256
DEEPNORM_ALPHA = 2.0 ** 0.25
LN_EPS = 1e-5
RMS_EPS = 1e-6
NEG_INF = -1e30
LANES = 128
SUBLANES = 8
DT_WIDTH = SSM_GROUPS * LANES
VMEM_LIMIT = 56 * 1024 * 1024


def _layer_norm(x, g, b):
    mu = jnp.mean(x, axis=-1, keepdims=True)
    xc = x - mu
    var = jnp.mean(xc * xc, axis=-1, keepdims=True)
    return xc * lax.rsqrt(var + LN_EPS) * g + b


def _silu(x):
    return x * (1.0 / (1.0 + jnp.exp(-x)))


def _softplus(x):
    return jnp.maximum(x, 0.0) + jnp.log1p(jnp.exp(-jnp.abs(x)))


def _ln_inproj_body(x_ref, g_ref, b_ref, w_ref, wdt_ref, o_ref, dt_ref, h_scr):
    @pl.when(pl.program_id(1) == 0)
    def _():
        h = _layer_norm(x_ref[...], g_ref[...], b_ref[...]).astype(bf16)
        h_scr[...] = h
        dt_ref[...] = jnp.dot(h, wdt_ref[...], preferred_element_type=f32)

    o_ref[...] = jnp.dot(h_scr[...], w_ref[...], preferred_element_type=f32).astype(bf16)


def _ln_inproj(x2d, g, b, w_main, w_dt, *, tm, tn):
    m = x2d.shape[0]
    return pl.pallas_call(
        _ln_inproj_body,
        grid=(m // tm, MAIN_WIDTH // tn),
        in_specs=[
            pl.BlockSpec((tm, D_MODEL), lambda i, j: (i, 0)),
            pl.BlockSpec((1, D_MODEL), lambda i, j: (0, 0)),
            pl.BlockSpec((1, D_MODEL), lambda i, j: (0, 0)),
            pl.BlockSpec((D_MODEL, tn), lambda i, j: (0, j)),
            pl.BlockSpec((D_MODEL, DT_WIDTH), lambda i, j: (0, 0)),
        ],
        out_specs=[
            pl.BlockSpec((tm, tn), lambda i, j: (i, j)),
            pl.BlockSpec((tm, DT_WIDTH), lambda i, j: (i, 0)),
        ],
        out_shape=[
            jax.ShapeDtypeStruct((m, MAIN_WIDTH), bf16),
            jax.ShapeDtypeStruct((m, DT_WIDTH), f32),
        ],
        scratch_shapes=[pltpu.VMEM((tm, D_MODEL), bf16)],
        compiler_params=pltpu.CompilerParams(
            dimension_semantics=("arbitrary", "arbitrary"), vmem_limit_bytes=VMEM_LIMIT),
        name="ln_inproj",
    )(x2d, g, b, w_main, w_dt)


def _rotate(t, tab):
    w = t.shape[-1]
    half = ROPE_DIM // 2
    return (t * tab[0]
            + pltpu.roll(t, w - half, 1) * tab[1]
            + pltpu.roll(t, half, 1) * tab[2])


def _attention_body(sink_ref, q_ref, kvc_ref, kvp_ref, kvm_ref, tabc_ref, tabp_ref, tabm_ref,
                    g_ref, o_ref):
    j = pl.program_id(1)
    tabc = tabc_ref[...]
    kc = _rotate(kvc_ref[:, :KV_WIDTH].astype(f32), tabc[:, :, :KV_WIDTH]).astype(bf16)
    kp = _rotate(kvp_ref[:, :KV_WIDTH].astype(f32), tabp_ref[:, :, :KV_WIDTH]).astype(bf16)
    km = _rotate(kvm_ref[:, :KV_WIDTH].astype(f32), tabm_ref[:, :, :KV_WIDTH]).astype(bf16)
    k_all = jnp.concatenate([kp, kc, km], axis=0)
    v_all = jnp.concatenate([kvp_ref[:, KV_WIDTH:], kvc_ref[:, KV_WIDTH:], kvm_ref[:, KV_WIDTH:]], axis=0)

    rows = ATT_GROUP * BLOCK
    r = lax.broadcasted_iota(i32, (rows, 3 * BLOCK), 0) % BLOCK
    c = lax.broadcasted_iota(i32, (rows, 3 * BLOCK), 1)
    mask = (((c < BLOCK) & (c > r) & (j > 0))
            | ((c >= BLOCK) & (c - BLOCK <= r))
            | ((c >= 2 * BLOCK) & (c < 2 * BLOCK + N_META)))
    head_of_row = lax.broadcasted_iota(i32, (rows, 1), 0) // BLOCK

    outs = []
    for g in range(ATT_KV_HEADS):
        gw = ATT_GROUP * HEAD_DIM
        qg = _rotate(q_ref[:, g * gw:(g + 1) * gw].astype(f32), tabc) * (HEAD_DIM ** -0.5)
        qs = jnp.concatenate([qg[:, h * HEAD_DIM:(h + 1) * HEAD_DIM] for h in range(ATT_GROUP)],
                             axis=0).astype(bf16)
        kg = k_all[:, g * HEAD_DIM:(g + 1) * HEAD_DIM]
        vg = v_all[:, g * HEAD_DIM:(g + 1) * HEAD_DIM]
        s = lax.dot_general(qs, kg, (((1,), (1,)), ((), ())), preferred_element_type=f32)
        s = jnp.where(mask, s, NEG_INF)
        sink = jnp.zeros((rows, 1), f32)
        for h in range(ATT_GROUP):
            sink = jnp.where(head_of_row == h, sink_ref[g * ATT_GROUP + h], sink)
        m = jnp.maximum(jnp.max(s, axis=1, keepdims=True), sink)
        p = jnp.exp(s - m)
        denom = jnp.sum(p, axis=1, keepdims=True) + jnp.exp(sink - m)
        o = jnp.dot(p.astype(bf16), vg, preferred_element_type=f32) / denom
        outs.append(jnp.concatenate([o[h * BLOCK:(h + 1) * BLOCK] for h in range(ATT_GROUP)], axis=1))
    att = jnp.concatenate(outs, axis=1)
    att = att * lax.rsqrt(jnp.mean(att * att, axis=-1, keepdims=True) + RMS_EPS) * g_ref[...]
    o_ref[...] = att.astype(bf16)


def _attention(proj, kv_meta, tab, tab_meta, sinks, att_g, *, batch, nblk):
    m = proj.shape[0]
    kvb = ATT_WIDTH // (2 * KV_WIDTH)
    gw = ATT_GROUP * HEAD_DIM
    return pl.pallas_call(
        _attention_body,
        grid=(batch, nblk),
        in_specs=[
            pl.BlockSpec(memory_space=pltpu.SMEM),
            pl.BlockSpec((BLOCK, ATT_WIDTH), lambda b, j: (b * nblk + j, 0)),
            pl.BlockSpec((BLOCK, 2 * KV_WIDTH), lambda b, j: (b * nblk + j, kvb)),
            pl.BlockSpec((BLOCK, 2 * KV_WIDTH), lambda b, j: (b * nblk + jnp.maximum(j - 1, 0), kvb)),
            pl.BlockSpec((BLOCK, 2 * KV_WIDTH), lambda b, j: (0, 0)),
            pl.BlockSpec((3, BLOCK, gw), lambda b, j: (0, j, 0)),
            pl.BlockSpec((3, BLOCK, gw), lambda b, j: (0, jnp.maximum(j - 1, 0), 0)),
            pl.BlockSpec((3, BLOCK, gw), lambda b, j: (0, 0, 0)),
            pl.BlockSpec((1, ATT_WIDTH), lambda b, j: (0, 0)),
        ],
        out_specs=pl.BlockSpec((BLOCK, ATT_WIDTH), lambda b, j: (b * nblk + j, 0)),
        out_shape=jax.ShapeDtypeStruct((m, ATT_WIDTH), bf16),
        compiler_params=pltpu.CompilerParams(
            dimension_semantics=("arbitrary", "arbitrary"), vmem_limit_bytes=VMEM_LIMIT),
        name="swa_attention",
    )(sinks, proj, proj, proj, kv_meta, tab, tab, tab_meta, att_g)


def _cumsum_rows(x):
    n = x.shape[0]
    row = lax.broadcasted_iota(i32, x.shape, 0)
    k = 1
    while k < n:
        x = x + jnp.where(row >= k, pltpu.roll(x, k, 0), 0.0)
        k *= 2
    return x


def _expand_heads(col):
    rows = col.shape[0]
    lane_head = lax.broadcasted_iota(i32, (rows, SSM_GROUP_WIDTH), 1) // HEAD_DIM
    out = jnp.zeros((rows, SSM_GROUP_WIDTH), f32)
    for i in range(SSM_HEADS_PER_GROUP):
        out = jnp.where(lane_head == i, col[:, i:i + 1], out)
    return out


def _ssd_body(z_ref, x_ref, b_ref, c_ref, dt_ref, hx_ref, hb_ref, hc_ref, s0_ref,
              wx_ref, wb_ref, wc_ref, bx_ref, bb_ref, bc_ref, dtb_ref, alog_ref, dskip_ref, ng_ref,
              *rest, vstart, emit_state):
    if emit_state:
        y_ref, sout_ref, state, halo_x, halo_b, halo_c, work_x, work_b, work_c = rest
    else:
        y_ref, state, halo_x, halo_b, halo_c, work_x, work_b, work_c = rest
    c_idx = pl.program_id(1)
    g = pl.program_id(2)

    @pl.when(c_idx == 0)
    def _():
        state[g] = s0_ref[0]
        halo_x[g] = hx_ref[...]
        halo_b[g] = hb_ref[...]
        halo_c[g] = hc_ref[...]

    row = lax.broadcasted_iota(i32, (CHUNK, 1), 0)
    valid = row >= vstart

    def conv(u_ref, halo, work, w_ref, bias_ref):
        u = u_ref[...].astype(f32)
        work[0:8] = halo[g]
        work[8:8 + CHUNK] = u
        w = w_ref[...]
        acc = bias_ref[...] + u * w[3:4]
        for jj in range(CONV_WIDTH - 1):
            acc = acc + work[5 + jj:5 + jj + CHUNK] * w[jj:jj + 1]
        halo[g] = u[CHUNK - 8:]
        return jnp.where(valid, _silu(acc), 0.0)

    xs = conv(x_ref, halo_x, work_x, wx_ref, bx_ref)
    bm = conv(b_ref, halo_b, work_b, wb_ref, bb_ref)
    cm = conv(c_ref, halo_c, work_c, wc_ref, bc_ref)

    dt = jnp.where(valid, _softplus(dt_ref[...] + dtb_ref[...]), 0.0)
    a = -jnp.exp(alog_ref[...])
    a_cs = _cumsum_rows(dt * a)
    a_cs_t = a_cs.T
    a_last = a_cs[CHUNK - 1:CHUNK]

    xdt = xs * _expand_heads(dt)
    bmb = bm.astype(bf16)
    cmb = cm.astype(bf16)
    cb = lax.dot_general(cmb, bmb, (((1,), (1,)), ((), ())), preferred_element_type=f32)
    causal = (lax.broadcasted_iota(i32, (CHUNK, CHUNK), 0) >= lax.broadcasted_iota(i32, (CHUNK, CHUNK), 1))
    xdtb = xdt.astype(bf16)
    ys = []
    for i in range(SSM_HEADS_PER_GROUP):
        seg = a_cs[:, i:i + 1] - a_cs_t[i:i + 1, :]
        lmat = (cb * jnp.exp(jnp.where(causal, seg, -jnp.inf))).astype(bf16)
        ys.append(jnp.dot(lmat, xdtb[:, i * HEAD_DIM:(i + 1) * HEAD_DIM], preferred_element_type=f32))
    y = jnp.concatenate(ys, axis=1)

    st = state[g]
    y_off = lax.dot_general(cmb, st.astype(bf16), (((1,), (1,)), ((), ())), preferred_element_type=f32)
    y = y + y_off * _expand_heads(jnp.exp(a_cs))

    x_end = (xdt * _expand_heads(jnp.exp(a_last - a_cs))).T.astype(bf16)
    row_head = lax.broadcasted_iota(i32, (SSM_GROUP_WIDTH, 1), 0) // HEAD_DIM
    decay_rows = jnp.zeros((SSM_GROUP_WIDTH, 1), f32)
    for i in range(SSM_HEADS_PER_GROUP):
        decay_rows = jnp.where(row_head == i, a_cs_t[i:i + 1, CHUNK - 1:CHUNK], decay_rows)
    new_state = st * jnp.exp(decay_rows) + jnp.dot(x_end, bmb, preferred_element_type=f32)
    state[g] = new_state
    if emit_state:
        sout_ref[0] = new_state

    y = y + dskip_ref[...] * xs
    y = y * _silu(z_ref[...].astype(f32))
    y = y * lax.rsqrt(jnp.mean(y * y, axis=-1, keepdims=True) + RMS_EPS) * ng_ref[...]
    y_ref[...] = y.astype(bf16)


def _ssd(proj, dtx, halo, s0, conv_w, conv_b, dt_bias_x, a_log_x, d_skip_ch, norm_g,
         *, batch, nchunk, vstart, emit_state):
    m = proj.shape[0]
    gwb = SSM_GROUP_WIDTH
    zb = Z_COL // gwb
    xb = XBC_COL // gwb
    bb = (XBC_COL + SSM_WIDTH) // SSM_STATE
    cb = (XBC_COL + SSM_WIDTH + SSM_GROUPS * SSM_STATE) // SSM_STATE
    wbb = SSM_WIDTH // SSM_STATE
    wcb = wbb + SSM_GROUPS
    rowblk = lambda b, c, g: b * nchunk + c
    in_specs = [
        pl.BlockSpec((CHUNK, gwb), lambda b, c, g: (rowblk(b, c, g), zb + g)),
        pl.BlockSpec((CHUNK, gwb), lambda b, c, g: (rowblk(b, c, g), xb + g)),
        pl.BlockSpec((CHUNK, SSM_STATE), lambda b, c, g: (rowblk(b, c, g), bb + g)),
        pl.BlockSpec((CHUNK, SSM_STATE), lambda b, c, g: (rowblk(b, c, g), cb + g)),
        pl.BlockSpec((CHUNK, LANES), lambda b, c, g: (rowblk(b, c, g), g)),
        pl.BlockSpec((8, gwb), lambda b, c, g: (0, g)),
        pl.BlockSpec((8, SSM_STATE), lambda b, c, g: (0, wbb + g)),
        pl.BlockSpec((8, SSM_STATE), lambda b, c, g: (0, wcb + g)),
        pl.BlockSpec((1, gwb, SSM_STATE), lambda b, c, g: (g, 0, 0)),
        pl.BlockSpec((CONV_WIDTH, gwb), lambda b, c, g: (0, g)),
        pl.BlockSpec((CONV_WIDTH, SSM_STATE), lambda b, c, g: (0, wbb + g)),
        pl.BlockSpec((CONV_WIDTH, SSM_STATE), lambda b, c, g: (0, wcb + g)),
        pl.BlockSpec((1, gwb), lambda b, c, g: (0, g)),
        pl.BlockSpec((1, SSM_STATE), lambda b, c, g: (0, wbb + g)),
        pl.BlockSpec((1, SSM_STATE), lambda b, c, g: (0, wcb + g)),
        pl.BlockSpec((1, LANES), lambda b, c, g: (0, g)),
        pl.BlockSpec((1, LANES), lambda b, c, g: (0, g)),
        pl.BlockSpec((1, gwb), lambda b, c, g: (0, g)),
        pl.BlockSpec((1, gwb), lambda b, c, g: (0, g)),
    ]
    out_specs = [pl.BlockSpec((CHUNK, gwb), lambda b, c, g: (rowblk(b, c, g), g))]
    out_shape = [jax.ShapeDtypeStruct((m, SSM_WIDTH), bf16)]
    if emit_state:
        out_specs.append(pl.BlockSpec((1, gwb, SSM_STATE), lambda b, c, g: (g, 0, 0)))
        out_shape.append(jax.ShapeDtypeStruct((SSM_GROUPS, gwb, SSM_STATE), f32))
    return pl.pallas_call(
        functools.partial(_ssd_body, vstart=vstart, emit_state=emit_state),
        grid=(batch, nchunk, SSM_GROUPS),
        in_specs=in_specs,
        out_specs=out_specs,
        out_shape=out_shape,
        scratch_shapes=[
            pltpu.VMEM((SSM_GROUPS, gwb, SSM_STATE), f32),
            pltpu.VMEM((SSM_GROUPS, 8, gwb), f32),
            pltpu.VMEM((SSM_GROUPS, 8, SSM_STATE), f32),
            pltpu.VMEM((SSM_GROUPS, 8, SSM_STATE), f32),
            pltpu.VMEM((8 + CHUNK, gwb), f32),
            pltpu.VMEM((8 + CHUNK, SSM_STATE), f32),
            pltpu.VMEM((8 + CHUNK, SSM_STATE), f32),
        ],
        compiler_params=pltpu.CompilerParams(
            dimension_semantics=("arbitrary", "arbitrary", "arbitrary"), vmem_limit_bytes=VMEM_LIMIT),
        name="ssd_state" if emit_state else "ssd_scan",
    )(proj, proj, proj, proj, dtx, halo, halo, halo, s0, conv_w, conv_w, conv_w,
      conv_b, conv_b, conv_b, dt_bias_x, a_log_x, d_skip_ch, norm_g)


def _outproj_body(att_ref, ssm_ref, w_ref, x_ref, gi_ref, bi_ref, g1_ref, b1_ref, h_ref, acc,
                  *, n_att_k):
    k = pl.program_id(1)

    @pl.when(k == 0)
    def _():
        acc[...] = jnp.zeros_like(acc)

    @pl.when(k < n_att_k)
    def _():
        acc[...] += jnp.dot(att_ref[...], w_ref[...], preferred_element_type=f32)

    @pl.when(k >= n_att_k)
    def _():
        acc[...] += jnp.dot(ssm_ref[...], w_ref[...], preferred_element_type=f32)

    @pl.when(k == pl.num_programs(1) - 1)
    def _():
        h0 = _layer_norm(x_ref[...], gi_ref[...], bi_ref[...])
        h_ref[...] = _layer_norm(DEEPNORM_ALPHA * h0 + acc[...], g1_ref[...], b1_ref[...])


def _outproj(att, ssm, w_out, x2d, gi, bi, g1, b1, *, tm, tk):
    m = att.shape[0]
    n_att_k = ATT_WIDTH // tk
    nk = (ATT_WIDTH + SSM_WIDTH) // tk
    return pl.pallas_call(
        functools.partial(_outproj_body, n_att_k=n_att_k),
        grid=(m // tm, nk),
        in_specs=[
            pl.BlockSpec((tm, tk), lambda i, k: (i, jnp.minimum(k, n_att_k - 1))),
            pl.BlockSpec((tm, tk), lambda i, k: (i, jnp.maximum(k - n_att_k, 0))),
            pl.BlockSpec((tk, D_MODEL), lambda i, k: (k, 0)),
            pl.BlockSpec((tm, D_MODEL), lambda i, k: (i, 0)),
            pl.BlockSpec((1, D_MODEL), lambda i, k: (0, 0)),
            pl.BlockSpec((1, D_MODEL), lambda i, k: (0, 0)),
            pl.BlockSpec((1, D_MODEL), lambda i, k: (0, 0)),
            pl.BlockSpec((1, D_MODEL), lambda i, k: (0, 0)),
        ],
        out_specs=pl.BlockSpec((tm, D_MODEL), lambda i, k: (i, 0)),
        out_shape=jax.ShapeDtypeStruct((m, D_MODEL), f32),
        scratch_shapes=[pltpu.VMEM((tm, D_MODEL), f32)],
        compiler_params=pltpu.CompilerParams(
            dimension_semantics=("arbitrary", "arbitrary"), vmem_limit_bytes=VMEM_LIMIT),
        name="outproj_ln1",
    )(att, ssm, w_out, x2d, gi, bi, g1, b1)


def _max01(v):
    return jnp.max(jnp.max(v, axis=1, keepdims=True), axis=0, keepdims=True)


def _router_body(h_ref, wr_ref, br_ref, tri_ref, eidx_ref, gate_ref, rank_ref, cnt_ref, running):
    tt = h_ref.shape[0]
    shape3 = (N_EXPERT_GROUPS, EXPERTS_PER_GROUP, tt)

    @pl.when(pl.program_id(0) == 0)
    def _():
        running[...] = jnp.zeros_like(running)

    logits = lax.dot_general(wr_ref[...], h_ref[...], (((1,), (1,)), ((), ())),
                             precision=lax.Precision.HIGHEST, preferred_element_type=f32)
    scores = (1.0 / (1.0 + jnp.exp(-logits)))
    sel3 = (scores + br_ref[...]).reshape(shape3)
    scores3 = scores.reshape(shape3)
    within = lax.broadcasted_iota(i32, shape3, 1).astype(f32)
    m1 = jnp.max(sel3, axis=1, keepdims=True)
    i1 = jnp.min(jnp.where(sel3 == m1, within, float(EXPERTS_PER_GROUP)), axis=1, keepdims=True)
    m2 = jnp.max(jnp.where(within == i1, -jnp.inf, sel3), axis=1, keepdims=True)
    gs = m1 + m2
    giota = lax.broadcasted_iota(i32, gs.shape, 0).astype(f32)
    gmask = jnp.zeros(gs.shape, f32)
    for _ in range(TOPK_GROUPS):
        gm = jnp.max(gs, axis=0, keepdims=True)
        gi = jnp.min(jnp.where(gs == gm, giota, float(N_EXPERT_GROUPS)), axis=0, keepdims=True)
        hit = giota == gi
        gmask = jnp.where(hit, 1.0, gmask)
        gs = jnp.where(hit, -jnp.inf, gs)
    selm = jnp.where(gmask > 0.0, sel3, -jnp.inf)
    eiota = (lax.broadcasted_iota(i32, shape3, 0) * EXPERTS_PER_GROUP
             + lax.broadcasted_iota(i32, shape3, 1)).astype(f32)
    eidx, gates, hits = [], [], []
    member = jnp.zeros(shape3, f32)
    gsum = jnp.zeros((1, 1, tt), f32)
    for _ in range(TOP_K):
        m = _max01(selm)
        ei = -_max01(-jnp.where(selm == m, eiota, float(N_EXPERTS)))
        hit = eiota == ei
        gk = jnp.sum(jnp.sum(jnp.where(hit, scores3, 0.0), axis=1, keepdims=True), axis=0, keepdims=True)
        eidx.append(ei)
        gates.append(gk)
        hits.append(hit)
        gsum = gsum + gk
        member = jnp.where(hit, 1.0, member)
        selm = jnp.where(hit, -jnp.inf, selm)
    member2 = member.reshape(N_EXPERTS, tt)
    incl = jnp.dot(member2.astype(bf16), tri_ref[...], preferred_element_type=f32)
    base = (running[...] + (incl - member2)).reshape(shape3)
    for k in range(TOP_K):
        rk = jnp.sum(jnp.sum(jnp.where(hits[k], base, 0.0), axis=1, keepdims=True), axis=0, keepdims=True)
        eidx_ref[k:k + 1, :] = eidx[k].reshape(1, tt).astype(i32)
        gate_ref[k:k + 1, :] = (gates[k] / gsum * ROUTED_SCALE).reshape(1, tt)
        rank_ref[k:k + 1, :] = rk.reshape(1, tt).astype(i32)
    total = running[...] + incl[:, tt - 1:tt]
    running[...] = total
    cnt_ref[...] = total.astype(i32)


def _router(h1, w_router_t, b_router_col, tri, *, tt):
    m = h1.shape[0]
    return pl.pallas_call(
        _router_body,
        grid=(m // tt,),
        in_specs=[
            pl.BlockSpec((tt, D_MODEL), lambda i: (i, 0)),
            pl.BlockSpec((N_EXPERTS, D_MODEL), lambda i: (0, 0)),
            pl.BlockSpec((N_EXPERTS, 1), lambda i: (0, 0)),
            pl.BlockSpec((tt, tt), lambda i: (0, 0)),
        ],
        out_specs=[
            pl.BlockSpec((TOP_K, tt), lambda i: (0, i)),
            pl.BlockSpec((TOP_K, tt), lambda i: (0, i)),
            pl.BlockSpec((TOP_K, tt), lambda i: (0, i)),
            pl.BlockSpec((N_EXPERTS, 1), lambda i: (0, 0)),
        ],
        out_shape=[
            jax.ShapeDtypeStruct((TOP_K, m), i32),
            jax.ShapeDtypeStruct((TOP_K, m), f32),
            jax.ShapeDtypeStruct((TOP_K, m), i32),
            jax.ShapeDtypeStruct((N_EXPERTS, 1), i32),
        ],
        scratch_shapes=[pltpu.VMEM((N_EXPERTS, 1), f32)],
        compiler_params=pltpu.CompilerParams(
            dimension_semantics=("arbitrary",), vmem_limit_bytes=VMEM_LIMIT),
        name="router",
    )(h1, w_router_t, b_router_col, tri)


def _row_copy(src_ref, src_row, dst_ref, dst_row, sem):
    return pltpu.make_async_copy(src_ref.at[pl.ds(src_row, 1)], dst_ref.at[pl.ds(dst_row, 1)], sem)


FILL_SIZES = (128, 64, 32, 16, 8)


def _dispatch_body(dest_ref, cnt_ref, pstart_ref, padded_ref, h_ref, xs_ref, zbuf, sem, zsem, *, tt):
    i = pl.program_id(0)
    base = i * tt

    @pl.when(i == 0)
    def _():
        zbuf[...] = jnp.zeros_like(zbuf)

        def fill(e, wait):
            cnt = cnt_ref[e]
            first = pstart_ref[e] + cnt
            aligned = (first + SUBLANES - 1) // SUBLANES * SUBLANES
            head = aligned - first
            body = pstart_ref[e] + padded_ref[e] - aligned

            def copy(start, size):
                cp = pltpu.make_async_copy(zbuf.at[pl.ds(0, size)], xs_ref.at[pl.ds(start, size)], zsem)
                if wait:
                    cp.wait()
                else:
                    cp.start()

            for r in range(SUBLANES - 1):
                @pl.when(r < head)
                def _():
                    copy(first + r, 1)
            for size in FILL_SIZES:
                @pl.when((body & size) != 0)
                def _():
                    copy(pl.multiple_of(aligned + (body & (-2 * size)), SUBLANES), size)

        def start_fill(e, c):
            fill(e, False)
            return c

        def wait_fill(e, c):
            fill(e, True)
            return c

        lax.fori_loop(0, N_EXPERTS, start_fill, 0)
        lax.fori_loop(0, N_EXPERTS, wait_fill, 0)

    def issue(t, c):
        for k in range(TOP_K):
            _row_copy(h_ref, base + t, xs_ref, dest_ref[0, k, t], sem).start()
        return c

    def drain(t, c):
        for k in range(TOP_K):
            _row_copy(h_ref, 0, xs_ref, 0, sem).wait()
        return c

    lax.fori_loop(0, tt, issue, 0, unroll=4)
    lax.fori_loop(0, tt, drain, 0, unroll=4)


def _dispatch(dest3, counts, pstart, padded, h1, *, rows, tt):
    m = h1.shape[0]
    smem = pl.BlockSpec(memory_space=pltpu.SMEM)
    return pl.pallas_call(
        functools.partial(_dispatch_body, tt=tt),
        grid=(m // tt,),
        in_specs=[
            pl.BlockSpec((1, TOP_K, tt), lambda i: (i, 0, 0), memory_space=pltpu.SMEM),
            smem, smem, smem,
            pl.BlockSpec(memory_space=pl.ANY),
        ],
        out_specs=pl.BlockSpec(memory_space=pl.ANY),
        out_shape=jax.ShapeDtypeStruct((rows, D_MODEL), f32),
        scratch_shapes=[
            pltpu.VMEM((FILL_SIZES[0], D_MODEL), f32),
            pltpu.SemaphoreType.DMA(()),
            pltpu.SemaphoreType.DMA(()),
        ],
        compiler_params=pltpu.CompilerParams(
            dimension_semantics=("arbitrary",), vmem_limit_bytes=VMEM_LIMIT, has_side_effects=True),
        name="moe_dispatch",
    )(dest3, counts, pstart, padded, h1)


def _experts_body(blk_e_ref, nused_ref, x_ref, wg_ref, wu_ref, wd_ref, y_ref, wgb, wub, wdb):
    i = pl.program_id(0)

    @pl.when(i < nused_ref[0])
    def _():
        e = blk_e_ref[i]
        prev = blk_e_ref[jnp.maximum(i - 1, 0)]

        @pl.when((i == 0) | (e != prev))
        def _():
            wgb[...] = wg_ref[0].astype(bf16)
            wub[...] = wu_ref[0].astype(bf16)
            wdb[...] = wd_ref[0].astype(bf16)

        x = x_ref[...].astype(bf16)
        hg = jnp.dot(x, wgb[...], preferred_element_type=f32)
        hu = jnp.dot(x, wub[...], preferred_element_type=f32)
        hb = (_silu(hg) * hu).astype(bf16)
        y_ref[...] = jnp.dot(hb, wdb[...], preferred_element_type=f32)


def _experts(blk_e, nused, xs, w_gate, w_up, w_down):
    rows = xs.shape[0]
    nblk = rows // MOE_BLOCK
    blk = lambda i, be, nu: jnp.minimum(i, nu[0] - 1)
    grid_spec = pltpu.PrefetchScalarGridSpec(
        num_scalar_prefetch=2,
        grid=(nblk,),
        in_specs=[
            pl.BlockSpec((MOE_BLOCK, D_MODEL), lambda i, be, nu: (blk(i, be, nu), 0)),
            pl.BlockSpec((1, D_MODEL, EXPERT_DIM), lambda i, be, nu: (be[i], 0, 0)),
            pl.BlockSpec((1, D_MODEL, EXPERT_DIM), lambda i, be, nu: (be[i], 0, 0)),
            pl.BlockSpec((1, EXPERT_DIM, D_MODEL), lambda i, be, nu: (be[i], 0, 0)),
        ],
        out_specs=pl.BlockSpec((MOE_BLOCK, D_MODEL), lambda i, be, nu: (blk(i, be, nu), 0)),
        scratch_shapes=[
            pltpu.VMEM((D_MODEL, EXPERT_DIM), bf16),
            pltpu.VMEM((D_MODEL, EXPERT_DIM), bf16),
            pltpu.VMEM((EXPERT_DIM, D_MODEL), bf16),
        ],
    )
    return pl.pallas_call(
        _experts_body,
        grid_spec=grid_spec,
        out_shape=jax.ShapeDtypeStruct((rows, D_MODEL), f32),
        compiler_params=pltpu.CompilerParams(
            dimension_semantics=("arbitrary",), vmem_limit_bytes=VMEM_LIMIT),
        name="routed_experts",
    )(blk_e, nused, xs, w_gate, w_up, w_down)


def _combine_body(dest_ref, destn_ref, h_ref, gate_ref, wsg_ref, wsu_ref, wsd_ref, g2_ref, b2_ref, ys_ref,
                  o_ref, buf, sem, *, tm):
    i = pl.program_id(0)
    slot = i % 2

    def issue(d_ref, s):
        def body(t, c):
            for k in range(TOP_K):
                _row_copy(ys_ref, d_ref[0, k, t], buf.at[s, k], t, sem.at[s]).start()
            return c
        lax.fori_loop(0, tm, body, 0, unroll=4)

    @pl.when(i == 0)
    def _():
        issue(dest_ref, 0)

    @pl.when(i + 1 < pl.num_programs(0))
    def _():
        issue(destn_ref, 1 - slot)

    h = h_ref[...]
    hb = h.astype(bf16)
    sg = jnp.dot(hb, wsg_ref[...], preferred_element_type=f32)
    su = jnp.dot(hb, wsu_ref[...], preferred_element_type=f32)
    ffn = jnp.dot((_silu(sg) * su).astype(bf16), wsd_ref[...], preferred_element_type=f32)

    def drain(t, c):
        for k in range(TOP_K):
            _row_copy(ys_ref, 0, buf.at[slot, k], 0, sem.at[slot]).wait()
        return c
    lax.fori_loop(0, tm, drain, 0, unroll=4)

    gate = gate_ref[...]
    for k in range(TOP_K):
        ffn = ffn + gate[:, k:k + 1] * buf[slot, k]
    o_ref[...] = _layer_norm(DEEPNORM_ALPHA * h + ffn, g2_ref[...], b2_ref[...])


def _combine(dest3, h1, ys, gate_tok, wsg, wsu, wsd, g2, b2, *, tm):
    m = h1.shape[0]
    nt = m // tm
    return pl.pallas_call(
        functools.partial(_combine_body, tm=tm),
        grid=(nt,),
        in_specs=[
            pl.BlockSpec((1, TOP_K, tm), lambda i: (i, 0, 0), memory_space=pltpu.SMEM),
            pl.BlockSpec((1, TOP_K, tm), lambda i: (jnp.minimum(i + 1, nt - 1), 0, 0), memory_space=pltpu.SMEM),
            pl.BlockSpec((tm, D_MODEL), lambda i: (i, 0)),
            pl.BlockSpec((tm, TOP_K), lambda i: (i, 0)),
            pl.BlockSpec((D_MODEL, EXPERT_DIM), lambda i: (0, 0)),
            pl.BlockSpec((D_MODEL, EXPERT_DIM), lambda i: (0, 0)),
            pl.BlockSpec((EXPERT_DIM, D_MODEL), lambda i: (0, 0)),
            pl.BlockSpec((1, D_MODEL), lambda i: (0, 0)),
            pl.BlockSpec((1, D_MODEL), lambda i: (0, 0)),
            pl.BlockSpec(memory_space=pl.ANY),
        ],
        out_specs=pl.BlockSpec((tm, D_MODEL), lambda i: (i, 0)),
        out_shape=jax.ShapeDtypeStruct((m, D_MODEL), f32),
        scratch_shapes=[
            pltpu.VMEM((2, TOP_K, tm, D_MODEL), f32),
            pltpu.SemaphoreType.DMA((2,)),
        ],
        compiler_params=pltpu.CompilerParams(
            dimension_semantics=("arbitrary",), vmem_limit_bytes=VMEM_LIMIT),
        name="combine_ln2",
    )(dest3, dest3, h1, gate_tok, wsg, wsu, wsd, g2, b2, ys)


def _rope_tables(pos, width):
    half = ROPE_DIM // 2
    inv_freq = jnp.power(ROPE_THETA, -jnp.arange(0, ROPE_DIM, 2, dtype=f32) / ROPE_DIM)
    ang = pos.astype(f32)[:, None] * inv_freq[None, :]
    cos, sin = jnp.cos(ang), jnp.sin(ang)
    n = pos.shape[0]
    pad = jnp.zeros((n, HEAD_DIM - ROPE_DIM), f32)
    zero = jnp.zeros((n, half), f32)
    c = jnp.concatenate([cos, cos, pad + 1.0], axis=1)
    s1 = jnp.concatenate([-sin, zero, pad], axis=1)
    s2 = jnp.concatenate([zero, sin, pad], axis=1)
    tab = jnp.stack([c, s1, s2])
    return jnp.tile(tab, (1, 1, width // HEAD_DIM))


def _group_lanes(v):
    v = v.reshape(SSM_GROUPS, SSM_HEADS_PER_GROUP)
    return jnp.pad(v, ((0, 0), (0, LANES - SSM_HEADS_PER_GROUP))).reshape(1, DT_WIDTH)


def kernel(x, meta_tokens, ln_in_g, ln_in_b, w_in, conv_w, conv_b, dt_bias, a_log, d_skip, ssm_norm_g, att_norm_g, attn_sinks, w_out, ln1_g, ln1_b, w_router, b_router, w_gate, w_up, w_down, ws_gate, ws_up, ws_down, ln2_g, ln2_b):
    batch, seq, d = x.shape
    assert d == D_MODEL and seq % BLOCK == 0 and meta_tokens.shape == (N_META, D_MODEL)
    assert w_in.shape[0] == 1, "single layer"
    n_tok = batch * seq
    nblk = seq // BLOCK
    row = lambda v: v.reshape(1, -1).astype(f32)

    x2d = x.reshape(n_tok, D_MODEL)
    gi, bi = row(ln_in_g), row(ln_in_b)
    w_main = w_in[0, :, :MAIN_WIDTH].astype(bf16)
    w_dt = w_in[0, :, MAIN_WIDTH:].reshape(D_MODEL, SSM_GROUPS, SSM_HEADS_PER_GROUP)
    w_dt = jnp.pad(w_dt, ((0, 0), (0, 0), (0, LANES - SSM_HEADS_PER_GROUP))).reshape(D_MODEL, DT_WIDTH).astype(bf16)

    proj, dtx = _ln_inproj(x2d, gi, bi, w_main, w_dt, tm=512, tn=512)
    proj_m, dtx_m = _ln_inproj(meta_tokens.astype(f32), gi, bi, w_main, w_dt, tm=N_META, tn=512)
    proj_m = jnp.pad(proj_m, ((BLOCK - N_META, 0), (0, 0)))
    dtx_m = jnp.pad(dtx_m, ((CHUNK - N_META, 0), (0, 0)))

    gw = ATT_GROUP * HEAD_DIM
    tab = _rope_tables(N_META + jnp.arange(seq), gw)
    tab_meta = _rope_tables(jnp.arange(BLOCK), gw)
    kv_meta = jnp.roll(proj_m[:, ATT_WIDTH:ATT_WIDTH + 2 * KV_WIDTH], N_META, axis=0)
    att = _attention(proj, kv_meta, tab, tab_meta, attn_sinks[0].astype(f32), row(att_norm_g[0]),
                     batch=batch, nblk=nblk)

    conv_w0 = conv_w[0].astype(f32)
    conv_b0 = row(conv_b[0])
    dtb_x = _group_lanes(dt_bias[0].astype(f32))
    alog_x = _group_lanes(a_log[0].astype(f32))
    dskip_ch = jnp.repeat(d_skip[0].astype(f32), HEAD_DIM).reshape(1, SSM_WIDTH)
    ng = row(ssm_norm_g[0])
    zeros_halo = jnp.zeros((8, CONV_CH), f32)
    zeros_state = jnp.zeros((SSM_GROUPS, SSM_GROUP_WIDTH, SSM_STATE), f32)
    _, s_meta = _ssd(proj_m, dtx_m, zeros_halo, zeros_state, conv_w0, conv_b0, dtb_x, alog_x, dskip_ch, ng,
                     batch=1, nchunk=1, vstart=CHUNK - N_META, emit_state=True)
    halo = proj_m[CHUNK - 8:, XBC_COL:].astype(f32)
    (ssm,) = _ssd(proj, dtx, halo, s_meta, conv_w0, conv_b0, dtb_x, alog_x, dskip_ch, ng,
                  batch=batch, nchunk=nblk, vstart=0, emit_state=False)

    h1 = _outproj(att, ssm, w_out[0].astype(bf16), x2d, gi, bi, row(ln1_g[0]), row(ln1_b[0]),
                  tm=512, tk=1024)

    tt = 512
    tri = (jnp.arange(tt)[:, None] <= jnp.arange(tt)[None, :]).astype(bf16)
    eidx, gate, rank, counts = _router(h1, w_router[0].T.astype(f32), b_router[0].reshape(N_EXPERTS, 1).astype(f32),
                                       tri, tt=tt)
    counts = counts.reshape(N_EXPERTS)
    padded = (counts + MOE_BLOCK - 1) // MOE_BLOCK * MOE_BLOCK
    pend = jnp.cumsum(padded)
    pstart = pend - padded
    dest = jnp.take(pstart, eidx) + rank
    n_blocks = n_tok * TOP_K // MOE_BLOCK + N_EXPERTS
    blk_first = jnp.arange(n_blocks, dtype=i32) * MOE_BLOCK
    blk_e = jnp.minimum(jnp.sum(pend[None, :] <= blk_first[:, None], axis=1), N_EXPERTS - 1).astype(i32)
    nused = (pend[-1] // MOE_BLOCK).astype(i32).reshape(1)
    rows = n_blocks * MOE_BLOCK
    tiles = lambda t: dest.reshape(TOP_K, n_tok // t, t).transpose(1, 0, 2)

    td, tc = 512, 128
    xs = _dispatch(tiles(td), counts, pstart.astype(i32), padded.astype(i32), h1, rows=rows, tt=td)
    ys = _experts(blk_e, nused, xs, w_gate[0], w_up[0], w_down[0])
    out = _combine(tiles(tc), h1, ys, gate.T, ws_gate[0].astype(bf16), ws_up[0].astype(bf16),
                   ws_down[0].astype(bf16), row(ln2_g[0]), row(ln2_b[0]), tm=tc)
    return out.reshape(batch, seq, D_MODEL)
```

```python
import functools
import math

import jax
import jax.numpy as jnp
from jax import lax
from jax.experimental import pallas as pl
from jax.experimental.pallas import tpu as pltpu

f32 = jnp.float32
bf16 = jnp.bfloat16
i32 = jnp.int32

D_MODEL = 2048
N_META = 16
HEAD_DIM = 64
ATT_HEADS = 32
ATT_KV_HEADS = 4
ATT_GROUP = ATT_HEADS // ATT_KV_HEADS
ATT_WIDTH = 2048
KV_WIDTH = 256
BLOCK = 128
ROPE_DIM = 16
ROPE_THETA = 500000.0
SSM_WIDTH = 4096
SSM_HEADS = 64
SSM_GROUPS = 8
SSM_HEADS_PER_GROUP = SSM_HEADS // SSM_GROUPS
SSM_STATE = 128
SSM_GROUP_WIDTH = SSM_WIDTH // SSM_GROUPS
CONV_WIDTH = 4
CHUNK = 128
CONV_CH = SSM_WIDTH + 2 * SSM_GROUPS * SSM_STATE
MAIN_WIDTH = ATT_WIDTH + 2 * KV_WIDTH + SSM_WIDTH + CONV_CH
Z_COL = ATT_WIDTH + 2 * KV_WIDTH
XBC_COL = Z_COL + SSM_WIDTH
N_EXPERTS = 64
EXPERT_DIM = 512
TOP_K = 8
N_EXPERT_GROUPS = 8
EXPERTS_PER_GROUP = N_EXPERTS // N_EXPERT_GROUPS
TOPK_GROUPS = 4
ROUTED_SCALE = 2.5
MOE_BLOCK = 256
DEEPNORM_ALPHA = 2.0 ** 0.25
LN_EPS = 1e-5
RMS_EPS = 1e-6
NEG_INF = -1e30
LANES = 128
SUBLANES = 8
DT_WIDTH = SSM_GROUPS * LANES
VMEM_LIMIT = 56 * 1024 * 1024


def _layer_norm(x, g, b):
    mu = jnp.mean(x, axis=-1, keepdims=True)
    xc = x - mu
    var = jnp.mean(xc * xc, axis=-1, keepdims=True)
    return xc * lax.rsqrt(var + LN_EPS) * g + b


def _silu(x):
    return x * (1.0 / (1.0 + jnp.exp(-x)))


def _softplus(x):
    return jnp.maximum(x, 0.0) + jnp.log1p(jnp.exp(-jnp.abs(x)))


def _ln_inproj_body(x_ref, g_ref, b_ref, w_ref, wdt_ref, o_ref, dt_ref, h_scr):
    @pl.when(pl.program_id(1) == 0)
    def _():
        h = _layer_norm(x_ref[...], g_ref[...], b_ref[...]).astype(bf16)
        h_scr[...] = h
        dt_ref[...] = jnp.dot(h, wdt_ref[...], preferred_element_type=f32)

    o_ref[...] = jnp.dot(h_scr[...], w_ref[...], preferred_element_type=f32).astype(bf16)


def _ln_inproj(x2d, g, b, w_main, w_dt, *, tm, tn):
    m = x2d.shape[0]
    return pl.pallas_call(
        _ln_inproj_body,
        grid=(m // tm, MAIN_WIDTH // tn),
        in_specs=[
            pl.BlockSpec((tm, D_MODEL), lambda i, j: (i, 0)),
            pl.BlockSpec((1, D_MODEL), lambda i, j: (0, 0)),
            pl.BlockSpec((1, D_MODEL), lambda i, j: (0, 0)),
            pl.BlockSpec((D_MODEL, tn), lambda i, j: (0, j)),
            pl.BlockSpec((D_MODEL, DT_WIDTH), lambda i, j: (0, 0)),
        ],
        out_specs=[
            pl.BlockSpec((tm, tn), lambda i, j: (i, j)),
            pl.BlockSpec((tm, DT_WIDTH), lambda i, j: (i, 0)),
        ],
        out_shape=[
            jax.ShapeDtypeStruct((m, MAIN_WIDTH), bf16),
            jax.ShapeDtypeStruct((m, DT_WIDTH), f32),
        ],
        scratch_shapes=[pltpu.VMEM((tm, D_MODEL), bf16)],
        compiler_params=pltpu.CompilerParams(
            dimension_semantics=("arbitrary", "arbitrary"), vmem_limit_bytes=VMEM_LIMIT),
        name="ln_inproj",
    )(x2d, g, b, w_main, w_dt)


def _rotate(t, tab):
    w = t.shape[-1]
    half = ROPE_DIM // 2
    return (t * tab[0]
            + pltpu.roll(t, w - half, 1) * tab[1]
            + pltpu.roll(t, half, 1) * tab[2])


def _attention_body(sink_ref, q_ref, kvc_ref, kvp_ref, kvm_ref, tabc_ref, tabp_ref, tabm_ref,
                    g_ref, o_ref):
    j = pl.program_id(1)
    tabc = tabc_ref[...]
    kc = _rotate(kvc_ref[:, :KV_WIDTH].astype(f32), tabc[:, :, :KV_WIDTH]).astype(bf16)
    kp = _rotate(kvp_ref[:, :KV_WIDTH].astype(f32), tabp_ref[:, :, :KV_WIDTH]).astype(bf16)
    km = _rotate(kvm_ref[:, :KV_WIDTH].astype(f32), tabm_ref[:, :, :KV_WIDTH]).astype(bf16)
    k_all = jnp.concatenate([kp, kc, km], axis=0)
    v_all = jnp.concatenate([kvp_ref[:, KV_WIDTH:], kvc_ref[:, KV_WIDTH:], kvm_ref[:, KV_WIDTH:]], axis=0)

    rows = ATT_GROUP * BLOCK
    r = lax.broadcasted_iota(i32, (rows, 3 * BLOCK), 0) % BLOCK
    c = lax.broadcasted_iota(i32, (rows, 3 * BLOCK), 1)
    mask = (((c < BLOCK) & (c > r) & (j > 0))
            | ((c >= BLOCK) & (c - BLOCK <= r))
            | ((c >= 2 * BLOCK) & (c < 2 * BLOCK + N_META)))
    head_of_row = lax.broadcasted_iota(i32, (rows, 1), 0) // BLOCK

    outs = []
    for g in range(ATT_KV_HEADS):
        gw = ATT_GROUP * HEAD_DIM
        qg = _rotate(q_ref[:, g * gw:(g + 1) * gw].astype(f32), tabc) * (HEAD_DIM ** -0.5)
        qs = jnp.concatenate([qg[:, h * HEAD_DIM:(h + 1) * HEAD_DIM] for h in range(ATT_GROUP)],
                             axis=0).astype(bf16)
        kg = k_all[:, g * HEAD_DIM:(g + 1) * HEAD_DIM]
        vg = v_all[:, g * HEAD_DIM:(g + 1) * HEAD_DIM]
        s = lax.dot_general(qs, kg, (((1,), (1,)), ((), ())), preferred_element_type=f32)
        s = jnp.where(mask, s, NEG_INF)
        sink = jnp.zeros((rows, 1), f32)
        for h in range(ATT_GROUP):
            sink = jnp.where(head_of_row == h, sink_ref[g * ATT_GROUP + h], sink)
        m = jnp.maximum(jnp.max(s, axis=1, keepdims=True), sink)
        p = jnp.exp(s - m)
        denom = jnp.sum(p, axis=1, keepdims=True) + jnp.exp(sink - m)
        o = jnp.dot(p.astype(bf16), vg, preferred_element_type=f32) / denom
        outs.append(jnp.concatenate([o[h * BLOCK:(h + 1) * BLOCK] for h in range(ATT_GROUP)], axis=1))
    att = jnp.concatenate(outs, axis=1)
    att = att * lax.rsqrt(jnp.mean(att * att, axis=-1, keepdims=True) + RMS_EPS) * g_ref[...]
    o_ref[...] = att.astype(bf16)


def _attention(proj, kv_meta, tab, tab_meta, sinks, att_g, *, batch, nblk):
    m = proj.shape[0]
    kvb = ATT_WIDTH // (2 * KV_WIDTH)
    gw = ATT_GROUP * HEAD_DIM
    return pl.pallas_call(
        _attention_body,
        grid=(batch, nblk),
        in_specs=[
            pl.BlockSpec(memory_space=pltpu.SMEM),
            pl.BlockSpec((BLOCK, ATT_WIDTH), lambda b, j: (b * nblk + j, 0)),
            pl.BlockSpec((BLOCK, 2 * KV_WIDTH), lambda b, j: (b * nblk + j, kvb)),
            pl.BlockSpec((BLOCK, 2 * KV_WIDTH), lambda b, j: (b * nblk + jnp.maximum(j - 1, 0), kvb)),
            pl.BlockSpec((BLOCK, 2 * KV_WIDTH), lambda b, j: (0, 0)),
            pl.BlockSpec((3, BLOCK, gw), lambda b, j: (0, j, 0)),
            pl.BlockSpec((3, BLOCK, gw), lambda b, j: (0, jnp.maximum(j - 1, 0), 0)),
            pl.BlockSpec((3, BLOCK, gw), lambda b, j: (0, 0, 0)),
            pl.BlockSpec((1, ATT_WIDTH), lambda b, j: (0, 0)),
        ],
        out_specs=pl.BlockSpec((BLOCK, ATT_WIDTH), lambda b, j: (b * nblk + j, 0)),
        out_shape=jax.ShapeDtypeStruct((m, ATT_WIDTH), bf16),
        compiler_params=pltpu.CompilerParams(
            dimension_semantics=("arbitrary", "arbitrary"), vmem_limit_bytes=VMEM_LIMIT),
        name="swa_attention",
    )(sinks, proj, proj, proj, kv_meta, tab, tab, tab_meta, att_g)


def _cumsum_rows(x):
    n = x.shape[0]
    row = lax.broadcasted_iota(i32, x.shape, 0)
    k = 1
    while k < n:
        x = x + jnp.where(row >= k, pltpu.roll(x, k, 0), 0.0)
        k *= 2
    return x


def _expand_heads(col):
    lane = lax.broadcasted_iota(i32, col.shape, 1) // HEAD_DIM
    per_tile = LANES // HEAD_DIM
    return jnp.concatenate(
        [jnp.take_along_axis(col, lane + per_tile * j, axis=1) for j in range(SSM_GROUP_WIDTH // LANES)], axis=1)


def _ssd_body(proj_ref, dt_ref, halo_ref, s0_ref, cw_ref, cb_ref, dtb_ref, alog_ref, dskip_ref, ng_ref,
              *rest, vstart, emit_state):
    if emit_state:
        y_ref, sout_ref, state, halo, work = rest
    else:
        y_ref, state, halo, work = rest

    @pl.when(pl.program_id(1) == 0)
    def _():
        state[...] = s0_ref[...]
        halo[...] = halo_ref[...]

    valid = lax.broadcasted_iota(i32, (CHUNK, 1), 0) >= vstart
    causal = (lax.broadcasted_iota(i32, (CHUNK, CHUNK), 0) >= lax.broadcasted_iota(i32, (CHUNK, CHUNK), 1))

    def conv(col, width):
        u = proj_ref[:, pl.ds(pl.multiple_of(XBC_COL + col, LANES), width)].astype(f32)
        cs = pl.ds(pl.multiple_of(col, LANES), width)
        work[0:8, 0:width] = halo[:, cs]
        work[8:8 + CHUNK, 0:width] = u
        w = cw_ref[:, cs]
        acc = cb_ref[:, cs] + u * w[CONV_WIDTH - 1:CONV_WIDTH]
        for jj in range(CONV_WIDTH - 1):
            acc = acc + work[5 + jj:5 + jj + CHUNK, 0:width] * w[jj:jj + 1]
        halo[:, cs] = u[CHUNK - 8:]
        return jnp.where(valid, _silu(acc), 0.0)

    def group(g, carry):
        gs = pl.ds(pl.multiple_of(g * SSM_GROUP_WIDTH, SSM_GROUP_WIDTH), SSM_GROUP_WIDTH)
        gl = pl.ds(pl.multiple_of(g * LANES, LANES), LANES)
        xs = conv(g * SSM_GROUP_WIDTH, SSM_GROUP_WIDTH)
        bm = conv(SSM_WIDTH + g * SSM_STATE, SSM_STATE)
        cm = conv(SSM_WIDTH + SSM_GROUPS * SSM_STATE + g * SSM_STATE, SSM_STATE)

        dt = jnp.where(valid, _softplus(dt_ref[:, gl] + dtb_ref[:, gl]), 0.0)
        a_cs = _cumsum_rows(dt * -jnp.exp(alog_ref[:, gl]))
        a_cs_t = a_cs.T
        a_last = a_cs[CHUNK - 1:CHUNK]

        xdt = xs * _expand_heads(dt)
        bmb = bm.astype(bf16)
        cmb = cm.astype(bf16)
        cb = lax.dot_general(cmb, bmb, (((1,), (1,)), ((), ())), preferred_element_type=f32)
        xdtb = xdt.astype(bf16)
        ys = []
        for i in range(SSM_HEADS_PER_GROUP):
            seg = a_cs[:, i:i + 1] - a_cs_t[i:i + 1, :]
            lmat = (cb * jnp.exp(jnp.where(causal, seg, -jnp.inf))).astype(bf16)
            ys.append(jnp.dot(lmat, xdtb[:, i * HEAD_DIM:(i + 1) * HEAD_DIM], preferred_element_type=f32))
        y = jnp.concatenate(ys, axis=1)

        st = state[g]
        y_off = lax.dot_general(cmb, st.astype(bf16), (((1,), (1,)), ((), ())), preferred_element_type=f32)
        y = y + y_off * _expand_heads(jnp.exp(a_cs))

        x_end = (xdt * _expand_heads(jnp.exp(a_last - a_cs))).T.astype(bf16)
        upd = jnp.dot(x_end, bmb, preferred_element_type=f32)
        for i in range(SSM_HEADS_PER_GROUP):
            rows = slice(i * HEAD_DIM, (i + 1) * HEAD_DIM)
            state[g, rows] = st[rows] * jnp.exp(a_cs_t[i:i + 1, CHUNK - 1:CHUNK]) + upd[rows]

        y = y + dskip_ref[:, gs] * xs
        y = y * _silu(proj_ref[:, pl.ds(pl.multiple_of(Z_COL + g * SSM_GROUP_WIDTH, SSM_GROUP_WIDTH),
                                       SSM_GROUP_WIDTH)].astype(f32))
        y = y * lax.rsqrt(jnp.mean(y * y, axis=-1, keepdims=True) + RMS_EPS) * ng_ref[:, gs]
        y_ref[:, gs] = y.astype(bf16)
        return carry

    lax.fori_loop(0, SSM_GROUPS, group, 0)
    if emit_state:
        sout_ref[...] = state[...]


def _ssd(proj, dtx, halo, s0, conv_w, conv_b, dt_bias_x, a_log_x, d_skip_ch, norm_g,
         *, batch, nchunk, vstart, emit_state):
    m = proj.shape[0]
    full = lambda shape: pl.BlockSpec(shape, lambda b, c: (0,) * len(shape))
    state_shape = (SSM_GROUPS, SSM_GROUP_WIDTH, SSM_STATE)
    in_specs = [
        pl.BlockSpec((CHUNK, MAIN_WIDTH), lambda b, c: (b * nchunk + c, 0)),
        pl.BlockSpec((CHUNK, DT_WIDTH), lambda b, c: (b * nchunk + c, 0)),
        full((8, CONV_CH)),
        full(state_shape),
        full((CONV_WIDTH, CONV_CH)),
        full((1, CONV_CH)),
        full((1, DT_WIDTH)),
        full((1, DT_WIDTH)),
        full((1, SSM_WIDTH)),
        full((1, SSM_WIDTH)),
    ]
    out_specs = [pl.BlockSpec((CHUNK, SSM_WIDTH), lambda b, c: (b * nchunk + c, 0))]
    out_shape = [jax.ShapeDtypeStruct((m, SSM_WIDTH), bf16)]
    if emit_state:
        out_specs.append(full(state_shape))
        out_shape.append(jax.ShapeDtypeStruct(state_shape, f32))
    return pl.pallas_call(
        functools.partial(_ssd_body, vstart=vstart, emit_state=emit_state),
        grid=(batch, nchunk),
        in_specs=in_specs,
        out_specs=out_specs,
        out_shape=out_shape,
        scratch_shapes=[
            pltpu.VMEM(state_shape, f32),
            pltpu.VMEM((8, CONV_CH), f32),
            pltpu.VMEM((8 + CHUNK, SSM_GROUP_WIDTH), f32),
        ],
        compiler_params=pltpu.CompilerParams(
            dimension_semantics=("arbitrary", "arbitrary"), vmem_limit_bytes=VMEM_LIMIT),
        name="ssd_state" if emit_state else "ssd_scan",
    )(proj, dtx, halo, s0, conv_w, conv_b, dt_bias_x, a_log_x, d_skip_ch, norm_g)


def _outproj_body(att_ref, ssm_ref, w_ref, x_ref, gi_ref, bi_ref, g1_ref, b1_ref, h_ref, acc,
                  *, n_att_k):
    k = pl.program_id(1)

    @pl.when(k == 0)
    def _():
        acc[...] = jnp.zeros_like(acc)

    @pl.when(k < n_att_k)
    def _():
        acc[...] += jnp.dot(att_ref[...], w_ref[...], preferred_element_type=f32)

    @pl.when(k >= n_att_k)
    def _():
        acc[...] += jnp.dot(ssm_ref[...], w_ref[...], preferred_element_type=f32)

    @pl.when(k == pl.num_programs(1) - 1)
    def _():
        h0 = _layer_norm(x_ref[...], gi_ref[...], bi_ref[...])
        h_ref[...] = _layer_norm(DEEPNORM_ALPHA * h0 + acc[...], g1_ref[...], b1_ref[...])


def _outproj(att, ssm, w_out, x2d, gi, bi, g1, b1, *, tm, tk):
    m = att.shape[0]
    n_att_k = ATT_WIDTH // tk
    nk = (ATT_WIDTH + SSM_WIDTH) // tk
    return pl.pallas_call(
        functools.partial(_outproj_body, n_att_k=n_att_k),
        grid=(m // tm, nk),
        in_specs=[
            pl.BlockSpec((tm, tk), lambda i, k: (i, jnp.minimum(k, n_att_k - 1))),
            pl.BlockSpec((tm, tk), lambda i, k: (i, jnp.maximum(k - n_att_k, 0))),
            pl.BlockSpec((tk, D_MODEL), lambda i, k: (k, 0)),
            pl.BlockSpec((tm, D_MODEL), lambda i, k: (i, 0)),
            pl.BlockSpec((1, D_MODEL), lambda i, k: (0, 0)),
            pl.BlockSpec((1, D_MODEL), lambda i, k: (0, 0)),
            pl.BlockSpec((1, D_MODEL), lambda i, k: (0, 0)),
            pl.BlockSpec((1, D_MODEL), lambda i, k: (0, 0)),
        ],
        out_specs=pl.BlockSpec((tm, D_MODEL), lambda i, k: (i, 0)),
        out_shape=jax.ShapeDtypeStruct((m, D_MODEL), f32),
        scratch_shapes=[pltpu.VMEM((tm, D_MODEL), f32)],
        compiler_params=pltpu.CompilerParams(
            dimension_semantics=("arbitrary", "arbitrary"), vmem_limit_bytes=VMEM_LIMIT),
        name="outproj_ln1",
    )(att, ssm, w_out, x2d, gi, bi, g1, b1)


def _max01(v):
    return jnp.max(jnp.max(v, axis=1, keepdims=True), axis=0, keepdims=True)


def _router_body(h_ref, wr_ref, br_ref, tri_ref, eidx_ref, gate_ref, rank_ref, cnt_ref, running):
    tt = h_ref.shape[0]
    shape3 = (N_EXPERT_GROUPS, EXPERTS_PER_GROUP, tt)

    @pl.when(pl.program_id(0) == 0)
    def _():
        running[...] = jnp.zeros_like(running)

    logits = lax.dot_general(wr_ref[...], h_ref[...], (((1,), (1,)), ((), ())),
                             precision=lax.Precision.HIGHEST, preferred_element_type=f32)
    scores = (1.0 / (1.0 + jnp.exp(-logits)))
    sel3 = (scores + br_ref[...]).reshape(shape3)
    scores3 = scores.reshape(shape3)
    within = lax.broadcasted_iota(i32, shape3, 1).astype(f32)
    m1 = jnp.max(sel3, axis=1, keepdims=True)
    i1 = jnp.min(jnp.where(sel3 == m1, within, float(EXPERTS_PER_GROUP)), axis=1, keepdims=True)
    m2 = jnp.max(jnp.where(within == i1, -jnp.inf, sel3), axis=1, keepdims=True)
    gs = m1 + m2
    giota = lax.broadcasted_iota(i32, gs.shape, 0).astype(f32)
    gmask = jnp.zeros(gs.shape, f32)
    for _ in range(TOPK_GROUPS):
        gm = jnp.max(gs, axis=0, keepdims=True)
        gi = jnp.min(jnp.where(gs == gm, giota, float(N_EXPERT_GROUPS)), axis=0, keepdims=True)
        hit = giota == gi
        gmask = jnp.where(hit, 1.0, gmask)
        gs = jnp.where(hit, -jnp.inf, gs)
    selm = jnp.where(gmask > 0.0, sel3, -jnp.inf)
    eiota = (lax.broadcasted_iota(i32, shape3, 0) * EXPERTS_PER_GROUP
             + lax.broadcasted_iota(i32, shape3, 1)).astype(f32)
    eidx, gates, hits = [], [], []
    member = jnp.zeros(shape3, f32)
    gsum = jnp.zeros((1, 1, tt), f32)
    for _ in range(TOP_K):
        m = _max01(selm)
        ei = -_max01(-jnp.where(selm == m, eiota, float(N_EXPERTS)))
        hit = eiota == ei
        gk = jnp.sum(jnp.sum(jnp.where(hit, scores3, 0.0), axis=1, keepdims=True), axis=0, keepdims=True)
        eidx.append(ei)
        gates.append(gk)
        hits.append(hit)
        gsum = gsum + gk
        member = jnp.where(hit, 1.0, member)
        selm = jnp.where(hit, -jnp.inf, selm)
    member2 = member.reshape(N_EXPERTS, tt)
    incl = jnp.dot(member2.astype(bf16), tri_ref[...], preferred_element_type=f32)
    base = (running[...] + (incl - member2)).reshape(shape3)
    for k in range(TOP_K):
        rk = jnp.sum(jnp.sum(jnp.where(hits[k], base, 0.0), axis=1, keepdims=True), axis=0, keepdims=True)
        eidx_ref[k:k + 1, :] = eidx[k].reshape(1, tt).astype(i32)
        gate_ref[k:k + 1, :] = (gates[k] / gsum * ROUTED_SCALE).reshape(1, tt)
        rank_ref[k:k + 1, :] = rk.reshape(1, tt).astype(i32)
    total = running[...] + incl[:, tt - 1:tt]
    running[...] = total
    cnt_ref[...] = total.astype(i32)


def _router(h1, w_router_t, b_router_col, tri, *, tt):
    m = h1.shape[0]
    return pl.pallas_call(
        _router_body,
        grid=(m // tt,),
        in_specs=[
            pl.BlockSpec((tt, D_MODEL), lambda i: (i, 0)),
            pl.BlockSpec((N_EXPERTS, D_MODEL), lambda i: (0, 0)),
            pl.BlockSpec((N_EXPERTS, 1), lambda i: (0, 0)),
            pl.BlockSpec((tt, tt), lambda i: (0, 0)),
        ],
        out_specs=[
            pl.BlockSpec((TOP_K, tt), lambda i: (0, i)),
            pl.BlockSpec((TOP_K, tt), lambda i: (0, i)),
            pl.BlockSpec((TOP_K, tt), lambda i: (0, i)),
            pl.BlockSpec((N_EXPERTS, 1), lambda i: (0, 0)),
        ],
        out_shape=[
            jax.ShapeDtypeStruct((TOP_K, m), i32),
            jax.ShapeDtypeStruct((TOP_K, m), f32),
            jax.ShapeDtypeStruct((TOP_K, m), i32),
            jax.ShapeDtypeStruct((N_EXPERTS, 1), i32),
        ],
        scratch_shapes=[pltpu.VMEM((N_EXPERTS, 1), f32)],
        compiler_params=pltpu.CompilerParams(
            dimension_semantics=("arbitrary",), vmem_limit_bytes=VMEM_LIMIT),
        name="router",
    )(h1, w_router_t, b_router_col, tri)


def _row_copy(src_ref, src_row, dst_ref, dst_row, sem):
    return pltpu.make_async_copy(src_ref.at[pl.ds(src_row, 1)], dst_ref.at[pl.ds(dst_row, 1)], sem)


FILL_SIZES = (128, 64, 32, 16, 8)
PACKED_WIDTH = D_MODEL // 2
HIGH_HALF = 0xFFFF0000


def _pack_bf16_pairs(v):
    w = v.shape[1] // 2
    bits = lambda t: lax.bitcast_convert_type(t.astype(bf16).astype(f32), jnp.uint32)
    return (bits(v[:, :w]) >> 16) | (bits(v[:, w:]) & jnp.uint32(HIGH_HALF))


def _unpack_bf16_pairs(p):
    lo = lax.bitcast_convert_type(p << 16, f32)
    hi = lax.bitcast_convert_type(p & jnp.uint32(HIGH_HALF), f32)
    return jnp.concatenate([lo, hi], axis=1)


def _dispatch_body(dest_ref, cnt_ref, pstart_ref, padded_ref, h_ref, xs_ref, packed, zbuf, sem, zsem, *, tt):
    i = pl.program_id(0)
    packed[...] = _pack_bf16_pairs(h_ref[...])

    @pl.when(i == 0)
    def _():
        zbuf[...] = jnp.zeros_like(zbuf)

        def fill(e, wait):
            cnt = cnt_ref[e]
            first = pstart_ref[e] + cnt
            aligned = (first + SUBLANES - 1) // SUBLANES * SUBLANES
            head = aligned - first
            body = pstart_ref[e] + padded_ref[e] - aligned

            def copy(start, size):
                cp = pltpu.make_async_copy(zbuf.at[pl.ds(0, size)], xs_ref.at[pl.ds(start, size)], zsem)
                if wait:
                    cp.wait()
                else:
                    cp.start()

            for r in range(SUBLANES - 1):
                @pl.when(r < head)
                def _():
                    copy(first + r, 1)
            for size in FILL_SIZES:
                @pl.when((body & size) != 0)
                def _():
                    copy(pl.multiple_of(aligned + (body & (-2 * size)), SUBLANES), size)

        def start_fill(e, c):
            fill(e, False)
            return c

        def wait_fill(e, c):
            fill(e, True)
            return c

        lax.fori_loop(0, N_EXPERTS, start_fill, 0)
        lax.fori_loop(0, N_EXPERTS, wait_fill, 0)

    def issue(t, c):
        for k in range(TOP_K):
            _row_copy(packed, t, xs_ref, dest_ref[0, k, t], sem).start(priority=k % 2)
        return c

    def drain(t, c):
        for k in range(TOP_K):
            _row_copy(packed, 0, xs_ref, 0, sem).wait()
        return c

    lax.fori_loop(0, tt, issue, 0, unroll=4)
    lax.fori_loop(0, tt, drain, 0, unroll=4)


def _dispatch(dest3, counts, pstart, padded, h1, *, rows, tt):
    m = h1.shape[0]
    smem = pl.BlockSpec(memory_space=pltpu.SMEM)
    return pl.pallas_call(
        functools.partial(_dispatch_body, tt=tt),
        grid=(m // tt,),
        in_specs=[
            pl.BlockSpec((1, TOP_K, tt), lambda i: (i, 0, 0), memory_space=pltpu.SMEM),
            smem, smem, smem,
            pl.BlockSpec((tt, D_MODEL), lambda i: (i, 0)),
        ],
        out_specs=pl.BlockSpec(memory_space=pl.ANY),
        out_shape=jax.ShapeDtypeStruct((rows, PACKED_WIDTH), jnp.uint32),
        scratch_shapes=[
            pltpu.VMEM((tt, PACKED_WIDTH), jnp.uint32),
            pltpu.VMEM((FILL_SIZES[0], PACKED_WIDTH), jnp.uint32),
            pltpu.SemaphoreType.DMA(()),
            pltpu.SemaphoreType.DMA(()),
        ],
        compiler_params=pltpu.CompilerParams(
            dimension_semantics=("arbitrary",), vmem_limit_bytes=VMEM_LIMIT, has_side_effects=True),
        name="moe_dispatch",
    )(dest3, counts, pstart, padded, h1)


def _experts_body(blk_e_ref, nused_ref, x_ref, wg_ref, wu_ref, wd_ref, y_ref, wgb, wub, wdb):
    i = pl.program_id(0)

    @pl.when(i < nused_ref[0])
    def _():
        e = blk_e_ref[i]
        prev = blk_e_ref[jnp.maximum(i - 1, 0)]

        @pl.when((i == 0) | (e != prev))
        def _():
            wgb[...] = wg_ref[0].astype(bf16)
            wub[...] = wu_ref[0].astype(bf16)
            wdb[...] = wd_ref[0].astype(bf16)

        x = _unpack_bf16_pairs(x_ref[...]).astype(bf16)
        hg = jnp.dot(x, wgb[...], preferred_element_type=f32)
        hu = jnp.dot(x, wub[...], preferred_element_type=f32)
        hb = (_silu(hg) * hu).astype(bf16)
        y_ref[...] = _pack_bf16_pairs(jnp.dot(hb, wdb[...], preferred_element_type=f32))


def _experts(blk_e, nused, xs, w_gate, w_up, w_down):
    rows = xs.shape[0]
    nblk = rows // MOE_BLOCK
    blk = lambda i, be, nu: jnp.minimum(i, nu[0] - 1)
    grid_spec = pltpu.PrefetchScalarGridSpec(
        num_scalar_prefetch=2,
        grid=(nblk,),
        in_specs=[
            pl.BlockSpec((MOE_BLOCK, PACKED_WIDTH), lambda i, be, nu: (blk(i, be, nu), 0)),
            pl.BlockSpec((1, D_MODEL, EXPERT_DIM), lambda i, be, nu: (be[i], 0, 0)),
            pl.BlockSpec((1, D_MODEL, EXPERT_DIM), lambda i, be, nu: (be[i], 0, 0)),
            pl.BlockSpec((1, EXPERT_DIM, D_MODEL), lambda i, be, nu: (be[i], 0, 0)),
        ],
        out_specs=pl.BlockSpec((MOE_BLOCK, PACKED_WIDTH), lambda i, be, nu: (blk(i, be, nu), 0)),
        scratch_shapes=[
            pltpu.VMEM((D_MODEL, EXPERT_DIM), bf16),
            pltpu.VMEM((D_MODEL, EXPERT_DIM), bf16),
            pltpu.VMEM((EXPERT_DIM, D_MODEL), bf16),
        ],
    )
    return pl.pallas_call(
        _experts_body,
        grid_spec=grid_spec,
        out_shape=jax.ShapeDtypeStruct((rows, PACKED_WIDTH), jnp.uint32),
        compiler_params=pltpu.CompilerParams(
            dimension_semantics=("arbitrary",), vmem_limit_bytes=VMEM_LIMIT),
        name="routed_experts",
    )(blk_e, nused, xs, w_gate, w_up, w_down)


def _combine_body(dest_ref, destn_ref, h_ref, gate_ref, wsg_ref, wsu_ref, wsd_ref, g2_ref, b2_ref, ys_ref,
                  o_ref, buf, sem, *, tm):
    i = pl.program_id(0)
    slot = i % 2

    def issue(d_ref, s):
        def body(t, c):
            for k in range(TOP_K):
                _row_copy(ys_ref, d_ref[0, k, t], buf.at[s, k], t, sem.at[s]).start(priority=k % 2)
            return c
        lax.fori_loop(0, tm, body, 0, unroll=4)

    @pl.when(i == 0)
    def _():
        issue(dest_ref, 0)

    @pl.when(i + 1 < pl.num_programs(0))
    def _():
        issue(destn_ref, 1 - slot)

    h = h_ref[...]
    hb = h.astype(bf16)
    sg = jnp.dot(hb, wsg_ref[...], preferred_element_type=f32)
    su = jnp.dot(hb, wsu_ref[...], preferred_element_type=f32)
    ffn = jnp.dot((_silu(sg) * su).astype(bf16), wsd_ref[...], preferred_element_type=f32)

    def drain(t, c):
        for k in range(TOP_K):
            _row_copy(ys_ref, 0, buf.at[slot, k], 0, sem.at[slot]).wait()
        return c
    lax.fori_loop(0, tm, drain, 0, unroll=4)

    gate = gate_ref[...]
    for k in range(TOP_K):
        ffn = ffn + gate[:, k:k + 1] * _unpack_bf16_pairs(buf[slot, k])
    o_ref[...] = _layer_norm(DEEPNORM_ALPHA * h + ffn, g2_ref[...], b2_ref[...])


def _combine(dest3, h1, ys, gate_tok, wsg, wsu, wsd, g2, b2, *, tm):
    m = h1.shape[0]
    nt = m // tm
    return pl.pallas_call(
        functools.partial(_combine_body, tm=tm),
        grid=(nt,),
        in_specs=[
            pl.BlockSpec((1, TOP_K, tm), lambda i: (i, 0, 0), memory_space=pltpu.SMEM),
            pl.BlockSpec((1, TOP_K, tm), lambda i: (jnp.minimum(i + 1, nt - 1), 0, 0), memory_space=pltpu.SMEM),
            pl.BlockSpec((tm, D_MODEL), lambda i: (i, 0)),
            pl.BlockSpec((tm, TOP_K), lambda i: (i, 0)),
            pl.BlockSpec((D_MODEL, EXPERT_DIM), lambda i: (0, 0)),
            pl.BlockSpec((D_MODEL, EXPERT_DIM), lambda i: (0, 0)),
            pl.BlockSpec((EXPERT_DIM, D_MODEL), lambda i: (0, 0)),
            pl.BlockSpec((1, D_MODEL), lambda i: (0, 0)),
            pl.BlockSpec((1, D_MODEL), lambda i: (0, 0)),
            pl.BlockSpec(memory_space=pl.ANY),
        ],
        out_specs=pl.BlockSpec((tm, D_MODEL), lambda i: (i, 0)),
        out_shape=jax.ShapeDtypeStruct((m, D_MODEL), f32),
        scratch_shapes=[
            pltpu.VMEM((2, TOP_K, tm, PACKED_WIDTH), jnp.uint32),
            pltpu.SemaphoreType.DMA((2,)),
        ],
        compiler_params=pltpu.CompilerParams(
            dimension_semantics=("arbitrary",), vmem_limit_bytes=VMEM_LIMIT),
        name="combine_ln2",
    )(dest3, dest3, h1, gate_tok, wsg, wsu, wsd, g2, b2, ys)


def _rope_tables(pos, width):
    half = ROPE_DIM // 2
    inv_freq = jnp.power(ROPE_THETA, -jnp.arange(0, ROPE_DIM, 2, dtype=f32) / ROPE_DIM)
    ang = pos.astype(f32)[:, None] * inv_freq[None, :]
    cos, sin = jnp.cos(ang), jnp.sin(ang)
    n = pos.shape[0]
    pad = jnp.zeros((n, HEAD_DIM - ROPE_DIM), f32)
    zero = jnp.zeros((n, half), f32)
    c = jnp.concatenate([cos, cos, pad + 1.0], axis=1)
    s1 = jnp.concatenate([-sin, zero, pad], axis=1)
    s2 = jnp.concatenate([zero, sin, pad], axis=1)
    tab = jnp.stack([c, s1, s2])
    return jnp.tile(tab, (1, 1, width // HEAD_DIM))


def _group_lanes(v):
    v = v.reshape(SSM_GROUPS, SSM_HEADS_PER_GROUP)
    return jnp.pad(v, ((0, 0), (0, LANES - SSM_HEADS_PER_GROUP))).reshape(1, DT_WIDTH)


def kernel(x, meta_tokens, ln_in_g, ln_in_b, w_in, conv_w, conv_b, dt_bias, a_log, d_skip, ssm_norm_g, att_norm_g, attn_sinks, w_out, ln1_g, ln1_b, w_router, b_router, w_gate, w_up, w_down, ws_gate, ws_up, ws_down, ln2_g, ln2_b):
    batch, seq, d = x.shape
    assert d == D_MODEL and seq % BLOCK == 0 and meta_tokens.shape == (N_META, D_MODEL)
    assert w_in.shape[0] == 1, "single layer"
    n_tok = batch * seq
    nblk = seq // BLOCK
    row = lambda v: v.reshape(1, -1).astype(f32)

    x2d = x.reshape(n_tok, D_MODEL)
    gi, bi = row(ln_in_g), row(ln_in_b)
    w_main = w_in[0, :, :MAIN_WIDTH].astype(bf16)
    w_dt = w_in[0, :, MAIN_WIDTH:].reshape(D_MODEL, SSM_GROUPS, SSM_HEADS_PER_GROUP)
    w_dt = jnp.pad(w_dt, ((0, 0), (0, 0), (0, LANES - SSM_HEADS_PER_GROUP))).reshape(D_MODEL, DT_WIDTH).astype(bf16)

    proj, dtx = _ln_inproj(x2d, gi, bi, w_main, w_dt, tm=1024, tn=512)
    proj_m, dtx_m = _ln_inproj(meta_tokens.astype(f32), gi, bi, w_main, w_dt, tm=N_META, tn=512)
    proj_m = jnp.pad(proj_m, ((BLOCK - N_META, 0), (0, 0)))
    dtx_m = jnp.pad(dtx_m, ((CHUNK - N_META, 0), (0, 0)))

    gw = ATT_GROUP * HEAD_DIM
    tab = _rope_tables(N_META + jnp.arange(seq), gw)
    tab_meta = _rope_tables(jnp.arange(BLOCK), gw)
    kv_meta = jnp.roll(proj_m[:, ATT_WIDTH:ATT_WIDTH + 2 * KV_WIDTH], N_META, axis=0)
    att = _attention(proj, kv_meta, tab, tab_meta, attn_sinks[0].astype(f32), row(att_norm_g[0]),
                     batch=batch, nblk=nblk)

    conv_w0 = conv_w[0].astype(f32)
    conv_b0 = row(conv_b[0])
    dtb_x = _group_lanes(dt_bias[0].astype(f32))
    alog_x = _group_lanes(a_log[0].astype(f32))
    dskip_ch = jnp.repeat(d_skip[0].astype(f32), HEAD_DIM).reshape(1, SSM_WIDTH)
    ng = row(ssm_norm_g[0])
    zeros_halo = jnp.zeros((8, CONV_CH), f32)
    zeros_state = jnp.zeros((SSM_GROUPS, SSM_GROUP_WIDTH, SSM_STATE), f32)
    _, s_meta = _ssd(proj_m, dtx_m, zeros_halo, zeros_state, conv_w0, conv_b0, dtb_x, alog_x, dskip_ch, ng,
                     batch=1, nchunk=1, vstart=CHUNK - N_META, emit_state=True)
    halo = proj_m[CHUNK - 8:, XBC_COL:].astype(f32)
    (ssm,) = _ssd(proj, dtx, halo, s_meta, conv_w0, conv_b0, dtb_x, alog_x, dskip_ch, ng,
                  batch=batch, nchunk=nblk, vstart=0, emit_state=False)

    h1 = _outproj(att, ssm, w_out[0].astype(bf16), x2d, gi, bi, row(ln1_g[0]), row(ln1_b[0]),
                  tm=512, tk=1024)

    tt = 512
    tri = (jnp.arange(tt)[:, None] <= jnp.arange(tt)[None, :]).astype(bf16)
    eidx, gate, rank, counts = _router(h1, w_router[0].T.astype(f32), b_router[0].reshape(N_EXPERTS, 1).astype(f32),
                                       tri, tt=tt)
    counts = counts.reshape(N_EXPERTS)
    padded = (counts + MOE_BLOCK - 1) // MOE_BLOCK * MOE_BLOCK
    pend = jnp.cumsum(padded)
    pstart = pend - padded
    first_row = jnp.sum(jnp.where(eidx[..., None] == jnp.arange(N_EXPERTS, dtype=i32), pstart.astype(i32), 0), axis=-1)
    dest = first_row + rank
    n_blocks = n_tok * TOP_K // MOE_BLOCK + N_EXPERTS
    blk_first = jnp.arange(n_blocks, dtype=i32) * MOE_BLOCK
    blk_e = jnp.minimum(jnp.sum(pend[None, :] <= blk_first[:, None], axis=1), N_EXPERTS - 1).astype(i32)
    nused = (pend[-1] // MOE_BLOCK).astype(i32).reshape(1)
    rows = n_blocks * MOE_BLOCK
    tiles = lambda t: dest.reshape(TOP_K, n_tok // t, t).transpose(1, 0, 2)

    td, tc = 256, 128
    xs = _dispatch(tiles(td), counts, pstart.astype(i32), padded.astype(i32), h1, rows=rows, tt=td)
    ys = _experts(blk_e, nused, xs, w_gate[0], w_up[0], w_down[0])
    out = _combine(tiles(tc), h1, ys, gate.T, ws_gate[0].astype(bf16), ws_up[0].astype(bf16),
                   ws_down[0].astype(bf16), row(ln2_g[0]), row(ln2_b[0]), tm=tc)
    return out.reshape(batch, seq, D_MODEL)
```

```python
import functools
import math

import jax
import jax.numpy as jnp
from jax import lax
from jax.experimental import pallas as pl
from jax.experimental.pallas import tpu as pltpu

f32 = jnp.float32
bf16 = jnp.bfloat16
i32 = jnp.int32

D_MODEL = 2048
N_META = 16
HEAD_DIM = 64
ATT_HEADS = 32
ATT_KV_HEADS = 4
ATT_GROUP = ATT_HEADS // ATT_KV_HEADS
ATT_WIDTH = 2048
KV_WIDTH = 256
BLOCK = 128
ROPE_DIM = 16
ROPE_THETA = 500000.0
SSM_WIDTH = 4096
SSM_HEADS = 64
SSM_GROUPS = 8
SSM_HEADS_PER_GROUP = SSM_HEADS // SSM_GROUPS
SSM_STATE = 128
SSM_GROUP_WIDTH = SSM_WIDTH // SSM_GROUPS
CONV_WIDTH = 4
CHUNK = 128
CONV_CH = SSM_WIDTH + 2 * SSM_GROUPS * SSM_STATE
MAIN_WIDTH = ATT_WIDTH + 2 * KV_WIDTH + SSM_WIDTH + CONV_CH
Z_COL = ATT_WIDTH + 2 * KV_WIDTH
XBC_COL = Z_COL + SSM_WIDTH
N_EXPERTS = 64
EXPERT_DIM = 512
TOP_K = 8
N_EXPERT_GROUPS = 8
EXPERTS_PER_GROUP = N_EXPERTS // N_EXPERT_GROUPS
TOPK_GROUPS = 4
ROUTED_SCALE = 2.5
MOE_BLOCK = 256
DEEPNORM_ALPHA = 2.0 ** 0.25
LN_EPS = 1e-5
RMS_EPS = 1e-6
NEG_INF = -1e30
LANES = 128
SUBLANES = 8
DT_WIDTH = SSM_GROUPS * LANES
VMEM_LIMIT = 56 * 1024 * 1024


def _layer_norm(x, g, b):
    mu = jnp.mean(x, axis=-1, keepdims=True)
    xc = x - mu
    var = jnp.mean(xc * xc, axis=-1, keepdims=True)
    return xc * lax.rsqrt(var + LN_EPS) * g + b


def _silu(x):
    return x * (1.0 / (1.0 + jnp.exp(-x)))


def _softplus(x):
    return jnp.maximum(x, 0.0) + jnp.log1p(jnp.exp(-jnp.abs(x)))


def _ln_inproj_body(x_ref, g_ref, b_ref, w_ref, wdt_ref, o_ref, dt_ref, h_scr):
    @pl.when(pl.program_id(1) == 0)
    def _():
        h = _layer_norm(x_ref[...], g_ref[...], b_ref[...]).astype(bf16)
        h_scr[...] = h
        dt_ref[...] = jnp.dot(h, wdt_ref[...], preferred_element_type=f32)

    o_ref[...] = jnp.dot(h_scr[...], w_ref[...], preferred_element_type=f32).astype(bf16)


def _ln_inproj(x2d, g, b, w_main, w_dt, *, tm, tn):
    m = x2d.shape[0]
    return pl.pallas_call(
        _ln_inproj_body,
        grid=(m // tm, MAIN_WIDTH // tn),
        in_specs=[
            pl.BlockSpec((tm, D_MODEL), lambda i, j: (i, 0)),
            pl.BlockSpec((1, D_MODEL), lambda i, j: (0, 0)),
            pl.BlockSpec((1, D_MODEL), lambda i, j: (0, 0)),
            pl.BlockSpec((D_MODEL, tn), lambda i, j: (0, j)),
            pl.BlockSpec((D_MODEL, DT_WIDTH), lambda i, j: (0, 0)),
        ],
        out_specs=[
            pl.BlockSpec((tm, tn), lambda i, j: (i, j)),
            pl.BlockSpec((tm, DT_WIDTH), lambda i, j: (i, 0)),
        ],
        out_shape=[
            jax.ShapeDtypeStruct((m, MAIN_WIDTH), bf16),
            jax.ShapeDtypeStruct((m, DT_WIDTH), f32),
        ],
        scratch_shapes=[pltpu.VMEM((tm, D_MODEL), bf16)],
        compiler_params=pltpu.CompilerParams(
            dimension_semantics=("arbitrary", "arbitrary"), vmem_limit_bytes=VMEM_LIMIT),
        name="ln_inproj",
    )(x2d, g, b, w_main, w_dt)


def _rotate(t, tab):
    w = t.shape[-1]
    half = ROPE_DIM // 2
    return (t * tab[0]
            + pltpu.roll(t, w - half, 1) * tab[1]
            + pltpu.roll(t, half, 1) * tab[2])


def _attention_body(sink_ref, q_ref, kvc_ref, kvp_ref, kvm_ref, tabc_ref, tabp_ref, tabm_ref,
                    g_ref, o_ref):
    j = pl.program_id(1)
    tabc = tabc_ref[...]
    kc = _rotate(kvc_ref[:, :KV_WIDTH].astype(f32), tabc[:, :, :KV_WIDTH]).astype(bf16)
    kp = _rotate(kvp_ref[:, :KV_WIDTH].astype(f32), tabp_ref[:, :, :KV_WIDTH]).astype(bf16)
    km = _rotate(kvm_ref[:, :KV_WIDTH].astype(f32), tabm_ref[:, :, :KV_WIDTH]).astype(bf16)
    v_all = jnp.concatenate([kvp_ref[:, KV_WIDTH:], kvc_ref[:, KV_WIDTH:], kvm_ref[:, KV_WIDTH:]], axis=0)

    rows = ATT_GROUP * BLOCK
    r = lax.broadcasted_iota(i32, (rows, BLOCK), 0) % BLOCK
    c = lax.broadcasted_iota(i32, (rows, BLOCK), 1)
    lower = c <= r
    prev_ok = (c > r) & (j > 0)
    meta_ok = c < N_META
    head_of_row = lax.broadcasted_iota(i32, (rows, 1), 0) // BLOCK
    nt_dims = (((1,), (1,)), ((), ()))

    outs = []
    for g in range(ATT_KV_HEADS):
        gw = ATT_GROUP * HEAD_DIM
        hs = slice(g * HEAD_DIM, (g + 1) * HEAD_DIM)
        qg = _rotate(q_ref[:, g * gw:(g + 1) * gw].astype(f32), tabc) * (HEAD_DIM ** -0.5)
        qs = jnp.concatenate([qg[:, h * HEAD_DIM:(h + 1) * HEAD_DIM] for h in range(ATT_GROUP)],
                             axis=0).astype(bf16)
        s_cur = lax.dot_general(qs, kc[:, hs], nt_dims, preferred_element_type=f32)
        s_prev = lax.dot_general(qs, kp[:, hs], nt_dims, preferred_element_type=f32)
        s_meta = lax.dot_general(qs, km[:, hs], nt_dims, preferred_element_type=f32)
        s_band = jnp.where(lower, s_cur, jnp.where(prev_ok, s_prev, NEG_INF))
        s_meta = jnp.where(meta_ok, s_meta, NEG_INF)
        sink = jnp.zeros((rows, 1), f32)
        for h in range(ATT_GROUP):
            sink = jnp.where(head_of_row == h, sink_ref[g * ATT_GROUP + h], sink)
        m = jnp.maximum(jnp.max(jnp.maximum(s_band, s_meta), axis=1, keepdims=True), sink)
        p_band = jnp.exp(s_band - m)
        p_meta = jnp.exp(s_meta - m)
        denom = jnp.sum(p_band + p_meta, axis=1, keepdims=True) + jnp.exp(sink - m)
        p_cur = jnp.where(lower, p_band, 0.0)
        p = jnp.concatenate([p_band - p_cur, p_cur, p_meta], axis=1).astype(bf16)
        o = jnp.dot(p, v_all[:, hs], preferred_element_type=f32) / denom
        outs.append(jnp.concatenate([o[h * BLOCK:(h + 1) * BLOCK] for h in range(ATT_GROUP)], axis=1))
    att = jnp.concatenate(outs, axis=1)
    att = att * lax.rsqrt(jnp.mean(att * att, axis=-1, keepdims=True) + RMS_EPS) * g_ref[...]
    o_ref[...] = att.astype(bf16)


def _attention(proj, kv_meta, tab, tab_meta, sinks, att_g, *, batch, nblk):
    m = proj.shape[0]
    kvb = ATT_WIDTH // (2 * KV_WIDTH)
    gw = ATT_GROUP * HEAD_DIM
    return pl.pallas_call(
        _attention_body,
        grid=(batch, nblk),
        in_specs=[
            pl.BlockSpec(memory_space=pltpu.SMEM),
            pl.BlockSpec((BLOCK, ATT_WIDTH), lambda b, j: (b * nblk + j, 0)),
            pl.BlockSpec((BLOCK, 2 * KV_WIDTH), lambda b, j: (b * nblk + j, kvb)),
            pl.BlockSpec((BLOCK, 2 * KV_WIDTH), lambda b, j: (b * nblk + jnp.maximum(j - 1, 0), kvb)),
            pl.BlockSpec((BLOCK, 2 * KV_WIDTH), lambda b, j: (0, 0)),
            pl.BlockSpec((3, BLOCK, gw), lambda b, j: (0, j, 0)),
            pl.BlockSpec((3, BLOCK, gw), lambda b, j: (0, jnp.maximum(j - 1, 0), 0)),
            pl.BlockSpec((3, BLOCK, gw), lambda b, j: (0, 0, 0)),
            pl.BlockSpec((1, ATT_WIDTH), lambda b, j: (0, 0)),
        ],
        out_specs=pl.BlockSpec((BLOCK, ATT_WIDTH), lambda b, j: (b * nblk + j, 0)),
        out_shape=jax.ShapeDtypeStruct((m, ATT_WIDTH), bf16),
        compiler_params=pltpu.CompilerParams(
            dimension_semantics=("arbitrary", "arbitrary"), vmem_limit_bytes=VMEM_LIMIT),
        name="swa_attention",
    )(sinks, proj, proj, proj, kv_meta, tab, tab, tab_meta, att_g)


def _cumsum_rows(x):
    n = x.shape[0]
    row = lax.broadcasted_iota(i32, x.shape, 0)
    k = 1
    while k < n:
        x = x + jnp.where(row >= k, pltpu.roll(x, k, 0), 0.0)
        k *= 2
    return x


def _expand_heads(col):
    lane = lax.broadcasted_iota(i32, col.shape, 1) // HEAD_DIM
    per_tile = LANES // HEAD_DIM
    return jnp.concatenate(
        [jnp.take_along_axis(col, lane + per_tile * j, axis=1) for j in range(SSM_GROUP_WIDTH // LANES)], axis=1)


def _ssd_body(proj_ref, dt_ref, halo_ref, s0_ref, cw_ref, cb_ref, dtb_ref, alog_ref, dskip_ref, ng_ref,
              *rest, vstart, emit_state):
    if emit_state:
        y_ref, sout_ref, state, halo, work = rest
    else:
        y_ref, state, halo, work = rest

    @pl.when(pl.program_id(1) == 0)
    def _():
        state[...] = s0_ref[...]
        halo[...] = halo_ref[...]

    valid = lax.broadcasted_iota(i32, (CHUNK, 1), 0) >= vstart
    causal = (lax.broadcasted_iota(i32, (CHUNK, CHUNK), 0) >= lax.broadcasted_iota(i32, (CHUNK, CHUNK), 1))

    def conv(col, width):
        u = proj_ref[:, pl.ds(pl.multiple_of(XBC_COL + col, LANES), width)].astype(f32)
        cs = pl.ds(pl.multiple_of(col, LANES), width)
        work[0:8, 0:width] = halo[:, cs]
        work[8:8 + CHUNK, 0:width] = u
        w = cw_ref[:, cs]
        acc = cb_ref[:, cs] + u * w[CONV_WIDTH - 1:CONV_WIDTH]
        for jj in range(CONV_WIDTH - 1):
            acc = acc + work[5 + jj:5 + jj + CHUNK, 0:width] * w[jj:jj + 1]
        halo[:, cs] = u[CHUNK - 8:]
        return jnp.where(valid, _silu(acc), 0.0)

    def group(g, carry):
        gs = pl.ds(pl.multiple_of(g * SSM_GROUP_WIDTH, SSM_GROUP_WIDTH), SSM_GROUP_WIDTH)
        gl = pl.ds(pl.multiple_of(g * LANES, LANES), LANES)
        xs = conv(g * SSM_GROUP_WIDTH, SSM_GROUP_WIDTH)
        bm = conv(SSM_WIDTH + g * SSM_STATE, SSM_STATE)
        cm = conv(SSM_WIDTH + SSM_GROUPS * SSM_STATE + g * SSM_STATE, SSM_STATE)

        dt = jnp.where(valid, _softplus(dt_ref[:, gl] + dtb_ref[:, gl]), 0.0)
        a_cs = _cumsum_rows(dt * -jnp.exp(alog_ref[:, gl]))
        a_cs_t = a_cs.T
        a_last = a_cs[CHUNK - 1:CHUNK]

        xdt = xs * _expand_heads(dt)
        bmb = bm.astype(bf16)
        cmb = cm.astype(bf16)
        cb = lax.dot_general(cmb, bmb, (((1,), (1,)), ((), ())), preferred_element_type=f32)
        xdtb = xdt.astype(bf16)
        ys = []
        for i in range(SSM_HEADS_PER_GROUP):
            seg = a_cs[:, i:i + 1] - a_cs_t[i:i + 1, :]
            lmat = (cb * jnp.exp(jnp.where(causal, seg, -jnp.inf))).astype(bf16)
            ys.append(jnp.dot(lmat, xdtb[:, i * HEAD_DIM:(i + 1) * HEAD_DIM], preferred_element_type=f32))
        y = jnp.concatenate(ys, axis=1)

        st = state[g]
        y_off = lax.dot_general(cmb, st.astype(bf16), (((1,), (1,)), ((), ())), preferred_element_type=f32)
        y = y + y_off * _expand_heads(jnp.exp(a_cs))

        x_end = (xdt * _expand_heads(jnp.exp(a_last - a_cs))).T.astype(bf16)
        upd = jnp.dot(x_end, bmb, preferred_element_type=f32)
        for i in range(SSM_HEADS_PER_GROUP):
            rows = slice(i * HEAD_DIM, (i + 1) * HEAD_DIM)
            state[g, rows] = st[rows] * jnp.exp(a_cs_t[i:i + 1, CHUNK - 1:CHUNK]) + upd[rows]

        y = y + dskip_ref[:, gs] * xs
        y = y * _silu(proj_ref[:, pl.ds(pl.multiple_of(Z_COL + g * SSM_GROUP_WIDTH, SSM_GROUP_WIDTH),
                                       SSM_GROUP_WIDTH)].astype(f32))
        y = y * lax.rsqrt(jnp.mean(y * y, axis=-1, keepdims=True) + RMS_EPS) * ng_ref[:, gs]
        y_ref[:, gs] = y.astype(bf16)
        return carry

    lax.fori_loop(0, SSM_GROUPS, group, 0)
    if emit_state:
        sout_ref[...] = state[...]


def _ssd(proj, dtx, halo, s0, conv_w, conv_b, dt_bias_x, a_log_x, d_skip_ch, norm_g,
         *, batch, nchunk, vstart, emit_state):
    m = proj.shape[0]
    full = lambda shape: pl.BlockSpec(shape, lambda b, c: (0,) * len(shape))
    state_shape = (SSM_GROUPS, SSM_GROUP_WIDTH, SSM_STATE)
    in_specs = [
        pl.BlockSpec((CHUNK, MAIN_WIDTH), lambda b, c: (b * nchunk + c, 0)),
        pl.BlockSpec((CHUNK, DT_WIDTH), lambda b, c: (b * nchunk + c, 0)),
        full((8, CONV_CH)),
        full(state_shape),
        full((CONV_WIDTH, CONV_CH)),
        full((1, CONV_CH)),
        full((1, DT_WIDTH)),
        full((1, DT_WIDTH)),
        full((1, SSM_WIDTH)),
        full((1, SSM_WIDTH)),
    ]
    out_specs = [pl.BlockSpec((CHUNK, SSM_WIDTH), lambda b, c: (b * nchunk + c, 0))]
    out_shape = [jax.ShapeDtypeStruct((m, SSM_WIDTH), bf16)]
    if emit_state:
        out_specs.append(full(state_shape))
        out_shape.append(jax.ShapeDtypeStruct(state_shape, f32))
    return pl.pallas_call(
        functools.partial(_ssd_body, vstart=vstart, emit_state=emit_state),
        grid=(batch, nchunk),
        in_specs=in_specs,
        out_specs=out_specs,
        out_shape=out_shape,
        scratch_shapes=[
            pltpu.VMEM(state_shape, f32),
            pltpu.VMEM((8, CONV_CH), f32),
            pltpu.VMEM((8 + CHUNK, SSM_GROUP_WIDTH), f32),
        ],
        compiler_params=pltpu.CompilerParams(
            dimension_semantics=("arbitrary", "arbitrary"), vmem_limit_bytes=VMEM_LIMIT),
        name="ssd_state" if emit_state else "ssd_scan",
    )(proj, dtx, halo, s0, conv_w, conv_b, dt_bias_x, a_log_x, d_skip_ch, norm_g)


def _outproj_body(att_ref, ssm_ref, w_ref, x_ref, gi_ref, bi_ref, g1_ref, b1_ref, h_ref, acc,
                  *, n_att_k):
    k = pl.program_id(1)

    @pl.when(k == 0)
    def _():
        acc[...] = jnp.zeros_like(acc)

    @pl.when(k < n_att_k)
    def _():
        acc[...] += jnp.dot(att_ref[...], w_ref[...], preferred_element_type=f32)

    @pl.when(k >= n_att_k)
    def _():
        acc[...] += jnp.dot(ssm_ref[...], w_ref[...], preferred_element_type=f32)

    @pl.when(k == pl.num_programs(1) - 1)
    def _():
        h0 = _layer_norm(x_ref[...], gi_ref[...], bi_ref[...])
        h_ref[...] = _layer_norm(DEEPNORM_ALPHA * h0 + acc[...], g1_ref[...], b1_ref[...])


def _outproj(att, ssm, w_out, x2d, gi, bi, g1, b1, *, tm, tk):
    m = att.shape[0]
    n_att_k = ATT_WIDTH // tk
    nk = (ATT_WIDTH + SSM_WIDTH) // tk
    return pl.pallas_call(
        functools.partial(_outproj_body, n_att_k=n_att_k),
        grid=(m // tm, nk),
        in_specs=[
            pl.BlockSpec((tm, tk), lambda i, k: (i, jnp.minimum(k, n_att_k - 1))),
            pl.BlockSpec((tm, tk), lambda i, k: (i, jnp.maximum(k - n_att_k, 0))),
            pl.BlockSpec((tk, D_MODEL), lambda i, k: (k, 0)),
            pl.BlockSpec((tm, D_MODEL), lambda i, k: (i, 0)),
            pl.BlockSpec((1, D_MODEL), lambda i, k: (0, 0)),
            pl.BlockSpec((1, D_MODEL), lambda i, k: (0, 0)),
            pl.BlockSpec((1, D_MODEL), lambda i, k: (0, 0)),
            pl.BlockSpec((1, D_MODEL), lambda i, k: (0, 0)),
        ],
        out_specs=pl.BlockSpec((tm, D_MODEL), lambda i, k: (i, 0)),
        out_shape=jax.ShapeDtypeStruct((m, D_MODEL), f32),
        scratch_shapes=[pltpu.VMEM((tm, D_MODEL), f32)],
        compiler_params=pltpu.CompilerParams(
            dimension_semantics=("arbitrary", "arbitrary"), vmem_limit_bytes=VMEM_LIMIT),
        name="outproj_ln1",
    )(att, ssm, w_out, x2d, gi, bi, g1, b1)


def _max01(v):
    return jnp.max(jnp.max(v, axis=1, keepdims=True), axis=0, keepdims=True)


def _router_body(h_ref, wr_ref, br_ref, tri_ref, eidx_ref, gate_ref, rank_ref, cnt_ref, running):
    tt = h_ref.shape[0]
    shape3 = (N_EXPERT_GROUPS, EXPERTS_PER_GROUP, tt)

    @pl.when(pl.program_id(0) == 0)
    def _():
        running[...] = jnp.zeros_like(running)

    logits = lax.dot_general(wr_ref[...], h_ref[...], (((1,), (1,)), ((), ())),
                             precision=lax.Precision.HIGHEST, preferred_element_type=f32)
    scores = (1.0 / (1.0 + jnp.exp(-logits)))
    sel3 = (scores + br_ref[...]).reshape(shape3)
    scores3 = scores.reshape(shape3)
    within = lax.broadcasted_iota(i32, shape3, 1).astype(f32)
    m1 = jnp.max(sel3, axis=1, keepdims=True)
    i1 = jnp.min(jnp.where(sel3 == m1, within, float(EXPERTS_PER_GROUP)), axis=1, keepdims=True)
    m2 = jnp.max(jnp.where(within == i1, -jnp.inf, sel3), axis=1, keepdims=True)
    gs = m1 + m2
    giota = lax.broadcasted_iota(i32, gs.shape, 0).astype(f32)
    gmask = jnp.zeros(gs.shape, f32)
    for _ in range(TOPK_GROUPS):
        gm = jnp.max(gs, axis=0, keepdims=True)
        gi = jnp.min(jnp.where(gs == gm, giota, float(N_EXPERT_GROUPS)), axis=0, keepdims=True)
        hit = giota == gi
        gmask = jnp.where(hit, 1.0, gmask)
        gs = jnp.where(hit, -jnp.inf, gs)
    selm = jnp.where(gmask > 0.0, sel3, -jnp.inf)
    eiota = (lax.broadcasted_iota(i32, shape3, 0) * EXPERTS_PER_GROUP
             + lax.broadcasted_iota(i32, shape3, 1)).astype(f32)
    eidx, gates, hits = [], [], []
    member = jnp.zeros(shape3, f32)
    gsum = jnp.zeros((1, 1, tt), f32)
    for _ in range(TOP_K):
        m = _max01(selm)
        ei = -_max01(-jnp.where(selm == m, eiota, float(N_EXPERTS)))
        hit = eiota == ei
        gk = jnp.sum(jnp.sum(jnp.where(hit, scores3, 0.0), axis=1, keepdims=True), axis=0, keepdims=True)
        eidx.append(ei)
        gates.append(gk)
        hits.append(hit)
        gsum = gsum + gk
        member = jnp.where(hit, 1.0, member)
        selm = jnp.where(hit, -jnp.inf, selm)
    member2 = member.reshape(N_EXPERTS, tt)
    incl = jnp.dot(member2.astype(bf16), tri_ref[...], preferred_element_type=f32)
    base = (running[...] + (incl - member2)).reshape(shape3)
    for k in range(TOP_K):
        rk = jnp.sum(jnp.sum(jnp.where(hits[k], base, 0.0), axis=1, keepdims=True), axis=0, keepdims=True)
        eidx_ref[k:k + 1, :] = eidx[k].reshape(1, tt).astype(i32)
        gate_ref[k:k + 1, :] = (gates[k] / gsum * ROUTED_SCALE).reshape(1, tt)
        rank_ref[k:k + 1, :] = rk.reshape(1, tt).astype(i32)
    total = running[...] + incl[:, tt - 1:tt]
    running[...] = total
    cnt_ref[...] = total.astype(i32)


def _router(h1, w_router_t, b_router_col, tri, *, tt):
    m = h1.shape[0]
    return pl.pallas_call(
        _router_body,
        grid=(m // tt,),
        in_specs=[
            pl.BlockSpec((tt, D_MODEL), lambda i: (i, 0)),
            pl.BlockSpec((N_EXPERTS, D_MODEL), lambda i: (0, 0)),
            pl.BlockSpec((N_EXPERTS, 1), lambda i: (0, 0)),
            pl.BlockSpec((tt, tt), lambda i: (0, 0)),
        ],
        out_specs=[
            pl.BlockSpec((TOP_K, tt), lambda i: (0, i)),
            pl.BlockSpec((TOP_K, tt), lambda i: (0, i)),
            pl.BlockSpec((TOP_K, tt), lambda i: (0, i)),
            pl.BlockSpec((N_EXPERTS, 1), lambda i: (0, 0)),
        ],
        out_shape=[
            jax.ShapeDtypeStruct((TOP_K, m), i32),
            jax.ShapeDtypeStruct((TOP_K, m), f32),
            jax.ShapeDtypeStruct((TOP_K, m), i32),
            jax.ShapeDtypeStruct((N_EXPERTS, 1), i32),
        ],
        scratch_shapes=[pltpu.VMEM((N_EXPERTS, 1), f32)],
        compiler_params=pltpu.CompilerParams(
            dimension_semantics=("arbitrary",), vmem_limit_bytes=VMEM_LIMIT),
        name="router",
    )(h1, w_router_t, b_router_col, tri)


FILL_SIZES = (128, 64, 32, 16, 8, 4, 2, 1)
PACKED_WIDTH = D_MODEL // 2
ROW_TILE = PACKED_WIDTH // LANES
HIGH_HALF = 0xFFFF0000
assert ROW_TILE == SUBLANES


def _row_copy(src_ref, src_row, dst_ref, dst_row, sem):
    src = src_ref.at[pl.ds(pl.multiple_of(src_row * ROW_TILE, ROW_TILE), ROW_TILE)]
    dst = dst_ref.at[pl.ds(pl.multiple_of(dst_row * ROW_TILE, ROW_TILE), ROW_TILE)]
    return pltpu.make_async_copy(src, dst, sem)


def _pack_bf16_pairs(v):
    w = v.shape[1] // 2
    bits = lambda t: lax.bitcast_convert_type(t.astype(bf16).astype(f32), jnp.uint32)
    return (bits(v[:, :w]) >> 16) | (bits(v[:, w:]) & jnp.uint32(HIGH_HALF))


def _unpack_bf16_pairs(p):
    lo = lax.bitcast_convert_type(p << 16, f32)
    hi = lax.bitcast_convert_type(p & jnp.uint32(HIGH_HALF), f32)
    return jnp.concatenate([lo, hi], axis=1)


def _store_tile_rows(ref, packed):
    n = packed.shape[0]
    for j in range(ROW_TILE):
        ref[pl.ds(j, n, stride=ROW_TILE), :] = packed[:, j * LANES:(j + 1) * LANES]


def _load_tile_rows(ref, n):
    return jnp.concatenate([ref[pl.ds(j, n, stride=ROW_TILE), :] for j in range(ROW_TILE)], axis=1)


def _dispatch_body(dest_ref, cnt_ref, pstart_ref, padded_ref, h_ref, xs_ref, packed, zbuf, sem, zsem, *, tt):
    i = pl.program_id(0)
    _store_tile_rows(packed, _pack_bf16_pairs(h_ref[...]))

    @pl.when(i == 0)
    def _():
        zbuf[...] = jnp.zeros_like(zbuf)

        def fill(e, wait):
            cnt = cnt_ref[e]
            first = pstart_ref[e] + cnt
            filler = padded_ref[e] - cnt
            for size in FILL_SIZES:
                @pl.when((filler & size) != 0)
                def _():
                    start = pl.multiple_of((first + (filler & (-2 * size))) * ROW_TILE, ROW_TILE)
                    cp = pltpu.make_async_copy(zbuf.at[pl.ds(0, size * ROW_TILE)],
                                               xs_ref.at[pl.ds(start, size * ROW_TILE)], zsem)
                    if wait:
                        cp.wait()
                    else:
                        cp.start()

        def start_fill(e, c):
            fill(e, False)
            return c

        def wait_fill(e, c):
            fill(e, True)
            return c

        lax.fori_loop(0, N_EXPERTS, start_fill, 0)
        lax.fori_loop(0, N_EXPERTS, wait_fill, 0)

    def issue(t, c):
        for k in range(TOP_K):
            _row_copy(packed, t, xs_ref, dest_ref[0, k, t], sem).start(priority=k % 2)
        return c

    def drain(t, c):
        for k in range(TOP_K):
            _row_copy(packed, 0, xs_ref, 0, sem).wait()
        return c

    lax.fori_loop(0, tt, issue, 0, unroll=8)
    lax.fori_loop(0, tt, drain, 0, unroll=8)


def _dispatch(dest3, counts, pstart, padded, h1, *, rows, tt):
    m = h1.shape[0]
    smem = pl.BlockSpec(memory_space=pltpu.SMEM)
    return pl.pallas_call(
        functools.partial(_dispatch_body, tt=tt),
        grid=(m // tt,),
        in_specs=[
            pl.BlockSpec((1, TOP_K, tt), lambda i: (i, 0, 0), memory_space=pltpu.SMEM),
            smem, smem, smem,
            pl.BlockSpec((tt, D_MODEL), lambda i: (i, 0)),
        ],
        out_specs=pl.BlockSpec(memory_space=pl.ANY),
        out_shape=jax.ShapeDtypeStruct((rows * ROW_TILE, LANES), jnp.uint32),
        scratch_shapes=[
            pltpu.VMEM((tt * ROW_TILE, LANES), jnp.uint32),
            pltpu.VMEM((FILL_SIZES[0] * ROW_TILE, LANES), jnp.uint32),
            pltpu.SemaphoreType.DMA(()),
            pltpu.SemaphoreType.DMA(()),
        ],
        compiler_params=pltpu.CompilerParams(
            dimension_semantics=("arbitrary",), vmem_limit_bytes=VMEM_LIMIT, has_side_effects=True),
        name="moe_dispatch",
    )(dest3, counts, pstart, padded, h1)


def _experts_body(blk_e_ref, nused_ref, first_ref, slot_ref, next_e_ref, x_ref, wg_hbm, wu_hbm, wd_hbm, y_ref,
                  rawg, rawu, rawd, wgb, wub, wdb, sem):
    i = pl.program_id(0)

    def weight_copies(e, s):
        return (pltpu.make_async_copy(wg_hbm.at[e], rawg.at[s], sem.at[s, 0]),
                pltpu.make_async_copy(wu_hbm.at[e], rawu.at[s], sem.at[s, 1]),
                pltpu.make_async_copy(wd_hbm.at[e], rawd.at[s], sem.at[s, 2]))

    @pl.when(i < nused_ref[0])
    def _():
        e = blk_e_ref[i]
        s = slot_ref[i]

        @pl.when(first_ref[i] == 1)
        def _():
            @pl.when(i == 0)
            def _():
                for cp in weight_copies(e, s):
                    cp.start()

            for cp in weight_copies(e, s):
                cp.wait()
            wgb[...] = rawg[s].astype(bf16)
            wub[...] = rawu[s].astype(bf16)
            wdb[...] = rawd[s].astype(bf16)

            @pl.when(next_e_ref[i] >= 0)
            def _():
                for cp in weight_copies(next_e_ref[i], 1 - s):
                    cp.start()

        x = _unpack_bf16_pairs(_load_tile_rows(x_ref, MOE_BLOCK)).astype(bf16)
        hg = jnp.dot(x, wgb[...], preferred_element_type=f32)
        hu = jnp.dot(x, wub[...], preferred_element_type=f32)
        hb = (_silu(hg) * hu).astype(bf16)
        _store_tile_rows(y_ref, _pack_bf16_pairs(jnp.dot(hb, wdb[...], preferred_element_type=f32)))


def _experts(blk_e, nused, first, slot, next_e, xs, w_gate, w_up, w_down):
    rows = xs.shape[0] // ROW_TILE
    nblk = rows // MOE_BLOCK
    blk = lambda i, be, nu, *_: (jnp.minimum(i, nu[0] - 1), 0)
    hbm = pl.BlockSpec(memory_space=pl.ANY)
    grid_spec = pltpu.PrefetchScalarGridSpec(
        num_scalar_prefetch=5,
        grid=(nblk,),
        in_specs=[pl.BlockSpec((MOE_BLOCK * ROW_TILE, LANES), blk), hbm, hbm, hbm],
        out_specs=pl.BlockSpec((MOE_BLOCK * ROW_TILE, LANES), blk),
        scratch_shapes=[
            pltpu.VMEM((2, D_MODEL, EXPERT_DIM), f32),
            pltpu.VMEM((2, D_MODEL, EXPERT_DIM), f32),
            pltpu.VMEM((2, EXPERT_DIM, D_MODEL), f32),
            pltpu.VMEM((D_MODEL, EXPERT_DIM), bf16),
            pltpu.VMEM((D_MODEL, EXPERT_DIM), bf16),
            pltpu.VMEM((EXPERT_DIM, D_MODEL), bf16),
            pltpu.SemaphoreType.DMA((2, 3)),
        ],
    )
    return pl.pallas_call(
        _experts_body,
        grid_spec=grid_spec,
        out_shape=jax.ShapeDtypeStruct((rows * ROW_TILE, LANES), jnp.uint32),
        compiler_params=pltpu.CompilerParams(
            dimension_semantics=("arbitrary",), vmem_limit_bytes=VMEM_LIMIT),
        name="routed_experts",
    )(blk_e, nused, first, slot, next_e, xs, w_gate, w_up, w_down)


def _combine_body(dest_ref, destn_ref, h_ref, gate_ref, wsg_ref, wsu_ref, wsd_ref, g2_ref, b2_ref, ys_ref,
                  o_ref, buf, sem, *, tm):
    i = pl.program_id(0)
    slot = i % 2

    def issue(d_ref, s):
        def body(t, c):
            for k in range(TOP_K):
                _row_copy(ys_ref, d_ref[0, k, t], buf.at[s, k], t, sem.at[s]).start(priority=k % 2)
            return c
        lax.fori_loop(0, tm, body, 0, unroll=8)

    @pl.when(i == 0)
    def _():
        issue(dest_ref, 0)

    @pl.when(i + 1 < pl.num_programs(0))
    def _():
        issue(destn_ref, 1 - slot)

    h = h_ref[...]
    hb = h.astype(bf16)
    sg = jnp.dot(hb, wsg_ref[...], preferred_element_type=f32)
    su = jnp.dot(hb, wsu_ref[...], preferred_element_type=f32)
    ffn = jnp.dot((_silu(sg) * su).astype(bf16), wsd_ref[...], preferred_element_type=f32)

    def drain(t, c):
        for k in range(TOP_K):
            _row_copy(ys_ref, 0, buf.at[slot, k], 0, sem.at[slot]).wait()
        return c
    lax.fori_loop(0, tm, drain, 0, unroll=8)

    gate = gate_ref[...]
    for k in range(TOP_K):
        ffn = ffn + gate[:, k:k + 1] * _unpack_bf16_pairs(_load_tile_rows(buf.at[slot, k], tm))
    o_ref[...] = _layer_norm(DEEPNORM_ALPHA * h + ffn, g2_ref[...], b2_ref[...])


def _combine(dest3, h1, ys, gate_tok, wsg, wsu, wsd, g2, b2, *, tm):
    m = h1.shape[0]
    nt = m // tm
    return pl.pallas_call(
        functools.partial(_combine_body, tm=tm),
        grid=(nt,),
        in_specs=[
            pl.BlockSpec((1, TOP_K, tm), lambda i: (i, 0, 0), memory_space=pltpu.SMEM),
            pl.BlockSpec((1, TOP_K, tm), lambda i: (jnp.minimum(i + 1, nt - 1), 0, 0), memory_space=pltpu.SMEM),
            pl.BlockSpec((tm, D_MODEL), lambda i: (i, 0)),
            pl.BlockSpec((tm, TOP_K), lambda i: (i, 0)),
            pl.BlockSpec((D_MODEL, EXPERT_DIM), lambda i: (0, 0)),
            pl.BlockSpec((D_MODEL, EXPERT_DIM), lambda i: (0, 0)),
            pl.BlockSpec((EXPERT_DIM, D_MODEL), lambda i: (0, 0)),
            pl.BlockSpec((1, D_MODEL), lambda i: (0, 0)),
            pl.BlockSpec((1, D_MODEL), lambda i: (0, 0)),
            pl.BlockSpec(memory_space=pl.ANY),
        ],
        out_specs=pl.BlockSpec((tm, D_MODEL), lambda i: (i, 0)),
        out_shape=jax.ShapeDtypeStruct((m, D_MODEL), f32),
        scratch_shapes=[
            pltpu.VMEM((2, TOP_K, tm * ROW_TILE, LANES), jnp.uint32),
            pltpu.SemaphoreType.DMA((2,)),
        ],
        compiler_params=pltpu.CompilerParams(
            dimension_semantics=("arbitrary",), vmem_limit_bytes=VMEM_LIMIT),
        name="combine_ln2",
    )(dest3, dest3, h1, gate_tok, wsg, wsu, wsd, g2, b2, ys)


def _rope_tables(pos, width):
    half = ROPE_DIM // 2
    inv_freq = jnp.power(ROPE_THETA, -jnp.arange(0, ROPE_DIM, 2, dtype=f32) / ROPE_DIM)
    ang = pos.astype(f32)[:, None] * inv_freq[None, :]
    cos, sin = jnp.cos(ang), jnp.sin(ang)
    n = pos.shape[0]
    pad = jnp.zeros((n, HEAD_DIM - ROPE_DIM), f32)
    zero = jnp.zeros((n, half), f32)
    c = jnp.concatenate([cos, cos, pad + 1.0], axis=1)
    s1 = jnp.concatenate([-sin, zero, pad], axis=1)
    s2 = jnp.concatenate([zero, sin, pad], axis=1)
    tab = jnp.stack([c, s1, s2])
    return jnp.tile(tab, (1, 1, width // HEAD_DIM))


def _group_lanes(v):
    v = v.reshape(SSM_GROUPS, SSM_HEADS_PER_GROUP)
    return jnp.pad(v, ((0, 0), (0, LANES - SSM_HEADS_PER_GROUP))).reshape(1, DT_WIDTH)


def kernel(x, meta_tokens, ln_in_g, ln_in_b, w_in, conv_w, conv_b, dt_bias, a_log, d_skip, ssm_norm_g, att_norm_g, attn_sinks, w_out, ln1_g, ln1_b, w_router, b_router, w_gate, w_up, w_down, ws_gate, ws_up, ws_down, ln2_g, ln2_b):
    batch, seq, d = x.shape
    assert d == D_MODEL and seq % BLOCK == 0 and meta_tokens.shape == (N_META, D_MODEL)
    assert w_in.shape[0] == 1, "single layer"
    n_tok = batch * seq
    nblk = seq // BLOCK
    row = lambda v: v.reshape(1, -1).astype(f32)

    x2d = x.reshape(n_tok, D_MODEL)
    gi, bi = row(ln_in_g), row(ln_in_b)
    w_main = w_in[0, :, :MAIN_WIDTH].astype(bf16)
    w_dt = w_in[0, :, MAIN_WIDTH:].reshape(D_MODEL, SSM_GROUPS, SSM_HEADS_PER_GROUP)
    w_dt = jnp.pad(w_dt, ((0, 0), (0, 0), (0, LANES - SSM_HEADS_PER_GROUP))).reshape(D_MODEL, DT_WIDTH).astype(bf16)

    proj, dtx = _ln_inproj(x2d, gi, bi, w_main, w_dt, tm=1024, tn=512)
    proj_m, dtx_m = _ln_inproj(meta_tokens.astype(f32), gi, bi, w_main, w_dt, tm=N_META, tn=512)
    proj_m = jnp.pad(proj_m, ((BLOCK - N_META, 0), (0, 0)))
    dtx_m = jnp.pad(dtx_m, ((CHUNK - N_META, 0), (0, 0)))

    gw = ATT_GROUP * HEAD_DIM
    tab = _rope_tables(N_META + jnp.arange(seq), gw)
    tab_meta = _rope_tables(jnp.arange(BLOCK), gw)
    kv_meta = jnp.roll(proj_m[:, ATT_WIDTH:ATT_WIDTH + 2 * KV_WIDTH], N_META, axis=0)
    att = _attention(proj, kv_meta, tab, tab_meta, attn_sinks[0].astype(f32), row(att_norm_g[0]),
                     batch=batch, nblk=nblk)

    conv_w0 = conv_w[0].astype(f32)
    conv_b0 = row(conv_b[0])
    dtb_x = _group_lanes(dt_bias[0].astype(f32))
    alog_x = _group_lanes(a_log[0].astype(f32))
    dskip_ch = jnp.repeat(d_skip[0].astype(f32), HEAD_DIM).reshape(1, SSM_WIDTH)
    ng = row(ssm_norm_g[0])
    zeros_halo = jnp.zeros((8, CONV_CH), f32)
    zeros_state = jnp.zeros((SSM_GROUPS, SSM_GROUP_WIDTH, SSM_STATE), f32)
    _, s_meta = _ssd(proj_m, dtx_m, zeros_halo, zeros_state, conv_w0, conv_b0, dtb_x, alog_x, dskip_ch, ng,
                     batch=1, nchunk=1, vstart=CHUNK - N_META, emit_state=True)
    halo = proj_m[CHUNK - 8:, XBC_COL:].astype(f32)
    (ssm,) = _ssd(proj, dtx, halo, s_meta, conv_w0, conv_b0, dtb_x, alog_x, dskip_ch, ng,
                  batch=batch, nchunk=nblk, vstart=0, emit_state=False)

    h1 = _outproj(att, ssm, w_out[0].astype(bf16), x2d, gi, bi, row(ln1_g[0]), row(ln1_b[0]),
                  tm=512, tk=1024)

    tt = 512
    tri = (jnp.arange(tt)[:, None] <= jnp.arange(tt)[None, :]).astype(bf16)
    eidx, gate, rank, counts = _router(h1, w_router[0].T.astype(f32), b_router[0].reshape(N_EXPERTS, 1).astype(f32),
                                       tri, tt=tt)
    counts = counts.reshape(N_EXPERTS)
    padded = (counts + MOE_BLOCK - 1) // MOE_BLOCK * MOE_BLOCK
    pend = jnp.cumsum(padded)
    pstart = pend - padded
    first_row = jnp.sum(jnp.where(eidx[..., None] == jnp.arange(N_EXPERTS, dtype=i32), pstart.astype(i32), 0), axis=-1)
    dest = first_row + rank
    n_blocks = n_tok * TOP_K // MOE_BLOCK + N_EXPERTS
    blk_first = jnp.arange(n_blocks, dtype=i32) * MOE_BLOCK
    blk_e = jnp.minimum(jnp.sum(pend[None, :] <= blk_first[:, None], axis=1), N_EXPERTS - 1).astype(i32)
    nused = (pend[-1] // MOE_BLOCK).astype(i32).reshape(1)
    rows = n_blocks * MOE_BLOCK
    tiles = lambda t: dest.reshape(TOP_K, n_tok // t, t).transpose(1, 0, 2)

    td, tc = 256, 128
    xs = _dispatch(tiles(td), counts, pstart.astype(i32), padded.astype(i32), h1, rows=rows, tt=td)
    eids = jnp.arange(N_EXPERTS, dtype=i32)
    nonempty = counts > 0
    ordinal = jnp.cumsum(nonempty.astype(i32)) - nonempty.astype(i32)
    later = (eids[None, :] > eids[:, None]) & nonempty[None, :]
    next_nonempty = jnp.min(jnp.where(later, eids[None, :], N_EXPERTS), axis=1)
    next_nonempty = jnp.where(next_nonempty == N_EXPERTS, -1, next_nonempty).astype(i32)
    onehot_e = blk_e[:, None] == eids[None, :]
    pick = lambda table: jnp.sum(jnp.where(onehot_e, table[None, :], 0), axis=1).astype(i32)
    first = jnp.concatenate([jnp.ones((1,), i32), (blk_e[1:] != blk_e[:-1]).astype(i32)])
    ys = _experts(blk_e, nused, first, pick(ordinal) % 2, pick(next_nonempty), xs, w_gate[0], w_up[0], w_down[0])
    out = _combine(tiles(tc), h1, ys, gate.T, ws_gate[0].astype(bf16), ws_up[0].astype(bf16),
                   ws_down[0].astype(bf16), row(ln2_g[0]), row(ln2_b[0]), tm=tc)
    return out.reshape(batch, seq, D_MODEL)
```

```python
import functools
import math

import jax
import jax.numpy as jnp
from jax import lax
from jax.experimental import pallas as pl
from jax.experimental.pallas import tpu as pltpu

f32 = jnp.float32
bf16 = jnp.bfloat16
i32 = jnp.int32

D_MODEL = 2048
N_META = 16
HEAD_DIM = 64
ATT_HEADS = 32
ATT_KV_HEADS = 4
ATT_GROUP = ATT_HEADS // ATT_KV_HEADS
ATT_WIDTH = 2048
KV_WIDTH = 256
BLOCK = 128
ROPE_DIM = 16
ROPE_THETA = 500000.0
SSM_WIDTH = 4096
SSM_HEADS = 64
SSM_GROUPS = 8
SSM_HEADS_PER_GROUP = SSM_HEADS // SSM_GROUPS
SSM_STATE = 128
SSM_GROUP_WIDTH = SSM_WIDTH // SSM_GROUPS
CONV_WIDTH = 4
CHUNK = 128
CONV_CH = SSM_WIDTH + 2 * SSM_GROUPS * SSM_STATE
MAIN_WIDTH = ATT_WIDTH + 2 * KV_WIDTH + SSM_WIDTH + CONV_CH
Z_COL = ATT_WIDTH + 2 * KV_WIDTH
XBC_COL = Z_COL + SSM_WIDTH
N_EXPERTS = 64
EXPERT_DIM = 512
TOP_K = 8
N_EXPERT_GROUPS = 8
EXPERTS_PER_GROUP = N_EXPERTS // N_EXPERT_GROUPS
TOPK_GROUPS = 4
ROUTED_SCALE = 2.5
MOE_BLOCK = 256
DEEPNORM_ALPHA = 2.0 ** 0.25
LN_EPS = 1e-5
RMS_EPS = 1e-6
NEG_INF = -1e30
LANES = 128
SUBLANES = 8
DT_WIDTH = SSM_GROUPS * LANES
VMEM_LIMIT = 56 * 1024 * 1024


def _layer_norm(x, g, b):
    mu = jnp.mean(x, axis=-1, keepdims=True)
    xc = x - mu
    var = jnp.mean(xc * xc, axis=-1, keepdims=True)
    return xc * lax.rsqrt(var + LN_EPS) * g + b


def _silu(x):
    return x * (1.0 / (1.0 + jnp.exp(-x)))


def _softplus(x):
    return jnp.maximum(x, 0.0) + jnp.log1p(jnp.exp(-jnp.abs(x)))


def _ln_inproj_body(x_ref, g_ref, b_ref, w_ref, wdt_ref, o_ref, dt_ref, h_scr):
    @pl.when(pl.program_id(1) == 0)
    def _():
        h = _layer_norm(x_ref[...], g_ref[...], b_ref[...]).astype(bf16)
        h_scr[...] = h
        dt_ref[...] = jnp.dot(h, wdt_ref[...], preferred_element_type=f32)

    o_ref[...] = jnp.dot(h_scr[...], w_ref[...], preferred_element_type=f32).astype(bf16)


def _ln_inproj(x2d, g, b, w_main, w_dt, *, tm, tn):
    m = x2d.shape[0]
    return pl.pallas_call(
        _ln_inproj_body,
        grid=(m // tm, MAIN_WIDTH // tn),
        in_specs=[
            pl.BlockSpec((tm, D_MODEL), lambda i, j: (i, 0)),
            pl.BlockSpec((1, D_MODEL), lambda i, j: (0, 0)),
            pl.BlockSpec((1, D_MODEL), lambda i, j: (0, 0)),
            pl.BlockSpec((D_MODEL, tn), lambda i, j: (0, j)),
            pl.BlockSpec((D_MODEL, LANES), lambda i, j: (0, 0)),
        ],
        out_specs=[
            pl.BlockSpec((tm, tn), lambda i, j: (i, j)),
            pl.BlockSpec((tm, LANES), lambda i, j: (i, 0)),
        ],
        out_shape=[
            jax.ShapeDtypeStruct((m, MAIN_WIDTH), bf16),
            jax.ShapeDtypeStruct((m, LANES), f32),
        ],
        scratch_shapes=[pltpu.VMEM((tm, D_MODEL), bf16)],
        compiler_params=pltpu.CompilerParams(
            dimension_semantics=("arbitrary", "arbitrary"), vmem_limit_bytes=VMEM_LIMIT),
        name="ln_inproj",
    )(x2d, g, b, w_main, w_dt)


def _rotate(t, tab):
    w = t.shape[-1]
    half = ROPE_DIM // 2
    return (t * tab[0]
            + pltpu.roll(t, w - half, 1) * tab[1]
            + pltpu.roll(t, half, 1) * tab[2])


def _attention_body(sink_ref, q_ref, kvc_ref, kvp_ref, kvm_ref, tabc_ref, tabp_ref, tabm_ref,
                    g_ref, o_ref):
    j = pl.program_id(1)
    tabc = tabc_ref[...]
    kc = _rotate(kvc_ref[:, :KV_WIDTH].astype(f32), tabc[:, :, :KV_WIDTH]).astype(bf16)
    kp = _rotate(kvp_ref[:, :KV_WIDTH].astype(f32), tabp_ref[:, :, :KV_WIDTH]).astype(bf16)
    km = _rotate(kvm_ref[:, :KV_WIDTH].astype(f32), tabm_ref[:, :, :KV_WIDTH]).astype(bf16)
    v_all = jnp.concatenate([kvp_ref[:, KV_WIDTH:], kvc_ref[:, KV_WIDTH:], kvm_ref[:, KV_WIDTH:]], axis=0)

    rows = ATT_GROUP * BLOCK
    r = lax.broadcasted_iota(i32, (rows, BLOCK), 0) % BLOCK
    c = lax.broadcasted_iota(i32, (rows, BLOCK), 1)
    lower = c <= r
    prev_ok = (c > r) & (j > 0)
    meta_ok = c < N_META
    head_of_row = lax.broadcasted_iota(i32, (rows, 1), 0) // BLOCK
    nt_dims = (((1,), (1,)), ((), ()))

    outs = []
    for g in range(ATT_KV_HEADS):
        gw = ATT_GROUP * HEAD_DIM
        hs = slice(g * HEAD_DIM, (g + 1) * HEAD_DIM)
        qg = _rotate(q_ref[:, g * gw:(g + 1) * gw].astype(f32), tabc) * (HEAD_DIM ** -0.5)
        qs = jnp.concatenate([qg[:, h * HEAD_DIM:(h + 1) * HEAD_DIM] for h in range(ATT_GROUP)],
                             axis=0).astype(bf16)
        s_cur = lax.dot_general(qs, kc[:, hs], nt_dims, preferred_element_type=f32)
        s_prev = lax.dot_general(qs, kp[:, hs], nt_dims, preferred_element_type=f32)
        s_meta = lax.dot_general(qs, km[:, hs], nt_dims, preferred_element_type=f32)
        s_band = jnp.where(lower, s_cur, jnp.where(prev_ok, s_prev, NEG_INF))
        s_meta = jnp.where(meta_ok, s_meta, NEG_INF)
        sink = jnp.zeros((rows, 1), f32)
        for h in range(ATT_GROUP):
            sink = jnp.where(head_of_row == h, sink_ref[g * ATT_GROUP + h], sink)
        m = jnp.maximum(jnp.max(jnp.maximum(s_band, s_meta), axis=1, keepdims=True), sink)
        p_band = jnp.exp(s_band - m)
        p_meta = jnp.exp(s_meta - m)
        denom = jnp.sum(p_band + p_meta, axis=1, keepdims=True) + jnp.exp(sink - m)
        p_cur = jnp.where(lower, p_band, 0.0)
        p = jnp.concatenate([p_band - p_cur, p_cur, p_meta], axis=1).astype(bf16)
        o = jnp.dot(p, v_all[:, hs], preferred_element_type=f32) / denom
        outs.append(jnp.concatenate([o[h * BLOCK:(h + 1) * BLOCK] for h in range(ATT_GROUP)], axis=1))
    att = jnp.concatenate(outs, axis=1)
    att = att * lax.rsqrt(jnp.mean(att * att, axis=-1, keepdims=True) + RMS_EPS) * g_ref[...]
    o_ref[...] = att.astype(bf16)


def _attention(proj, kv_meta, tab, tab_meta, sinks, att_g, *, batch, nblk):
    m = proj.shape[0]
    kvb = ATT_WIDTH // (2 * KV_WIDTH)
    gw = ATT_GROUP * HEAD_DIM
    return pl.pallas_call(
        _attention_body,
        grid=(batch, nblk),
        in_specs=[
            pl.BlockSpec(memory_space=pltpu.SMEM),
            pl.BlockSpec((BLOCK, ATT_WIDTH), lambda b, j: (b * nblk + j, 0)),
            pl.BlockSpec((BLOCK, 2 * KV_WIDTH), lambda b, j: (b * nblk + j, kvb)),
            pl.BlockSpec((BLOCK, 2 * KV_WIDTH), lambda b, j: (b * nblk + jnp.maximum(j - 1, 0), kvb)),
            pl.BlockSpec((BLOCK, 2 * KV_WIDTH), lambda b, j: (0, 0)),
            pl.BlockSpec((3, BLOCK, gw), lambda b, j: (0, j, 0)),
            pl.BlockSpec((3, BLOCK, gw), lambda b, j: (0, jnp.maximum(j - 1, 0), 0)),
            pl.BlockSpec((3, BLOCK, gw), lambda b, j: (0, 0, 0)),
            pl.BlockSpec((1, ATT_WIDTH), lambda b, j: (0, 0)),
        ],
        out_specs=pl.BlockSpec((BLOCK, ATT_WIDTH), lambda b, j: (b * nblk + j, 0)),
        out_shape=jax.ShapeDtypeStruct((m, ATT_WIDTH), bf16),
        compiler_params=pltpu.CompilerParams(
            dimension_semantics=("arbitrary", "arbitrary"), vmem_limit_bytes=VMEM_LIMIT),
        name="swa_attention",
    )(sinks, proj, proj, proj, kv_meta, tab, tab, tab_meta, att_g)


def _cumsum_rows(x):
    n = x.shape[0]
    row = lax.broadcasted_iota(i32, x.shape, 0)
    k = 1
    while k < n:
        x = x + jnp.where(row >= k, pltpu.roll(x, k, 0), 0.0)
        k *= 2
    return x


def _expand_heads(col):
    lane = lax.broadcasted_iota(i32, col.shape, 1) // HEAD_DIM
    per_tile = LANES // HEAD_DIM
    return jnp.concatenate(
        [jnp.take_along_axis(col, lane + per_tile * j, axis=1) for j in range(SSM_GROUP_WIDTH // LANES)], axis=1)


def _ssd_body(proj_ref, dt_ref, halo_ref, s0_ref, cw_ref, cb_ref, dtb_ref, alog_ref, dskip_ref, ng_ref,
              *rest, vstart, emit_state):
    if emit_state:
        y_ref, sout_ref, state, halo, work = rest
    else:
        y_ref, state, halo, work = rest

    @pl.when(pl.program_id(1) == 0)
    def _():
        state[...] = s0_ref[...]
        halo[...] = halo_ref[...]

    if vstart == 0:
        keep_valid = lambda v: v
    else:
        valid = lax.broadcasted_iota(i32, (CHUNK, 1), 0) >= vstart
        keep_valid = lambda v: jnp.where(valid, v, 0.0)
    lane = lax.broadcasted_iota(i32, (CHUNK, LANES), 1)
    causal =(lax.broadcasted_iota(i32, (CHUNK, CHUNK), 0) >= lax.broadcasted_iota(i32, (CHUNK, CHUNK), 1))

    def conv(col, width):
        u = proj_ref[:, pl.ds(pl.multiple_of(XBC_COL + col, LANES), width)].astype(f32)
        cs = pl.ds(pl.multiple_of(col, LANES), width)
        work[0:8, 0:width] = halo[:, cs]
        work[8:8 + CHUNK, 0:width] = u
        w = cw_ref[:, cs]
        acc = cb_ref[:, cs] + u * w[CONV_WIDTH - 1:CONV_WIDTH]
        for jj in range(CONV_WIDTH - 1):
            acc = acc + work[5 + jj:5 + jj + CHUNK, 0:width] * w[jj:jj + 1]
        halo[:, cs] = u[CHUNK - 8:]
        return keep_valid(_silu(acc))

    def group(g, carry):
        gs = pl.ds(pl.multiple_of(g * SSM_GROUP_WIDTH, SSM_GROUP_WIDTH), SSM_GROUP_WIDTH)
        gl = pl.ds(pl.multiple_of(g * LANES, LANES), LANES)
        xs = conv(g * SSM_GROUP_WIDTH, SSM_GROUP_WIDTH)
        bm = conv(SSM_WIDTH + g * SSM_STATE, SSM_STATE)
        cm = conv(SSM_WIDTH + SSM_GROUPS * SSM_STATE + g * SSM_STATE, SSM_STATE)

        dt_raw = jnp.take_along_axis(dt_ref[...], (lane + g * SSM_HEADS_PER_GROUP) % LANES, axis=1)
        dt = keep_valid(_softplus(dt_raw + dtb_ref[:, gl]))
        a_cs = _cumsum_rows(dt * -jnp.exp(alog_ref[:, gl]))
        a_cs_t = a_cs.T
        a_last = a_cs[CHUNK - 1:CHUNK]

        xdt = xs * _expand_heads(dt)
        bmb = bm.astype(bf16)
        cmb = cm.astype(bf16)
        cb = lax.dot_general(cmb, bmb, (((1,), (1,)), ((), ())), preferred_element_type=f32)
        xdtb = xdt.astype(bf16)
        ys = []
        for i in range(SSM_HEADS_PER_GROUP):
            seg = a_cs[:, i:i + 1] - a_cs_t[i:i + 1, :]
            lmat = (cb * jnp.exp(jnp.where(causal, seg, -jnp.inf))).astype(bf16)
            ys.append(jnp.dot(lmat, xdtb[:, i * HEAD_DIM:(i + 1) * HEAD_DIM], preferred_element_type=f32))
        y = jnp.concatenate(ys, axis=1)

        st = state[g]
        y_off = lax.dot_general(cmb, st.astype(bf16), (((1,), (1,)), ((), ())), preferred_element_type=f32)
        y = y + y_off * _expand_heads(jnp.exp(a_cs))

        x_end = (xdt * _expand_heads(jnp.exp(a_last - a_cs))).T.astype(bf16)
        upd = jnp.dot(x_end, bmb, preferred_element_type=f32)
        for i in range(SSM_HEADS_PER_GROUP):
            rows = slice(i * HEAD_DIM, (i + 1) * HEAD_DIM)
            state[g, rows] = st[rows] * jnp.exp(a_cs_t[i:i + 1, CHUNK - 1:CHUNK]) + upd[rows]

        y = y + dskip_ref[:, gs] * xs
        y = y * _silu(proj_ref[:, pl.ds(pl.multiple_of(Z_COL + g * SSM_GROUP_WIDTH, SSM_GROUP_WIDTH),
                                       SSM_GROUP_WIDTH)].astype(f32))
        y = y * lax.rsqrt(jnp.mean(y * y, axis=-1, keepdims=True) + RMS_EPS) * ng_ref[:, gs]
        y_ref[:, gs] = y.astype(bf16)
        return carry

    lax.fori_loop(0, SSM_GROUPS, group, 0)
    if emit_state:
        sout_ref[...] = state[...]


def _ssd(proj, dtx, halo, s0, conv_w, conv_b, dt_bias_x, a_log_x, d_skip_ch, norm_g,
         *, batch, nchunk, vstart, emit_state):
    m = proj.shape[0]
    full = lambda shape: pl.BlockSpec(shape, lambda b, c: (0,) * len(shape))
    state_shape = (SSM_GROUPS, SSM_GROUP_WIDTH, SSM_STATE)
    in_specs = [
        pl.BlockSpec((CHUNK, MAIN_WIDTH), lambda b, c: (b * nchunk + c, 0)),
        pl.BlockSpec((CHUNK, LANES), lambda b, c: (b * nchunk + c, 0)),
        full((8, CONV_CH)),
        full(state_shape),
        full((CONV_WIDTH, CONV_CH)),
        full((1, CONV_CH)),
        full((1, DT_WIDTH)),
        full((1, DT_WIDTH)),
        full((1, SSM_WIDTH)),
        full((1, SSM_WIDTH)),
    ]
    out_specs = [pl.BlockSpec((CHUNK, SSM_WIDTH), lambda b, c: (b * nchunk + c, 0))]
    out_shape = [jax.ShapeDtypeStruct((m, SSM_WIDTH), bf16)]
    if emit_state:
        out_specs.append(full(state_shape))
        out_shape.append(jax.ShapeDtypeStruct(state_shape, f32))
    return pl.pallas_call(
        functools.partial(_ssd_body, vstart=vstart, emit_state=emit_state),
        grid=(batch, nchunk),
        in_specs=in_specs,
        out_specs=out_specs,
        out_shape=out_shape,
        scratch_shapes=[
            pltpu.VMEM(state_shape, f32),
            pltpu.VMEM((8, CONV_CH), f32),
            pltpu.VMEM((8 + CHUNK, SSM_GROUP_WIDTH), f32),
        ],
        compiler_params=pltpu.CompilerParams(
            dimension_semantics=("arbitrary", "arbitrary"), vmem_limit_bytes=VMEM_LIMIT),
        name="ssd_state" if emit_state else "ssd_scan",
    )(proj, dtx, halo, s0, conv_w, conv_b, dt_bias_x, a_log_x, d_skip_ch, norm_g)


def _outproj_body(att_ref, ssm_ref, w_ref, x_ref, gi_ref, bi_ref, g1_ref, b1_ref, h_ref, acc,
                  *, n_att_k):
    k = pl.program_id(1)

    @pl.when(k == 0)
    def _():
        acc[...] = jnp.zeros_like(acc)

    @pl.when(k < n_att_k)
    def _():
        acc[...] += jnp.dot(att_ref[...], w_ref[...], preferred_element_type=f32)

    @pl.when(k >= n_att_k)
    def _():
        acc[...] += jnp.dot(ssm_ref[...], w_ref[...], preferred_element_type=f32)

    @pl.when(k == pl.num_programs(1) - 1)
    def _():
        h0 = _layer_norm(x_ref[...], gi_ref[...], bi_ref[...])
        h_ref[...] = _layer_norm(DEEPNORM_ALPHA * h0 + acc[...], g1_ref[...], b1_ref[...])


def _outproj(att, ssm, w_out, x2d, gi, bi, g1, b1, *, tm, tk):
    m = att.shape[0]
    n_att_k = ATT_WIDTH // tk
    nk = (ATT_WIDTH + SSM_WIDTH) // tk
    return pl.pallas_call(
        functools.partial(_outproj_body, n_att_k=n_att_k),
        grid=(m // tm, nk),
        in_specs=[
            pl.BlockSpec((tm, tk), lambda i, k: (i, jnp.minimum(k, n_att_k - 1))),
            pl.BlockSpec((tm, tk), lambda i, k: (i, jnp.maximum(k - n_att_k, 0))),
            pl.BlockSpec((tk, D_MODEL), lambda i, k: (k, 0)),
            pl.BlockSpec((tm, D_MODEL), lambda i, k: (i, 0)),
            pl.BlockSpec((1, D_MODEL), lambda i, k: (0, 0)),
            pl.BlockSpec((1, D_MODEL), lambda i, k: (0, 0)),
            pl.BlockSpec((1, D_MODEL), lambda i, k: (0, 0)),
            pl.BlockSpec((1, D_MODEL), lambda i, k: (0, 0)),
        ],
        out_specs=pl.BlockSpec((tm, D_MODEL), lambda i, k: (i, 0)),
        out_shape=jax.ShapeDtypeStruct((m, D_MODEL), f32),
        scratch_shapes=[pltpu.VMEM((tm, D_MODEL), f32)],
        compiler_params=pltpu.CompilerParams(
            dimension_semantics=("arbitrary", "arbitrary"), vmem_limit_bytes=VMEM_LIMIT),
        name="outproj_ln1",
    )(att, ssm, w_out, x2d, gi, bi, g1, b1)


def _max01(v):
    return jnp.max(jnp.max(v, axis=1, keepdims=True), axis=0, keepdims=True)


def _router_body(h_ref, wr_ref, br_ref, tri_ref, eidx_ref, gate_ref, rank_ref, cnt_ref, running):
    tt = h_ref.shape[0]
    shape3 = (N_EXPERT_GROUPS, EXPERTS_PER_GROUP, tt)

    @pl.when(pl.program_id(0) == 0)
    def _():
        running[...] = jnp.zeros_like(running)

    logits = lax.dot_general(wr_ref[...], h_ref[...], (((1,), (1,)), ((), ())),
                             precision=lax.Precision.HIGHEST, preferred_element_type=f32)
    scores = (1.0 / (1.0 + jnp.exp(-logits)))
    sel3 = (scores + br_ref[...]).reshape(shape3)
    scores3 = scores.reshape(shape3)
    within = lax.broadcasted_iota(i32, shape3, 1).astype(f32)
    m1 = jnp.max(sel3, axis=1, keepdims=True)
    i1 = jnp.min(jnp.where(sel3 == m1, within, float(EXPERTS_PER_GROUP)), axis=1, keepdims=True)
    m2 = jnp.max(jnp.where(within == i1, -jnp.inf, sel3), axis=1, keepdims=True)
    gs = m1 + m2
    giota = lax.broadcasted_iota(i32, gs.shape, 0).astype(f32)
    gmask = jnp.zeros(gs.shape, f32)
    for _ in range(TOPK_GROUPS):
        gm = jnp.max(gs, axis=0, keepdims=True)
        gi = jnp.min(jnp.where(gs == gm, giota, float(N_EXPERT_GROUPS)), axis=0, keepdims=True)
        hit = giota == gi
        gmask = jnp.where(hit, 1.0, gmask)
        gs = jnp.where(hit, -jnp.inf, gs)
    selm = jnp.where(gmask > 0.0, sel3, -jnp.inf)
    eiota = (lax.broadcasted_iota(i32, shape3, 0) * EXPERTS_PER_GROUP
             + lax.broadcasted_iota(i32, shape3, 1)).astype(f32)
    eidx, gates, hits = [], [], []
    member = jnp.zeros(shape3, f32)
    gsum = jnp.zeros((1, 1, tt), f32)
    for _ in range(TOP_K):
        m = _max01(selm)
        ei = -_max01(-jnp.where(selm == m, eiota, float(N_EXPERTS)))
        hit = eiota == ei
        gk = jnp.sum(jnp.sum(jnp.where(hit, scores3, 0.0), axis=1, keepdims=True), axis=0, keepdims=True)
        eidx.append(ei)
        gates.append(gk)
        hits.append(hit)
        gsum = gsum + gk
        member = jnp.where(hit, 1.0, member)
        selm = jnp.where(hit, -jnp.inf, selm)
    member2 = member.reshape(N_EXPERTS, tt)
    incl = jnp.dot(member2.astype(bf16), tri_ref[...], preferred_element_type=f32)
    base = (running[...] + (incl - member2)).reshape(shape3)
    for k in range(TOP_K):
        rk = jnp.sum(jnp.sum(jnp.where(hits[k], base, 0.0), axis=1, keepdims=True), axis=0, keepdims=True)
        eidx_ref[k:k + 1, :] = eidx[k].reshape(1, tt).astype(i32)
        gate_ref[k:k + 1, :] = (gates[k] / gsum * ROUTED_SCALE).reshape(1, tt)
        rank_ref[k:k + 1, :] = rk.reshape(1, tt).astype(i32)
    total = running[...] + incl[:, tt - 1:tt]
    running[...] = total
    cnt_ref[...] = total.astype(i32)


def _router(h1, w_router_t, b_router_col, tri, *, tt):
    m = h1.shape[0]
    return pl.pallas_call(
        _router_body,
        grid=(m // tt,),
        in_specs=[
            pl.BlockSpec((tt, D_MODEL), lambda i: (i, 0)),
            pl.BlockSpec((N_EXPERTS, D_MODEL), lambda i: (0, 0)),
            pl.BlockSpec((N_EXPERTS, 1), lambda i: (0, 0)),
            pl.BlockSpec((tt, tt), lambda i: (0, 0)),
        ],
        out_specs=[
            pl.BlockSpec((TOP_K, tt), lambda i: (0, i)),
            pl.BlockSpec((TOP_K, tt), lambda i: (0, i)),
            pl.BlockSpec((TOP_K, tt), lambda i: (0, i)),
            pl.BlockSpec((N_EXPERTS, 1), lambda i: (0, 0)),
        ],
        out_shape=[
            jax.ShapeDtypeStruct((TOP_K, m), i32),
            jax.ShapeDtypeStruct((TOP_K, m), f32),
            jax.ShapeDtypeStruct((TOP_K, m), i32),
            jax.ShapeDtypeStruct((N_EXPERTS, 1), i32),
        ],
        scratch_shapes=[pltpu.VMEM((N_EXPERTS, 1), f32)],
        compiler_params=pltpu.CompilerParams(
            dimension_semantics=("arbitrary",), vmem_limit_bytes=VMEM_LIMIT),
        name="router",
    )(h1, w_router_t, b_router_col, tri)


FILL_SIZES = (128, 64, 32, 16, 8, 4, 2, 1)
PACKED_WIDTH = D_MODEL // 2
ROW_TILE = PACKED_WIDTH // LANES
HIGH_HALF = 0xFFFF0000
assert ROW_TILE == SUBLANES


def _row_copy(src_ref, src_row, dst_ref, dst_row, sem):
    src = src_ref.at[pl.ds(pl.multiple_of(src_row * ROW_TILE, ROW_TILE), ROW_TILE)]
    dst = dst_ref.at[pl.ds(pl.multiple_of(dst_row * ROW_TILE, ROW_TILE), ROW_TILE)]
    return pltpu.make_async_copy(src, dst, sem)


def _pack_bf16_pairs(v):
    w = v.shape[1] // 2
    bits = lambda t: lax.bitcast_convert_type(t.astype(bf16).astype(f32), jnp.uint32)
    return (bits(v[:, :w]) >> 16) | (bits(v[:, w:]) & jnp.uint32(HIGH_HALF))


def _unpack_bf16_pairs(p):
    lo = lax.bitcast_convert_type(p << 16, f32)
    hi = lax.bitcast_convert_type(p & jnp.uint32(HIGH_HALF), f32)
    return jnp.concatenate([lo, hi], axis=1)


def _store_tile_rows(ref, packed):
    n = packed.shape[0]
    for j in range(ROW_TILE):
        ref[pl.ds(j, n, stride=ROW_TILE), :] = packed[:, j * LANES:(j + 1) * LANES]


def _load_tile_rows(ref, n):
    return jnp.concatenate([ref[pl.ds(j, n, stride=ROW_TILE), :] for j in range(ROW_TILE)], axis=1)


def _dispatch_body(dest_ref, cnt_ref, pstart_ref, padded_ref, h_ref, xs_ref, packed, zbuf, sem, zsem, *, tt):
    i = pl.program_id(0)
    _store_tile_rows(packed, _pack_bf16_pairs(h_ref[...]))

    @pl.when(i == 0)
    def _():
        zbuf[...] = jnp.zeros_like(zbuf)

        def fill(e, wait):
            cnt = cnt_ref[e]
            first = pstart_ref[e] + cnt
            filler = padded_ref[e] - cnt
            for size in FILL_SIZES:
                @pl.when((filler & size) != 0)
                def _():
                    start = pl.multiple_of((first + (filler & (-2 * size))) * ROW_TILE, ROW_TILE)
                    cp = pltpu.make_async_copy(zbuf.at[pl.ds(0, size * ROW_TILE)],
                                               xs_ref.at[pl.ds(start, size * ROW_TILE)], zsem)
                    if wait:
                        cp.wait()
                    else:
                        cp.start()

        def start_fill(e, c):
            fill(e, False)
            return c

        def wait_fill(e, c):
            fill(e, True)
            return c

        lax.fori_loop(0, N_EXPERTS, start_fill, 0)
        lax.fori_loop(0, N_EXPERTS, wait_fill, 0)

    def issue(t, c):
        for k in range(TOP_K):
            _row_copy(packed, t, xs_ref, dest_ref[0, k, t], sem).start(priority=k % 2)
        return c

    def drain(t, c):
        for k in range(TOP_K):
            _row_copy(packed, 0, xs_ref, 0, sem).wait()
        return c

    lax.fori_loop(0, tt, issue, 0, unroll=8)
    lax.fori_loop(0, tt, drain, 0, unroll=8)


def _dispatch(dest3, counts, pstart, padded, h1, *, rows, tt):
    m = h1.shape[0]
    smem = pl.BlockSpec(memory_space=pltpu.SMEM)
    return pl.pallas_call(
        functools.partial(_dispatch_body, tt=tt),
        grid=(m // tt,),
        in_specs=[
            pl.BlockSpec((1, TOP_K, tt), lambda i: (i, 0, 0), memory_space=pltpu.SMEM),
            smem, smem, smem,
            pl.BlockSpec((tt, D_MODEL), lambda i: (i, 0)),
        ],
        out_specs=pl.BlockSpec(memory_space=pl.ANY),
        out_shape=jax.ShapeDtypeStruct((rows * ROW_TILE, LANES), jnp.uint32),
        scratch_shapes=[
            pltpu.VMEM((tt * ROW_TILE, LANES), jnp.uint32),
            pltpu.VMEM((FILL_SIZES[0] * ROW_TILE, LANES), jnp.uint32),
            pltpu.SemaphoreType.DMA(()),
            pltpu.SemaphoreType.DMA(()),
        ],
        compiler_params=pltpu.CompilerParams(
            dimension_semantics=("arbitrary",), vmem_limit_bytes=VMEM_LIMIT, has_side_effects=True),
        name="moe_dispatch",
    )(dest3, counts, pstart, padded, h1)


def _experts_body(blk_e_ref, nused_ref, first_ref, slot_ref, next_e_ref, x_ref, wg_hbm, wu_hbm, wd_hbm, y_ref,
                  rawg, rawu, rawd, wgb, wub, wdb, sem):
    i = pl.program_id(0)

    def weight_copies(e, s):
        return (pltpu.make_async_copy(wg_hbm.at[e], rawg.at[s], sem.at[s, 0]),
                pltpu.make_async_copy(wu_hbm.at[e], rawu.at[s], sem.at[s, 1]),
                pltpu.make_async_copy(wd_hbm.at[e], rawd.at[s], sem.at[s, 2]))

    @pl.when(i < nused_ref[0])
    def _():
        e = blk_e_ref[i]
        s = slot_ref[i]

        @pl.when(first_ref[i] == 1)
        def _():
            @pl.when(i == 0)
            def _():
                for cp in weight_copies(e, s):
                    cp.start()

            for cp in weight_copies(e, s):
                cp.wait()
            wgb[...] = rawg[s].astype(bf16)
            wub[...] = rawu[s].astype(bf16)
            wdb[...] = rawd[s].astype(bf16)

            @pl.when(next_e_ref[i] >= 0)
            def _():
                for cp in weight_copies(next_e_ref[i], 1 - s):
                    cp.start()

        x = _unpack_bf16_pairs(_load_tile_rows(x_ref, MOE_BLOCK)).astype(bf16)
        hg = jnp.dot(x, wgb[...], preferred_element_type=f32)
        hu = jnp.dot(x, wub[...], preferred_element_type=f32)
        hb = (_silu(hg) * hu).astype(bf16)
        _store_tile_rows(y_ref, _pack_bf16_pairs(jnp.dot(hb, wdb[...], preferred_element_type=f32)))


def _experts(blk_e, nused, first, slot, next_e, xs, w_gate, w_up, w_down):
    rows = xs.shape[0] // ROW_TILE
    nblk = rows // MOE_BLOCK
    blk = lambda i, be, nu, *_: (jnp.minimum(i, nu[0] - 1), 0)
    hbm = pl.BlockSpec(memory_space=pl.ANY)
    grid_spec = pltpu.PrefetchScalarGridSpec(
        num_scalar_prefetch=5,
        grid=(nblk,),
        in_specs=[pl.BlockSpec((MOE_BLOCK * ROW_TILE, LANES), blk), hbm, hbm, hbm],
        out_specs=pl.BlockSpec((MOE_BLOCK * ROW_TILE, LANES), blk),
        scratch_shapes=[
            pltpu.VMEM((2, D_MODEL, EXPERT_DIM), f32),
            pltpu.VMEM((2, D_MODEL, EXPERT_DIM), f32),
            pltpu.VMEM((2, EXPERT_DIM, D_MODEL), f32),
            pltpu.VMEM((D_MODEL, EXPERT_DIM), bf16),
            pltpu.VMEM((D_MODEL, EXPERT_DIM), bf16),
            pltpu.VMEM((EXPERT_DIM, D_MODEL), bf16),
            pltpu.SemaphoreType.DMA((2, 3)),
        ],
    )
    return pl.pallas_call(
        _experts_body,
        grid_spec=grid_spec,
        out_shape=jax.ShapeDtypeStruct((rows * ROW_TILE, LANES), jnp.uint32),
        compiler_params=pltpu.CompilerParams(
            dimension_semantics=("arbitrary",), vmem_limit_bytes=VMEM_LIMIT),
        name="routed_experts",
    )(blk_e, nused, first, slot, next_e, xs, w_gate, w_up, w_down)


def _combine_body(d0_ref, d1_ref, d2_ref, h_ref, gate_ref, wsg_ref, wsu_ref, wsd_ref, g2_ref, b2_ref, ys_ref,
                  o_ref, buf0, buf1, sem, *, tm):
    i = pl.program_id(0)

    def issue(d_ref, buf, s):
        for t in range(tm):
            for k in range(TOP_K):
                _row_copy(ys_ref, d_ref[0, k, t], buf.at[k], t, sem.at[s]).start(priority=k % 2)

    def drain(buf, s):
        def body(t, c):
            for k in range(TOP_K):
                _row_copy(ys_ref, 0, buf.at[k], 0, sem.at[s]).wait()
            return c
        lax.fori_loop(0, tm, body, 0, unroll=8)

    def tile(rows, buf):
        h = h_ref[rows]
        hb = h.astype(bf16)
        sg = jnp.dot(hb, wsg_ref[...], preferred_element_type=f32)
        su = jnp.dot(hb, wsu_ref[...], preferred_element_type=f32)
        ffn = jnp.dot((_silu(sg) * su).astype(bf16), wsd_ref[...], preferred_element_type=f32)
        gate = gate_ref[rows]
        for k in range(TOP_K):
            ffn = ffn + gate[:, k:k + 1] * _unpack_bf16_pairs(_load_tile_rows(buf.at[k], tm))
        o_ref[rows] = _layer_norm(DEEPNORM_ALPHA * h + ffn, g2_ref[...], b2_ref[...])

    @pl.when(i == 0)
    def _():
        issue(d0_ref, buf0, 0)

    drain(buf0, 0)
    issue(d1_ref, buf1, 1)
    tile(slice(0, tm), buf0)
    drain(buf1, 1)
    issue(d2_ref, buf0, 0)
    tile(slice(tm, 2 * tm), buf1)

    @pl.when(i == pl.num_programs(0) - 1)
    def _():
        drain(buf0, 0)


def _combine(dest3, h1, ys, gate_tok, wsg, wsu, wsd, g2, b2, *, tm):
    m = h1.shape[0]
    nt = m // tm
    dest_tile = lambda f: pl.BlockSpec((1, TOP_K, tm), lambda i: (f(i), 0, 0), memory_space=pltpu.SMEM)
    return pl.pallas_call(
        functools.partial(_combine_body, tm=tm),
        grid=(nt // 2,),
        in_specs=[
            dest_tile(lambda i: 2 * i),
            dest_tile(lambda i: 2 * i + 1),
            dest_tile(lambda i: jnp.minimum(2 * i + 2, nt - 1)),
            pl.BlockSpec((2 * tm, D_MODEL), lambda i: (i, 0)),
            pl.BlockSpec((2 * tm, TOP_K), lambda i: (i, 0)),
            pl.BlockSpec((D_MODEL, EXPERT_DIM), lambda i: (0, 0)),
            pl.BlockSpec((D_MODEL, EXPERT_DIM), lambda i: (0, 0)),
            pl.BlockSpec((EXPERT_DIM, D_MODEL), lambda i: (0, 0)),
            pl.BlockSpec((1, D_MODEL), lambda i: (0, 0)),
            pl.BlockSpec((1, D_MODEL), lambda i: (0, 0)),
            pl.BlockSpec(memory_space=pl.ANY),
        ],
        out_specs=pl.BlockSpec((2 * tm, D_MODEL), lambda i: (i, 0)),
        out_shape=jax.ShapeDtypeStruct((m, D_MODEL), f32),
        scratch_shapes=[
            pltpu.VMEM((TOP_K, tm * ROW_TILE, LANES), jnp.uint32),
            pltpu.VMEM((TOP_K, tm * ROW_TILE, LANES), jnp.uint32),
            pltpu.SemaphoreType.DMA((2,)),
        ],
        compiler_params=pltpu.CompilerParams(
            dimension_semantics=("arbitrary",), vmem_limit_bytes=VMEM_LIMIT),
        name="combine_ln2",
    )(dest3, dest3, dest3, h1, gate_tok, wsg, wsu, wsd, g2, b2, ys)


def _rope_tables(pos, width):
    half = ROPE_DIM // 2
    inv_freq = jnp.power(ROPE_THETA, -jnp.arange(0, ROPE_DIM, 2, dtype=f32) / ROPE_DIM)
    ang = pos.astype(f32)[:, None] * inv_freq[None, :]
    cos, sin = jnp.cos(ang), jnp.sin(ang)
    n = pos.shape[0]
    pad = jnp.zeros((n, HEAD_DIM - ROPE_DIM), f32)
    zero = jnp.zeros((n, half), f32)
    c = jnp.concatenate([cos, cos, pad + 1.0], axis=1)
    s1 = jnp.concatenate([-sin, zero, pad], axis=1)
    s2 = jnp.concatenate([zero, sin, pad], axis=1)
    tab = jnp.stack([c, s1, s2])
    return jnp.tile(tab, (1, 1, width // HEAD_DIM))


def _group_lanes(v):
    v = v.reshape(SSM_GROUPS, SSM_HEADS_PER_GROUP)
    return jnp.pad(v, ((0, 0), (0, LANES - SSM_HEADS_PER_GROUP))).reshape(1, DT_WIDTH)


def kernel(x, meta_tokens, ln_in_g, ln_in_b, w_in, conv_w, conv_b, dt_bias, a_log, d_skip, ssm_norm_g, att_norm_g, attn_sinks, w_out, ln1_g, ln1_b, w_router, b_router, w_gate, w_up, w_down, ws_gate, ws_up, ws_down, ln2_g, ln2_b):
    batch, seq, d = x.shape
    assert d == D_MODEL and seq % BLOCK == 0 and meta_tokens.shape == (N_META, D_MODEL)
    assert w_in.shape[0] == 1, "single layer"
    n_tok = batch * seq
    nblk = seq // BLOCK
    row = lambda v: v.reshape(1, -1).astype(f32)

    x2d = x.reshape(n_tok, D_MODEL)
    gi, bi = row(ln_in_g), row(ln_in_b)
    w_main = w_in[0].astype(bf16)
    w_dt = jnp.pad(w_in[0, :, MAIN_WIDTH:], ((0, 0), (0, LANES - SSM_HEADS))).astype(bf16)

    proj, dtx = _ln_inproj(x2d, gi, bi, w_main, w_dt, tm=1024, tn=512)
    proj_m, dtx_m = _ln_inproj(meta_tokens.astype(f32), gi, bi, w_main, w_dt, tm=N_META, tn=512)
    proj_m = jnp.pad(proj_m, ((BLOCK - N_META, 0), (0, 0)))
    dtx_m = jnp.pad(dtx_m, ((CHUNK - N_META, 0), (0, 0)))

    gw = ATT_GROUP * HEAD_DIM
    tab = _rope_tables(N_META + jnp.arange(seq), gw)
    tab_meta = _rope_tables(jnp.arange(BLOCK), gw)
    kv_meta = jnp.roll(proj_m[:, ATT_WIDTH:ATT_WIDTH + 2 * KV_WIDTH], N_META, axis=0)
    att = _attention(proj, kv_meta, tab, tab_meta, attn_sinks[0].astype(f32), row(att_norm_g[0]),
                     batch=batch, nblk=nblk)

    conv_w0 = conv_w[0].astype(f32)
    conv_b0 = row(conv_b[0])
    dtb_x = _group_lanes(dt_bias[0].astype(f32))
    alog_x = _group_lanes(a_log[0].astype(f32))
    dskip_ch = jnp.repeat(d_skip[0].astype(f32), HEAD_DIM).reshape(1, SSM_WIDTH)
    ng = row(ssm_norm_g[0])
    zeros_halo = jnp.zeros((8, CONV_CH), f32)
    zeros_state = jnp.zeros((SSM_GROUPS, SSM_GROUP_WIDTH, SSM_STATE), f32)
    _, s_meta = _ssd(proj_m, dtx_m, zeros_halo, zeros_state, conv_w0, conv_b0, dtb_x, alog_x, dskip_ch, ng,
                     batch=1, nchunk=1, vstart=CHUNK - N_META, emit_state=True)
    halo = proj_m[CHUNK - 8:, XBC_COL:].astype(f32)
    (ssm,) = _ssd(proj, dtx, halo, s_meta, conv_w0, conv_b0, dtb_x, alog_x, dskip_ch, ng,
                  batch=batch, nchunk=nblk, vstart=0, emit_state=False)

    h1 = _outproj(att, ssm, w_out[0].astype(bf16), x2d, gi, bi, row(ln1_g[0]), row(ln1_b[0]),
                  tm=512, tk=1024)

    tt = 512
    tri = (jnp.arange(tt)[:, None] <= jnp.arange(tt)[None, :]).astype(bf16)
    eidx, gate, rank, counts = _router(h1, w_router[0].T.astype(f32), b_router[0].reshape(N_EXPERTS, 1).astype(f32),
                                       tri, tt=tt)
    counts = counts.reshape(N_EXPERTS)
    padded = (counts + MOE_BLOCK - 1) // MOE_BLOCK * MOE_BLOCK
    pend = jnp.cumsum(padded)
    pstart = pend - padded
    first_row = jnp.sum(jnp.where(eidx[..., None] == jnp.arange(N_EXPERTS, dtype=i32), pstart.astype(i32), 0), axis=-1)
    dest = first_row + rank
    n_blocks = n_tok * TOP_K // MOE_BLOCK + N_EXPERTS
    blk_first = jnp.arange(n_blocks, dtype=i32) * MOE_BLOCK
    blk_e = jnp.minimum(jnp.sum(pend[None, :] <= blk_first[:, None], axis=1), N_EXPERTS - 1).astype(i32)
    nused = (pend[-1] // MOE_BLOCK).astype(i32).reshape(1)
    rows = n_blocks * MOE_BLOCK
    tiles = lambda t: dest.reshape(TOP_K, n_tok // t, t).transpose(1, 0, 2)

    td, tc = 256, 128
    xs = _dispatch(tiles(td), counts, pstart.astype(i32), padded.astype(i32), h1, rows=rows, tt=td)
    eids = jnp.arange(N_EXPERTS, dtype=i32)
    nonempty = counts > 0
    ordinal = jnp.cumsum(nonempty.astype(i32)) - nonempty.astype(i32)
    later = (eids[None, :] > eids[:, None]) & nonempty[None, :]
    next_nonempty = jnp.min(jnp.where(later, eids[None, :], N_EXPERTS), axis=1)
    next_nonempty = jnp.where(next_nonempty == N_EXPERTS, -1, next_nonempty).astype(i32)
    onehot_e = blk_e[:, None] == eids[None, :]
    pick = lambda table: jnp.sum(jnp.where(onehot_e, table[None, :], 0), axis=1).astype(i32)
    first = jnp.concatenate([jnp.ones((1,), i32), (blk_e[1:] != blk_e[:-1]).astype(i32)])
    ys = _experts(blk_e, nused, first, pick(ordinal) % 2, pick(next_nonempty), xs, w_gate[0], w_up[0], w_down[0])
    out = _combine(tiles(tc), h1, ys, gate.T, ws_gate[0].astype(bf16), ws_up[0].astype(bf16),
                   ws_down[0].astype(bf16), row(ln2_g[0]), row(ln2_b[0]), tm=tc)
    return out.reshape(batch, seq, D_MODEL)
```

```python
import functools
import math

import jax
import jax.numpy as jnp
from jax import lax
from jax.experimental import pallas as pl
from jax.experimental.pallas import tpu as pltpu

f32 = jnp.float32
bf16 = jnp.bfloat16
i32 = jnp.int32

D_MODEL = 2048
N_META = 16
HEAD_DIM = 64
ATT_HEADS = 32
ATT_KV_HEADS = 4
ATT_GROUP = ATT_HEADS // ATT_KV_HEADS
ATT_WIDTH = 2048
KV_WIDTH = 256
BLOCK = 128
ROPE_DIM = 16
ROPE_THETA = 500000.0
SSM_WIDTH = 4096
SSM_HEADS = 64
SSM_GROUPS = 8
SSM_HEADS_PER_GROUP = SSM_HEADS // SSM_GROUPS
SSM_STATE = 128
SSM_GROUP_WIDTH = SSM_WIDTH // SSM_GROUPS
CONV_WIDTH = 4
CHUNK = 128
CONV_CH = SSM_WIDTH + 2 * SSM_GROUPS * SSM_STATE
MAIN_WIDTH = ATT_WIDTH + 2 * KV_WIDTH + SSM_WIDTH + CONV_CH
Z_COL = ATT_WIDTH + 2 * KV_WIDTH
XBC_COL = Z_COL + SSM_WIDTH
N_EXPERTS = 64
EXPERT_DIM = 512
TOP_K = 8
N_EXPERT_GROUPS = 8
EXPERTS_PER_GROUP = N_EXPERTS // N_EXPERT_GROUPS
TOPK_GROUPS = 4
ROUTED_SCALE = 2.5
MOE_BLOCK = 256
DEEPNORM_ALPHA = 2.0 ** 0.25
LN_EPS = 1e-5
RMS_EPS = 1e-6
NEG_INF = -1e30
LANES = 128
SUBLANES = 8
DT_WIDTH = SSM_GROUPS * LANES
VMEM_LIMIT = 56 * 1024 * 1024


def _layer_norm(x, g, b):
    mu = jnp.mean(x, axis=-1, keepdims=True)
    xc = x - mu
    var = jnp.mean(xc * xc, axis=-1, keepdims=True)
    return xc * lax.rsqrt(var + LN_EPS) * g + b


def _silu(x):
    return x * (1.0 / (1.0 + jnp.exp(-x)))


def _softplus(x):
    return jnp.maximum(x, 0.0) + jnp.log1p(jnp.exp(-jnp.abs(x)))


def _cast_body(w_ref, o_ref):
    o_ref[...] = w_ref[...].astype(bf16)


def _cast_bf16(w, *, cols, bk, bn):
    k = w.shape[0]
    return pl.pallas_call(
        _cast_body,
        grid=(k // bk, cols // bn),
        in_specs=[pl.BlockSpec((bk, bn), lambda i, j: (i, j))],
        out_specs=pl.BlockSpec((bk, bn), lambda i, j: (i, j)),
        out_shape=jax.ShapeDtypeStruct((k, cols), bf16),
        compiler_params=pltpu.CompilerParams(
            dimension_semantics=("arbitrary", "arbitrary"), vmem_limit_bytes=VMEM_LIMIT),
        name="cast_bf16",
    )(w)


def _ln_inproj_body(x_ref, g_ref, b_ref, w_ref, wdt_ref, o_ref, dt_ref, h_scr):
    @pl.when(pl.program_id(1) == 0)
    def _():
        h = _layer_norm(x_ref[...], g_ref[...], b_ref[...]).astype(bf16)
        h_scr[...] = h
        dt_ref[...] = jnp.dot(h, wdt_ref[...], preferred_element_type=f32)

    o_ref[...] = jnp.dot(h_scr[...], w_ref[...], preferred_element_type=f32).astype(bf16)


def _ln_inproj(x2d, g, b, w_main, w_dt, *, tm, tn):
    m = x2d.shape[0]
    return pl.pallas_call(
        _ln_inproj_body,
        grid=(m // tm, MAIN_WIDTH // tn),
        in_specs=[
            pl.BlockSpec((tm, D_MODEL), lambda i, j: (i, 0)),
            pl.BlockSpec((1, D_MODEL), lambda i, j: (0, 0)),
            pl.BlockSpec((1, D_MODEL), lambda i, j: (0, 0)),
            pl.BlockSpec((D_MODEL, tn), lambda i, j: (0, j)),
            pl.BlockSpec((D_MODEL, LANES), lambda i, j: (0, 0)),
        ],
        out_specs=[
            pl.BlockSpec((tm, tn), lambda i, j: (i, j)),
            pl.BlockSpec((tm, LANES), lambda i, j: (i, 0)),
        ],
        out_shape=[
            jax.ShapeDtypeStruct((m, MAIN_WIDTH), bf16),
            jax.ShapeDtypeStruct((m, LANES), f32),
        ],
        scratch_shapes=[pltpu.VMEM((tm, D_MODEL), bf16)],
        compiler_params=pltpu.CompilerParams(
            dimension_semantics=("arbitrary", "arbitrary"), vmem_limit_bytes=VMEM_LIMIT),
        name="ln_inproj",
    )(x2d, g, b, w_main, w_dt)


def _rotate(t, tab):
    w = t.shape[-1]
    half = ROPE_DIM // 2
    return (t * tab[0]
            + pltpu.roll(t, w - half, 1) * tab[1]
            + pltpu.roll(t, half, 1) * tab[2])


def _attention_body(sink_ref, q_ref, kvc_ref, kvp_ref, kvm_ref, tabc_ref, tabp_ref, tabm_ref,
                    g_ref, o_ref):
    j = pl.program_id(1)
    tabc = tabc_ref[...]
    kc = _rotate(kvc_ref[:, :KV_WIDTH].astype(f32), tabc[:, :, :KV_WIDTH]).astype(bf16)
    kp = _rotate(kvp_ref[:, :KV_WIDTH].astype(f32), tabp_ref[:, :, :KV_WIDTH]).astype(bf16)
    km = _rotate(kvm_ref[:, :KV_WIDTH].astype(f32), tabm_ref[:, :, :KV_WIDTH]).astype(bf16)
    v_all = jnp.concatenate([kvp_ref[:, KV_WIDTH:], kvc_ref[:, KV_WIDTH:], kvm_ref[:, KV_WIDTH:]], axis=0)

    rows = ATT_GROUP * BLOCK
    r = lax.broadcasted_iota(i32, (rows, BLOCK), 0) % BLOCK
    c = lax.broadcasted_iota(i32, (rows, BLOCK), 1)
    lower = c <= r
    prev_ok = (c > r) & (j > 0)
    meta_ok = c < N_META
    head_of_row = lax.broadcasted_iota(i32, (rows, 1), 0) // BLOCK
    nt_dims = (((1,), (1,)), ((), ()))

    outs = []
    for g in range(ATT_KV_HEADS):
        gw = ATT_GROUP * HEAD_DIM
        hs = slice(g * HEAD_DIM, (g + 1) * HEAD_DIM)
        qg = _rotate(q_ref[:, g * gw:(g + 1) * gw].astype(f32), tabc) * (HEAD_DIM ** -0.5)
        qs = jnp.concatenate([qg[:, h * HEAD_DIM:(h + 1) * HEAD_DIM] for h in range(ATT_GROUP)],
                             axis=0).astype(bf16)
        s_cur = lax.dot_general(qs, kc[:, hs], nt_dims, preferred_element_type=f32)
        s_prev = lax.dot_general(qs, kp[:, hs], nt_dims, preferred_element_type=f32)
        s_meta = lax.dot_general(qs, km[:, hs], nt_dims, preferred_element_type=f32)
        s_band = jnp.where(lower, s_cur, jnp.where(prev_ok, s_prev, NEG_INF))
        s_meta = jnp.where(meta_ok, s_meta, NEG_INF)
        sink = jnp.zeros((rows, 1), f32)
        for h in range(ATT_GROUP):
            sink = jnp.where(head_of_row == h, sink_ref[g * ATT_GROUP + h], sink)
        m = jnp.maximum(jnp.max(jnp.maximum(s_band, s_meta), axis=1, keepdims=True), sink)
        p_band = jnp.exp(s_band - m)
        p_meta = jnp.exp(s_meta - m)
        denom = jnp.sum(p_band + p_meta, axis=1, keepdims=True) + jnp.exp(sink - m)
        p_cur = jnp.where(lower, p_band, 0.0)
        p = jnp.concatenate([p_band - p_cur, p_cur, p_meta], axis=1).astype(bf16)
        o = jnp.dot(p, v_all[:, hs], preferred_element_type=f32) / denom
        outs.append(jnp.concatenate([o[h * BLOCK:(h + 1) * BLOCK] for h in range(ATT_GROUP)], axis=1))
    att = jnp.concatenate(outs, axis=1)
    att = att * lax.rsqrt(jnp.mean(att * att, axis=-1, keepdims=True) + RMS_EPS) * g_ref[...]
    o_ref[...] = att.astype(bf16)


def _attention(proj, kv_meta, tab, tab_meta, sinks, att_g, *, batch, nblk):
    m = proj.shape[0]
    kvb = ATT_WIDTH // (2 * KV_WIDTH)
    gw = ATT_GROUP * HEAD_DIM
    return pl.pallas_call(
        _attention_body,
        grid=(batch, nblk),
        in_specs=[
            pl.BlockSpec(memory_space=pltpu.SMEM),
            pl.BlockSpec((BLOCK, ATT_WIDTH), lambda b, j: (b * nblk + j, 0)),
            pl.BlockSpec((BLOCK, 2 * KV_WIDTH), lambda b, j: (b * nblk + j, kvb)),
            pl.BlockSpec((BLOCK, 2 * KV_WIDTH), lambda b, j: (b * nblk + jnp.maximum(j - 1, 0), kvb)),
            pl.BlockSpec((BLOCK, 2 * KV_WIDTH), lambda b, j: (0, 0)),
            pl.BlockSpec((3, BLOCK, gw), lambda b, j: (0, j, 0)),
            pl.BlockSpec((3, BLOCK, gw), lambda b, j: (0, jnp.maximum(j - 1, 0), 0)),
            pl.BlockSpec((3, BLOCK, gw), lambda b, j: (0, 0, 0)),
            pl.BlockSpec((1, ATT_WIDTH), lambda b, j: (0, 0)),
        ],
        out_specs=pl.BlockSpec((BLOCK, ATT_WIDTH), lambda b, j: (b * nblk + j, 0)),
        out_shape=jax.ShapeDtypeStruct((m, ATT_WIDTH), bf16),
        compiler_params=pltpu.CompilerParams(
            dimension_semantics=("arbitrary", "arbitrary"), vmem_limit_bytes=VMEM_LIMIT),
        name="swa_attention",
    )(sinks, proj, proj, proj, kv_meta, tab, tab, tab_meta, att_g)


def _cumsum_rows(x):
    n = x.shape[0]
    row = lax.broadcasted_iota(i32, x.shape, 0)
    k = 1
    while k < n:
        x = x + jnp.where(row >= k, pltpu.roll(x, k, 0), 0.0)
        k *= 2
    return x


def _expand_heads(col):
    lane = lax.broadcasted_iota(i32, col.shape, 1) // HEAD_DIM
    per_tile = LANES // HEAD_DIM
    return jnp.concatenate(
        [jnp.take_along_axis(col, lane + per_tile * j, axis=1) for j in range(SSM_GROUP_WIDTH // LANES)], axis=1)


def _ssd_body(proj_ref, dt_ref, halo_ref, s0_ref, cw_ref, cb_ref, dtb_ref, alog_ref, dskip_ref, ng_ref,
              *rest, vstart, emit_state):
    if emit_state:
        y_ref, sout_ref, state, halo, work = rest
    else:
        y_ref, state, halo, work = rest

    @pl.when(pl.program_id(1) == 0)
    def _():
        state[...] = s0_ref[...]
        halo[...] = halo_ref[...]

    if vstart == 0:
        keep_valid = lambda v: v
    else:
        valid = lax.broadcasted_iota(i32, (CHUNK, 1), 0) >= vstart
        keep_valid = lambda v: jnp.where(valid, v, 0.0)
    lane = lax.broadcasted_iota(i32, (CHUNK, LANES), 1)
    causal =(lax.broadcasted_iota(i32, (CHUNK, CHUNK), 0) >= lax.broadcasted_iota(i32, (CHUNK, CHUNK), 1))

    def conv(col, width):
        u = proj_ref[:, pl.ds(pl.multiple_of(XBC_COL + col, LANES), width)].astype(f32)
        cs = pl.ds(pl.multiple_of(col, LANES), width)
        work[0:8, 0:width] = halo[:, cs]
        work[8:8 + CHUNK, 0:width] = u
        w = cw_ref[:, cs]
        acc = cb_ref[:, cs] + u * w[CONV_WIDTH - 1:CONV_WIDTH]
        for jj in range(CONV_WIDTH - 1):
            acc = acc + work[5 + jj:5 + jj + CHUNK, 0:width] * w[jj:jj + 1]
        halo[:, cs] = u[CHUNK - 8:]
        return keep_valid(_silu(acc))

    def group(g, carry):
        gs = pl.ds(pl.multiple_of(g * SSM_GROUP_WIDTH, SSM_GROUP_WIDTH), SSM_GROUP_WIDTH)
        gl = pl.ds(pl.multiple_of(g * LANES, LANES), LANES)
        xs = conv(g * SSM_GROUP_WIDTH, SSM_GROUP_WIDTH)
        bm = conv(SSM_WIDTH + g * SSM_STATE, SSM_STATE)
        cm = conv(SSM_WIDTH + SSM_GROUPS * SSM_STATE + g * SSM_STATE, SSM_STATE)

        dt_raw = jnp.take_along_axis(dt_ref[...], (lane + g * SSM_HEADS_PER_GROUP) % LANES, axis=1)
        dt = keep_valid(_softplus(dt_raw + dtb_ref[:, gl]))
        a_cs = _cumsum_rows(dt * -jnp.exp(alog_ref[:, gl]))
        a_cs_t = a_cs.T
        a_last = a_cs[CHUNK - 1:CHUNK]

        xdt = xs * _expand_heads(dt)
        bmb = bm.astype(bf16)
        cmb = cm.astype(bf16)
        cb = lax.dot_general(cmb, bmb, (((1,), (1,)), ((), ())), preferred_element_type=f32)
        xdtb = xdt.astype(bf16)
        ys = []
        for i in range(SSM_HEADS_PER_GROUP):
            seg = a_cs[:, i:i + 1] - a_cs_t[i:i + 1, :]
            lmat = (cb * jnp.exp(jnp.where(causal, seg, -jnp.inf))).astype(bf16)
            ys.append(jnp.dot(lmat, xdtb[:, i * HEAD_DIM:(i + 1) * HEAD_DIM], preferred_element_type=f32))
        y = jnp.concatenate(ys, axis=1)

        st = state[g]
        y_off = jnp.dot(cmb, st.astype(bf16), preferred_element_type=f32)
        y = y + y_off * _expand_heads(jnp.exp(a_cs))

        x_end = (xdt * _expand_heads(jnp.exp(a_last - a_cs))).astype(bf16)
        upd = lax.dot_general(bmb, x_end, (((0,), (0,)), ((), ())), preferred_element_type=f32)
        keep = _expand_heads(jnp.exp(a_cs[CHUNK - SUBLANES:]))[SUBLANES - 1:]
        state[g] = st * keep + upd

        y = y + dskip_ref[:, gs] * xs
        y = y * _silu(proj_ref[:, pl.ds(pl.multiple_of(Z_COL + g * SSM_GROUP_WIDTH, SSM_GROUP_WIDTH),
                                       SSM_GROUP_WIDTH)].astype(f32))
        y = y * lax.rsqrt(jnp.mean(y * y, axis=-1, keepdims=True) + RMS_EPS) * ng_ref[:, gs]
        y_ref[:, gs] = y.astype(bf16)
        return carry

    lax.fori_loop(0, SSM_GROUPS, group, 0)
    if emit_state:
        sout_ref[...] = state[...]


def _ssd(proj, dtx, halo, s0, conv_w, conv_b, dt_bias_x, a_log_x, d_skip_ch, norm_g,
         *, batch, nchunk, vstart, emit_state):
    m = proj.shape[0]
    full = lambda shape: pl.BlockSpec(shape, lambda b, c: (0,) * len(shape))
    state_shape = (SSM_GROUPS, SSM_STATE, SSM_GROUP_WIDTH)
    in_specs = [
        pl.BlockSpec((CHUNK, MAIN_WIDTH), lambda b, c: (b * nchunk + c, 0)),
        pl.BlockSpec((CHUNK, LANES), lambda b, c: (b * nchunk + c, 0)),
        full((8, CONV_CH)),
        full(state_shape),
        full((CONV_WIDTH, CONV_CH)),
        full((1, CONV_CH)),
        full((1, DT_WIDTH)),
        full((1, DT_WIDTH)),
        full((1, SSM_WIDTH)),
        full((1, SSM_WIDTH)),
    ]
    out_specs = [pl.BlockSpec((CHUNK, SSM_WIDTH), lambda b, c: (b * nchunk + c, 0))]
    out_shape = [jax.ShapeDtypeStruct((m, SSM_WIDTH), bf16)]
    if emit_state:
        out_specs.append(full(state_shape))
        out_shape.append(jax.ShapeDtypeStruct(state_shape, f32))
    return pl.pallas_call(
        functools.partial(_ssd_body, vstart=vstart, emit_state=emit_state),
        grid=(batch, nchunk),
        in_specs=in_specs,
        out_specs=out_specs,
        out_shape=out_shape,
        scratch_shapes=[
            pltpu.VMEM(state_shape, f32),
            pltpu.VMEM((8, CONV_CH), f32),
            pltpu.VMEM((8 + CHUNK, SSM_GROUP_WIDTH), f32),
        ],
        compiler_params=pltpu.CompilerParams(
            dimension_semantics=("arbitrary", "arbitrary"), vmem_limit_bytes=VMEM_LIMIT),
        name="ssd_state" if emit_state else "ssd_scan",
    )(proj, dtx, halo, s0, conv_w, conv_b, dt_bias_x, a_log_x, d_skip_ch, norm_g)


def _outproj_body(att_ref, ssm_ref, w_ref, x_ref, gi_ref, bi_ref, g1_ref, b1_ref, h_ref, acc,
                  *, n_att_k):
    k = pl.program_id(1)

    @pl.when(k == 0)
    def _():
        acc[...] = jnp.zeros_like(acc)

    @pl.when(k < n_att_k)
    def _():
        acc[...] += jnp.dot(att_ref[...], w_ref[...], preferred_element_type=f32)

    @pl.when(k >= n_att_k)
    def _():
        acc[...] += jnp.dot(ssm_ref[...], w_ref[...], preferred_element_type=f32)

    @pl.when(k == pl.num_programs(1) - 1)
    def _():
        h0 = _layer_norm(x_ref[...], gi_ref[...], bi_ref[...])
        h_ref[...] = _layer_norm(DEEPNORM_ALPHA * h0 + acc[...], g1_ref[...], b1_ref[...])


def _outproj(att, ssm, w_out, x2d, gi, bi, g1, b1, *, tm, tk):
    m = att.shape[0]
    n_att_k = ATT_WIDTH // tk
    nk = (ATT_WIDTH + SSM_WIDTH) // tk
    return pl.pallas_call(
        functools.partial(_outproj_body, n_att_k=n_att_k),
        grid=(m // tm, nk),
        in_specs=[
            pl.BlockSpec((tm, tk), lambda i, k: (i, jnp.minimum(k, n_att_k - 1))),
            pl.BlockSpec((tm, tk), lambda i, k: (i, jnp.maximum(k - n_att_k, 0))),
            pl.BlockSpec((tk, D_MODEL), lambda i, k: (k, 0)),
            pl.BlockSpec((tm, D_MODEL), lambda i, k: (i, 0)),
            pl.BlockSpec((1, D_MODEL), lambda i, k: (0, 0)),
            pl.BlockSpec((1, D_MODEL), lambda i, k: (0, 0)),
            pl.BlockSpec((1, D_MODEL), lambda i, k: (0, 0)),
            pl.BlockSpec((1, D_MODEL), lambda i, k: (0, 0)),
        ],
        out_specs=pl.BlockSpec((tm, D_MODEL), lambda i, k: (i, 0)),
        out_shape=jax.ShapeDtypeStruct((m, D_MODEL), f32),
        scratch_shapes=[pltpu.VMEM((tm, D_MODEL), f32)],
        compiler_params=pltpu.CompilerParams(
            dimension_semantics=("arbitrary", "arbitrary"), vmem_limit_bytes=VMEM_LIMIT),
        name="outproj_ln1",
    )(att, ssm, w_out, x2d, gi, bi, g1, b1)


def _max01(v):
    return jnp.max(jnp.max(v, axis=1, keepdims=True), axis=0, keepdims=True)


def _router_body(h_ref, wr_ref, br_ref, tri_ref, eidx_ref, gate_ref, rank_ref, cnt_ref, running):
    tt = h_ref.shape[0]
    shape3 = (N_EXPERT_GROUPS, EXPERTS_PER_GROUP, tt)

    @pl.when(pl.program_id(0) == 0)
    def _():
        running[...] = jnp.zeros_like(running)

    logits = lax.dot_general(wr_ref[...], h_ref[...], (((1,), (1,)), ((), ())),
                             precision=lax.Precision.HIGHEST, preferred_element_type=f32)
    scores = (1.0 / (1.0 + jnp.exp(-logits)))
    sel3 = (scores + br_ref[...]).reshape(shape3)
    scores3 = scores.reshape(shape3)
    within = lax.broadcasted_iota(i32, shape3, 1).astype(f32)
    m1 = jnp.max(sel3, axis=1, keepdims=True)
    i1 = jnp.min(jnp.where(sel3 == m1, within, float(EXPERTS_PER_GROUP)), axis=1, keepdims=True)
    m2 = jnp.max(jnp.where(within == i1, -jnp.inf, sel3), axis=1, keepdims=True)
    gs = m1 + m2
    giota = lax.broadcasted_iota(i32, gs.shape, 0).astype(f32)
    gmask = jnp.zeros(gs.shape, f32)
    for _ in range(TOPK_GROUPS):
        gm = jnp.max(gs, axis=0, keepdims=True)
        gi = jnp.min(jnp.where(gs == gm, giota, float(N_EXPERT_GROUPS)), axis=0, keepdims=True)
        hit = giota == gi
        gmask = jnp.where(hit, 1.0, gmask)
        gs = jnp.where(hit, -jnp.inf, gs)
    selm = jnp.where(gmask > 0.0, sel3, -jnp.inf)
    eiota = (lax.broadcasted_iota(i32, shape3, 0) * EXPERTS_PER_GROUP
             + lax.broadcasted_iota(i32, shape3, 1)).astype(f32)
    eidx, gates, hits = [], [], []
    member = jnp.zeros(shape3, f32)
    gsum = jnp.zeros((1, 1, tt), f32)
    for _ in range(TOP_K):
        m = _max01(selm)
        ei = -_max01(-jnp.where(selm == m, eiota, float(N_EXPERTS)))
        hit = eiota == ei
        gk = jnp.sum(jnp.sum(jnp.where(hit, scores3, 0.0), axis=1, keepdims=True), axis=0, keepdims=True)
        eidx.append(ei)
        gates.append(gk)
        hits.append(hit)
        gsum = gsum + gk
        member = jnp.where(hit, 1.0, member)
        selm = jnp.where(hit, -jnp.inf, selm)
    member2 = member.reshape(N_EXPERTS, tt)
    incl = jnp.dot(member2.astype(bf16), tri_ref[...], preferred_element_type=f32)
    base = (running[...] + (incl - member2)).reshape(shape3)
    for k in range(TOP_K):
        rk = jnp.sum(jnp.sum(jnp.where(hits[k], base, 0.0), axis=1, keepdims=True), axis=0, keepdims=True)
        eidx_ref[k:k + 1, :] = eidx[k].reshape(1, tt).astype(i32)
        gate_ref[k:k + 1, :] = (gates[k] / gsum * ROUTED_SCALE).reshape(1, tt)
        rank_ref[k:k + 1, :] = rk.reshape(1, tt).astype(i32)
    total = running[...] + incl[:, tt - 1:tt]
    running[...] = total
    cnt_ref[...] = total.astype(i32)


def _router(h1, w_router_t, b_router_col, tri, *, tt):
    m = h1.shape[0]
    return pl.pallas_call(
        _router_body,
        grid=(m // tt,),
        in_specs=[
            pl.BlockSpec((tt, D_MODEL), lambda i: (i, 0)),
            pl.BlockSpec((N_EXPERTS, D_MODEL), lambda i: (0, 0)),
            pl.BlockSpec((N_EXPERTS, 1), lambda i: (0, 0)),
            pl.BlockSpec((tt, tt), lambda i: (0, 0)),
        ],
        out_specs=[
            pl.BlockSpec((TOP_K, tt), lambda i: (0, i)),
            pl.BlockSpec((TOP_K, tt), lambda i: (0, i)),
            pl.BlockSpec((TOP_K, tt), lambda i: (0, i)),
            pl.BlockSpec((N_EXPERTS, 1), lambda i: (0, 0)),
        ],
        out_shape=[
            jax.ShapeDtypeStruct((TOP_K, m), i32),
            jax.ShapeDtypeStruct((TOP_K, m), f32),
            jax.ShapeDtypeStruct((TOP_K, m), i32),
            jax.ShapeDtypeStruct((N_EXPERTS, 1), i32),
        ],
        scratch_shapes=[pltpu.VMEM((N_EXPERTS, 1), f32)],
        compiler_params=pltpu.CompilerParams(
            dimension_semantics=("arbitrary",), vmem_limit_bytes=VMEM_LIMIT),
        name="router",
    )(h1, w_router_t, b_router_col, tri)


FILL_SIZES = (128, 64, 32, 16, 8, 4, 2, 1)
PACKED_WIDTH = D_MODEL // 2
ROW_TILE = PACKED_WIDTH // LANES
HIGH_HALF = 0xFFFF0000
assert ROW_TILE == SUBLANES


def _row_copy(src_ref, src_row, dst_ref, dst_row, sem):
    src = src_ref.at[pl.ds(pl.multiple_of(src_row * ROW_TILE, ROW_TILE), ROW_TILE)]
    dst = dst_ref.at[pl.ds(pl.multiple_of(dst_row * ROW_TILE, ROW_TILE), ROW_TILE)]
    return pltpu.make_async_copy(src, dst, sem)


def _pack_bf16_pairs(v):
    w = v.shape[1] // 2
    bits = lambda t: lax.bitcast_convert_type(t.astype(bf16).astype(f32), jnp.uint32)
    return (bits(v[:, :w]) >> 16) | (bits(v[:, w:]) & jnp.uint32(HIGH_HALF))


def _unpack_bf16_pairs(p):
    lo = lax.bitcast_convert_type(p << 16, f32)
    hi = lax.bitcast_convert_type(p & jnp.uint32(HIGH_HALF), f32)
    return jnp.concatenate([lo, hi], axis=1)


def _store_tile_rows(ref, packed):
    n = packed.shape[0]
    for j in range(ROW_TILE):
        ref[pl.ds(j, n, stride=ROW_TILE), :] = packed[:, j * LANES:(j + 1) * LANES]


def _load_tile_rows(ref, n):
    return jnp.concatenate([ref[pl.ds(j, n, stride=ROW_TILE), :] for j in range(ROW_TILE)], axis=1)


def _dispatch_body(dest_ref, cnt_ref, pstart_ref, padded_ref, h_ref, xs_ref, packed, zbuf, sem, zsem, *, tt):
    i = pl.program_id(0)
    _store_tile_rows(packed, _pack_bf16_pairs(h_ref[...]))

    @pl.when(i == 0)
    def _():
        zbuf[...] = jnp.zeros_like(zbuf)

        def fill(e, wait):
            cnt = cnt_ref[e]
            first = pstart_ref[e] + cnt
            filler = padded_ref[e] - cnt
            for size in FILL_SIZES:
                @pl.when((filler & size) != 0)
                def _():
                    start = pl.multiple_of((first + (filler & (-2 * size))) * ROW_TILE, ROW_TILE)
                    cp = pltpu.make_async_copy(zbuf.at[pl.ds(0, size * ROW_TILE)],
                                               xs_ref.at[pl.ds(start, size * ROW_TILE)], zsem)
                    if wait:
                        cp.wait()
                    else:
                        cp.start()

        def start_fill(e, c):
            fill(e, False)
            return c

        def wait_fill(e, c):
            fill(e, True)
            return c

        lax.fori_loop(0, N_EXPERTS, start_fill, 0)
        lax.fori_loop(0, N_EXPERTS, wait_fill, 0)

    def issue(t, c):
        for k in range(TOP_K):
            _row_copy(packed, t, xs_ref, dest_ref[0, k, t], sem).start(priority=k % 2)
        return c

    def drain(t, c):
        for k in range(TOP_K):
            _row_copy(packed, 0, xs_ref, 0, sem).wait()
        return c

    lax.fori_loop(0, tt, issue, 0, unroll=8)
    lax.fori_loop(0, tt, drain, 0, unroll=8)


def _dispatch(dest3, counts, pstart, padded, h1, *, rows, tt):
    m = h1.shape[0]
    smem = pl.BlockSpec(memory_space=pltpu.SMEM)
    return pl.pallas_call(
        functools.partial(_dispatch_body, tt=tt),
        grid=(m // tt,),
        in_specs=[
            pl.BlockSpec((1, TOP_K, tt), lambda i: (i, 0, 0), memory_space=pltpu.SMEM),
            smem, smem, smem,
            pl.BlockSpec((tt, D_MODEL), lambda i: (i, 0)),
        ],
        out_specs=pl.BlockSpec(memory_space=pl.ANY),
        out_shape=jax.ShapeDtypeStruct((rows * ROW_TILE, LANES), jnp.uint32),
        scratch_shapes=[
            pltpu.VMEM((tt * ROW_TILE, LANES), jnp.uint32),
            pltpu.VMEM((FILL_SIZES[0] * ROW_TILE, LANES), jnp.uint32),
            pltpu.SemaphoreType.DMA(()),
            pltpu.SemaphoreType.DMA(()),
        ],
        compiler_params=pltpu.CompilerParams(
            dimension_semantics=("arbitrary",), vmem_limit_bytes=VMEM_LIMIT, has_side_effects=True),
        name="moe_dispatch",
    )(dest3, counts, pstart, padded, h1)


def _experts_body(blk_e_ref, nused_ref, first_ref, slot_ref, next_e_ref, x_ref, wg_hbm, wu_hbm, wd_hbm, y_ref,
                  rawg, rawu, rawd, wgb, wub, wdb, sem):
    i = pl.program_id(0)

    def weight_copies(e, s):
        return (pltpu.make_async_copy(wg_hbm.at[e], rawg.at[s], sem.at[s, 0]),
                pltpu.make_async_copy(wu_hbm.at[e], rawu.at[s], sem.at[s, 1]),
                pltpu.make_async_copy(wd_hbm.at[e], rawd.at[s], sem.at[s, 2]))

    @pl.when(i < nused_ref[0])
    def _():
        e = blk_e_ref[i]
        s = slot_ref[i]

        @pl.when(first_ref[i] == 1)
        def _():
            @pl.when(i == 0)
            def _():
                for cp in weight_copies(e, s):
                    cp.start()

            for cp in weight_copies(e, s):
                cp.wait()
            wgb[...] = rawg[s].astype(bf16)
            wub[...] = rawu[s].astype(bf16)
            wdb[...] = rawd[s].astype(bf16)

            @pl.when(next_e_ref[i] >= 0)
            def _():
                for cp in weight_copies(next_e_ref[i], 1 - s):
                    cp.start()

        x = _unpack_bf16_pairs(_load_tile_rows(x_ref, MOE_BLOCK)).astype(bf16)
        hg = jnp.dot(x, wgb[...], preferred_element_type=f32)
        hu = jnp.dot(x, wub[...], preferred_element_type=f32)
        hb = (_silu(hg) * hu).astype(bf16)
        _store_tile_rows(y_ref, _pack_bf16_pairs(jnp.dot(hb, wdb[...], preferred_element_type=f32)))


def _experts(blk_e, nused, first, slot, next_e, xs, w_gate, w_up, w_down):
    rows = xs.shape[0] // ROW_TILE
    nblk = rows // MOE_BLOCK
    blk = lambda i, be, nu, *_: (jnp.minimum(i, nu[0] - 1), 0)
    hbm = pl.BlockSpec(memory_space=pl.ANY)
    grid_spec = pltpu.PrefetchScalarGridSpec(
        num_scalar_prefetch=5,
        grid=(nblk,),
        in_specs=[pl.BlockSpec((MOE_BLOCK * ROW_TILE, LANES), blk), hbm, hbm, hbm],
        out_specs=pl.BlockSpec((MOE_BLOCK * ROW_TILE, LANES), blk),
        scratch_shapes=[
            pltpu.VMEM((2, D_MODEL, EXPERT_DIM), f32),
            pltpu.VMEM((2, D_MODEL, EXPERT_DIM), f32),
            pltpu.VMEM((2, EXPERT_DIM, D_MODEL), f32),
            pltpu.VMEM((D_MODEL, EXPERT_DIM), bf16),
            pltpu.VMEM((D_MODEL, EXPERT_DIM), bf16),
            pltpu.VMEM((EXPERT_DIM, D_MODEL), bf16),
            pltpu.SemaphoreType.DMA((2, 3)),
        ],
    )
    return pl.pallas_call(
        _experts_body,
        grid_spec=grid_spec,
        out_shape=jax.ShapeDtypeStruct((rows * ROW_TILE, LANES), jnp.uint32),
        compiler_params=pltpu.CompilerParams(
            dimension_semantics=("arbitrary",), vmem_limit_bytes=VMEM_LIMIT),
        name="routed_experts",
    )(blk_e, nused, first, slot, next_e, xs, w_gate, w_up, w_down)


def _combine_body(d0_ref, d1_ref, d2_ref, h_ref, gate_ref, wsg_ref, wsu_ref, wsd_ref, g2_ref, b2_ref, ys_ref,
                  o_ref, buf0, buf1, sem, *, tm):
    i = pl.program_id(0)

    def issue(d_ref, buf, s):
        for t in range(tm):
            for k in range(TOP_K):
                _row_copy(ys_ref, d_ref[0, k, t], buf.at[k], t, sem.at[s]).start(priority=k % 2)

    def drain(buf, s):
        def body(t, c):
            for k in range(TOP_K):
                _row_copy(ys_ref, 0, buf.at[k], 0, sem.at[s]).wait()
            return c
        lax.fori_loop(0, tm, body, 0, unroll=8)

    def tile(rows, buf):
        h = h_ref[rows]
        hb = h.astype(bf16)
        sg = jnp.dot(hb, wsg_ref[...], preferred_element_type=f32)
        su = jnp.dot(hb, wsu_ref[...], preferred_element_type=f32)
        ffn = jnp.dot((_silu(sg) * su).astype(bf16), wsd_ref[...], preferred_element_type=f32)
        gate = gate_ref[rows]
        for k in range(TOP_K):
            ffn = ffn + gate[:, k:k + 1] * _unpack_bf16_pairs(_load_tile_rows(buf.at[k], tm))
        o_ref[rows] = _layer_norm(DEEPNORM_ALPHA * h + ffn, g2_ref[...], b2_ref[...])

    @pl.when(i == 0)
    def _():
        issue(d0_ref, buf0, 0)

    drain(buf0, 0)
    issue(d1_ref, buf1, 1)
    tile(slice(0, tm), buf0)
    drain(buf1, 1)
    issue(d2_ref, buf0, 0)
    tile(slice(tm, 2 * tm), buf1)

    @pl.when(i == pl.num_programs(0) - 1)
    def _():
        drain(buf0, 0)


def _combine(dest3, h1, ys, gate_tok, wsg, wsu, wsd, g2, b2, *, tm):
    m = h1.shape[0]
    nt = m // tm
    dest_tile = lambda f: pl.BlockSpec((1, TOP_K, tm), lambda i: (f(i), 0, 0), memory_space=pltpu.SMEM)
    return pl.pallas_call(
        functools.partial(_combine_body, tm=tm),
        grid=(nt // 2,),
        in_specs=[
            dest_tile(lambda i: 2 * i),
            dest_tile(lambda i: 2 * i + 1),
            dest_tile(lambda i: jnp.minimum(2 * i + 2, nt - 1)),
            pl.BlockSpec((2 * tm, D_MODEL), lambda i: (i, 0)),
            pl.BlockSpec((2 * tm, TOP_K), lambda i: (i, 0)),
            pl.BlockSpec((D_MODEL, EXPERT_DIM), lambda i: (0, 0)),
            pl.BlockSpec((D_MODEL, EXPERT_DIM), lambda i: (0, 0)),
            pl.BlockSpec((EXPERT_DIM, D_MODEL), lambda i: (0, 0)),
            pl.BlockSpec((1, D_MODEL), lambda i: (0, 0)),
            pl.BlockSpec((1, D_MODEL), lambda i: (0, 0)),
            pl.BlockSpec(memory_space=pl.ANY),
        ],
        out_specs=pl.BlockSpec((2 * tm, D_MODEL), lambda i: (i, 0)),
        out_shape=jax.ShapeDtypeStruct((m, D_MODEL), f32),
        scratch_shapes=[
            pltpu.VMEM((TOP_K, tm * ROW_TILE, LANES), jnp.uint32),
            pltpu.VMEM((TOP_K, tm * ROW_TILE, LANES), jnp.uint32),
            pltpu.SemaphoreType.DMA((2,)),
        ],
        compiler_params=pltpu.CompilerParams(
            dimension_semantics=("arbitrary",), vmem_limit_bytes=VMEM_LIMIT),
        name="combine_ln2",
    )(dest3, dest3, dest3, h1, gate_tok, wsg, wsu, wsd, g2, b2, ys)


def _rope_tables(pos, width):
    half = ROPE_DIM // 2
    inv_freq = jnp.power(ROPE_THETA, -jnp.arange(0, ROPE_DIM, 2, dtype=f32) / ROPE_DIM)
    ang = pos.astype(f32)[:, None] * inv_freq[None, :]
    cos, sin = jnp.cos(ang), jnp.sin(ang)
    n = pos.shape[0]
    pad = jnp.zeros((n, HEAD_DIM - ROPE_DIM), f32)
    zero = jnp.zeros((n, half), f32)
    c = jnp.concatenate([cos, cos, pad + 1.0], axis=1)
    s1 = jnp.concatenate([-sin, zero, pad], axis=1)
    s2 = jnp.concatenate([zero, sin, pad], axis=1)
    tab = jnp.stack([c, s1, s2])
    return jnp.tile(tab, (1, 1, width // HEAD_DIM))


def _group_lanes(v):
    v = v.reshape(SSM_GROUPS, SSM_HEADS_PER_GROUP)
    return jnp.pad(v, ((0, 0), (0, LANES - SSM_HEADS_PER_GROUP))).reshape(1, DT_WIDTH)


def kernel(x, meta_tokens, ln_in_g, ln_in_b, w_in, conv_w, conv_b, dt_bias, a_log, d_skip, ssm_norm_g, att_norm_g, attn_sinks, w_out, ln1_g, ln1_b, w_router, b_router, w_gate, w_up, w_down, ws_gate, ws_up, ws_down, ln2_g, ln2_b):
    batch, seq, d = x.shape
    assert d == D_MODEL and seq % BLOCK == 0 and meta_tokens.shape == (N_META, D_MODEL)
    assert w_in.shape[0] == 1, "single layer"
    n_tok = batch * seq
    nblk = seq // BLOCK
    row = lambda v: v.reshape(1, -1).astype(f32)

    x2d = x.reshape(n_tok, D_MODEL)
    gi, bi = row(ln_in_g), row(ln_in_b)
    w_main = _cast_bf16(w_in[0], cols=MAIN_WIDTH, bk=D_MODEL, bn=512)
    w_dt = jnp.pad(w_in[0, :, MAIN_WIDTH:], ((0, 0), (0, LANES - SSM_HEADS))).astype(bf16)

    proj, dtx = _ln_inproj(x2d, gi, bi, w_main, w_dt, tm=1024, tn=512)
    proj_m, dtx_m = _ln_inproj(meta_tokens.astype(f32), gi, bi, w_main, w_dt, tm=N_META, tn=512)
    proj_m = jnp.pad(proj_m, ((BLOCK - N_META, 0), (0, 0)))
    dtx_m = jnp.pad(dtx_m, ((CHUNK - N_META, 0), (0, 0)))

    gw = ATT_GROUP * HEAD_DIM
    tab = _rope_tables(N_META + jnp.arange(seq), gw)
    tab_meta = _rope_tables(jnp.arange(BLOCK), gw)
    kv_meta = jnp.roll(proj_m[:, ATT_WIDTH:ATT_WIDTH + 2 * KV_WIDTH], N_META, axis=0)
    att = _attention(proj, kv_meta, tab, tab_meta, attn_sinks[0].astype(f32), row(att_norm_g[0]),
                     batch=batch, nblk=nblk)

    conv_w0 = conv_w[0].astype(f32)
    conv_b0 = row(conv_b[0])
    dtb_x = _group_lanes(dt_bias[0].astype(f32))
    alog_x = _group_lanes(a_log[0].astype(f32))
    dskip_ch = jnp.repeat(d_skip[0].astype(f32), HEAD_DIM).reshape(1, SSM_WIDTH)
    ng = row(ssm_norm_g[0])
    zeros_halo = jnp.zeros((8, CONV_CH), f32)
    zeros_state = jnp.zeros((SSM_GROUPS, SSM_STATE, SSM_GROUP_WIDTH), f32)
    _, s_meta = _ssd(proj_m, dtx_m, zeros_halo, zeros_state, conv_w0, conv_b0, dtb_x, alog_x, dskip_ch, ng,
                     batch=1, nchunk=1, vstart=CHUNK - N_META, emit_state=True)
    halo = proj_m[CHUNK - 8:, XBC_COL:].astype(f32)
    (ssm,) = _ssd(proj, dtx, halo, s_meta, conv_w0, conv_b0, dtb_x, alog_x, dskip_ch, ng,
                  batch=batch, nchunk=nblk, vstart=0, emit_state=False)

    w_out_b = _cast_bf16(w_out[0], cols=D_MODEL, bk=1024, bn=D_MODEL)
    h1 = _outproj(att, ssm, w_out_b, x2d, gi, bi, row(ln1_g[0]), row(ln1_b[0]),
                  tm=512, tk=1024)

    tt = 512
    tri = (jnp.arange(tt)[:, None] <= jnp.arange(tt)[None, :]).astype(bf16)
    eidx, gate, rank, counts = _router(h1, w_router[0].T.astype(f32), b_router[0].reshape(N_EXPERTS, 1).astype(f32),
                                       tri, tt=tt)
    counts = counts.reshape(N_EXPERTS)
    padded = (counts + MOE_BLOCK - 1) // MOE_BLOCK * MOE_BLOCK
    pend = jnp.cumsum(padded)
    pstart = pend - padded
    first_row = jnp.sum(jnp.where(eidx[..., None] == jnp.arange(N_EXPERTS, dtype=i32), pstart.astype(i32), 0), axis=-1)
    dest = first_row + rank
    n_blocks = n_tok * TOP_K // MOE_BLOCK + N_EXPERTS
    blk_first = jnp.arange(n_blocks, dtype=i32) * MOE_BLOCK
    blk_e = jnp.minimum(jnp.sum(pend[None, :] <= blk_first[:, None], axis=1), N_EXPERTS - 1).astype(i32)
    nused = (pend[-1] // MOE_BLOCK).astype(i32).reshape(1)
    rows = n_blocks * MOE_BLOCK
    tiles = lambda t: dest.reshape(TOP_K, n_tok // t, t).transpose(1, 0, 2)

    td, tc = 256, 128
    xs = _dispatch(tiles(td), counts, pstart.astype(i32), padded.astype(i32), h1, rows=rows, tt=td)
    eids = jnp.arange(N_EXPERTS, dtype=i32)
    nonempty = counts > 0
    ordinal = jnp.cumsum(nonempty.astype(i32)) - nonempty.astype(i32)
    later = (eids[None, :] > eids[:, None]) & nonempty[None, :]
    next_nonempty = jnp.min(jnp.where(later, eids[None, :], N_EXPERTS), axis=1)
    next_nonempty = jnp.where(next_nonempty == N_EXPERTS, -1, next_nonempty).astype(i32)
    onehot_e = blk_e[:, None] == eids[None, :]
    pick = lambda table: jnp.sum(jnp.where(onehot_e, table[None, :], 0), axis=1).astype(i32)
    first = jnp.concatenate([jnp.ones((1,), i32), (blk_e[1:] != blk_e[:-1]).astype(i32)])
    ys = _experts(blk_e, nused, first, pick(ordinal) % 2, pick(next_nonempty), xs, w_gate[0], w_up[0], w_down[0])
    out = _combine(tiles(tc), h1, ys, gate.T, ws_gate[0].astype(bf16), ws_up[0].astype(bf16),
                   ws_down[0].astype(bf16), row(ln2_g[0]), row(ln2_b[0]), tm=tc)
    return out.reshape(batch, seq, D_MODEL)
```

```python
import functools
import math

import jax
import jax.numpy as jnp
from jax import lax
from jax.experimental import pallas as pl
from jax.experimental.pallas import tpu as pltpu

f32 = jnp.float32
bf16 = jnp.bfloat16
i32 = jnp.int32

D_MODEL = 2048
N_META = 16
HEAD_DIM = 64
ATT_HEADS = 32
ATT_KV_HEADS = 4
ATT_GROUP = ATT_HEADS // ATT_KV_HEADS
ATT_WIDTH = 2048
KV_WIDTH = 256
BLOCK = 128
ROPE_DIM = 16
ROPE_THETA = 500000.0
SSM_WIDTH = 4096
SSM_HEADS = 64
SSM_GROUPS = 8
SSM_HEADS_PER_GROUP = SSM_HEADS // SSM_GROUPS
SSM_STATE = 128
SSM_GROUP_WIDTH = SSM_WIDTH // SSM_GROUPS
CONV_WIDTH = 4
CHUNK = 128
CONV_CH = SSM_WIDTH + 2 * SSM_GROUPS * SSM_STATE
MAIN_WIDTH = ATT_WIDTH + 2 * KV_WIDTH + SSM_WIDTH + CONV_CH
Z_COL = ATT_WIDTH + 2 * KV_WIDTH
XBC_COL = Z_COL + SSM_WIDTH
N_EXPERTS = 64
EXPERT_DIM = 512
TOP_K = 8
N_EXPERT_GROUPS = 8
EXPERTS_PER_GROUP = N_EXPERTS // N_EXPERT_GROUPS
TOPK_GROUPS = 4
ROUTED_SCALE = 2.5
MOE_BLOCK = 256
DEEPNORM_ALPHA = 2.0 ** 0.25
LN_EPS = 1e-5
RMS_EPS = 1e-6
NEG_INF = -1e30
LANES = 128
SUBLANES = 8
DT_WIDTH = SSM_GROUPS * LANES
VMEM_LIMIT = 56 * 1024 * 1024


def _layer_norm(x, g, b):
    mu = jnp.mean(x, axis=-1, keepdims=True)
    xc = x - mu
    var = jnp.mean(xc * xc, axis=-1, keepdims=True)
    return xc * lax.rsqrt(var + LN_EPS) * g + b


def _silu(x):
    return x * (1.0 / (1.0 + jnp.exp(-x)))


def _softplus(x):
    return jnp.maximum(x, 0.0) + jnp.log1p(jnp.exp(-jnp.abs(x)))


def _cast_body(w_ref, o_ref):
    o_ref[...] = w_ref[...].astype(bf16)


def _cast_bf16(w, *, cols, bk, bn):
    k = w.shape[1]
    return pl.pallas_call(
        _cast_body,
        grid=(k // bk, cols // bn),
        in_specs=[pl.BlockSpec((None, bk, bn), lambda i, j: (0, i, j))],
        out_specs=pl.BlockSpec((bk, bn), lambda i, j: (i, j)),
        out_shape=jax.ShapeDtypeStruct((k, cols), bf16),
        compiler_params=pltpu.CompilerParams(
            dimension_semantics=("arbitrary", "arbitrary"), vmem_limit_bytes=VMEM_LIMIT),
        name="cast_bf16",
    )(w)


def _ln_inproj_body(x_ref, g_ref, b_ref, w_ref, wdt_ref, o_ref, dt_ref, h_scr):
    @pl.when(pl.program_id(1) == 0)
    def _():
        h = _layer_norm(x_ref[...], g_ref[...], b_ref[...]).astype(bf16)
        h_scr[...] = h
        dt_ref[...] = jnp.dot(h, wdt_ref[...], preferred_element_type=f32)

    o_ref[...] = jnp.dot(h_scr[...], w_ref[...], preferred_element_type=f32).astype(bf16)


def _ln_inproj(x2d, g, b, w_main, w_dt, *, tm, tn):
    m = x2d.shape[0]
    return pl.pallas_call(
        _ln_inproj_body,
        grid=(m // tm, MAIN_WIDTH // tn),
        in_specs=[
            pl.BlockSpec((tm, D_MODEL), lambda i, j: (i, 0)),
            pl.BlockSpec((1, D_MODEL), lambda i, j: (0, 0)),
            pl.BlockSpec((1, D_MODEL), lambda i, j: (0, 0)),
            pl.BlockSpec((D_MODEL, tn), lambda i, j: (0, j)),
            pl.BlockSpec((D_MODEL, LANES), lambda i, j: (0, 0)),
        ],
        out_specs=[
            pl.BlockSpec((tm, tn), lambda i, j: (i, j)),
            pl.BlockSpec((tm, LANES), lambda i, j: (i, 0)),
        ],
        out_shape=[
            jax.ShapeDtypeStruct((m, MAIN_WIDTH), bf16),
            jax.ShapeDtypeStruct((m, LANES), f32),
        ],
        scratch_shapes=[pltpu.VMEM((tm, D_MODEL), bf16)],
        compiler_params=pltpu.CompilerParams(
            dimension_semantics=("arbitrary", "arbitrary"), vmem_limit_bytes=VMEM_LIMIT),
        name="ln_inproj",
    )(x2d, g, b, w_main, w_dt)


def _rotate(t, tab):
    w = t.shape[-1]
    half = ROPE_DIM // 2
    return (t * tab[0]
            + pltpu.roll(t, w - half, 1) * tab[1]
            + pltpu.roll(t, half, 1) * tab[2])


def _attention_body(sink_ref, q_ref, kvc_ref, kvp_ref, kvm_ref, tabc_ref, tabp_ref, tabm_ref,
                    g_ref, o_ref):
    j = pl.program_id(1)
    tabc = tabc_ref[...]
    kc = _rotate(kvc_ref[:, :KV_WIDTH].astype(f32), tabc[:, :, :KV_WIDTH]).astype(bf16)
    kp = _rotate(kvp_ref[:, :KV_WIDTH].astype(f32), tabp_ref[:, :, :KV_WIDTH]).astype(bf16)
    km = _rotate(kvm_ref[:, :KV_WIDTH].astype(f32), tabm_ref[:, :, :KV_WIDTH]).astype(bf16)
    v_all = jnp.concatenate([kvp_ref[:, KV_WIDTH:], kvc_ref[:, KV_WIDTH:], kvm_ref[:, KV_WIDTH:]], axis=0)

    rows = ATT_GROUP * BLOCK
    r = lax.broadcasted_iota(i32, (rows, BLOCK), 0) % BLOCK
    c = lax.broadcasted_iota(i32, (rows, BLOCK), 1)
    lower = c <= r
    prev_ok = (c > r) & (j > 0)
    meta_ok = c < N_META
    head_of_row = lax.broadcasted_iota(i32, (rows, 1), 0) // BLOCK
    nt_dims = (((1,), (1,)), ((), ()))

    outs = []
    for g in range(ATT_KV_HEADS):
        gw = ATT_GROUP * HEAD_DIM
        hs = slice(g * HEAD_DIM, (g + 1) * HEAD_DIM)
        qg = _rotate(q_ref[:, g * gw:(g + 1) * gw].astype(f32), tabc) * (HEAD_DIM ** -0.5)
        qs = jnp.concatenate([qg[:, h * HEAD_DIM:(h + 1) * HEAD_DIM] for h in range(ATT_GROUP)],
                             axis=0).astype(bf16)
        s_cur = lax.dot_general(qs, kc[:, hs], nt_dims, preferred_element_type=f32)
        s_prev = lax.dot_general(qs, kp[:, hs], nt_dims, preferred_element_type=f32)
        s_meta = lax.dot_general(qs, km[:, hs], nt_dims, preferred_element_type=f32)
        s_band = jnp.where(lower, s_cur, jnp.where(prev_ok, s_prev, NEG_INF))
        s_meta = jnp.where(meta_ok, s_meta, NEG_INF)
        sink = jnp.zeros((rows, 1), f32)
        for h in range(ATT_GROUP):
            sink = jnp.where(head_of_row == h, sink_ref[g * ATT_GROUP + h], sink)
        m = jnp.maximum(jnp.max(jnp.maximum(s_band, s_meta), axis=1, keepdims=True), sink)
        p_band = jnp.exp(s_band - m)
        p_meta = jnp.exp(s_meta - m)
        denom = jnp.sum(p_band + p_meta, axis=1, keepdims=True) + jnp.exp(sink - m)
        p_cur = jnp.where(lower, p_band, 0.0)
        p = jnp.concatenate([p_band - p_cur, p_cur, p_meta], axis=1).astype(bf16)
        o = jnp.dot(p, v_all[:, hs], preferred_element_type=f32) / denom
        outs.append(jnp.concatenate([o[h * BLOCK:(h + 1) * BLOCK] for h in range(ATT_GROUP)], axis=1))
    att = jnp.concatenate(outs, axis=1)
    att = att * lax.rsqrt(jnp.mean(att * att, axis=-1, keepdims=True) + RMS_EPS) * g_ref[...]
    o_ref[...] = att.astype(bf16)


def _attention(proj, kv_meta, tab, tab_meta, sinks, att_g, *, batch, nblk):
    m = proj.shape[0]
    kvb = ATT_WIDTH // (2 * KV_WIDTH)
    gw = ATT_GROUP * HEAD_DIM
    return pl.pallas_call(
        _attention_body,
        grid=(batch, nblk),
        in_specs=[
            pl.BlockSpec(memory_space=pltpu.SMEM),
            pl.BlockSpec((BLOCK, ATT_WIDTH), lambda b, j: (b * nblk + j, 0)),
            pl.BlockSpec((BLOCK, 2 * KV_WIDTH), lambda b, j: (b * nblk + j, kvb)),
            pl.BlockSpec((BLOCK, 2 * KV_WIDTH), lambda b, j: (b * nblk + jnp.maximum(j - 1, 0), kvb)),
            pl.BlockSpec((BLOCK, 2 * KV_WIDTH), lambda b, j: (0, 0)),
            pl.BlockSpec((3, BLOCK, gw), lambda b, j: (0, j, 0)),
            pl.BlockSpec((3, BLOCK, gw), lambda b, j: (0, jnp.maximum(j - 1, 0), 0)),
            pl.BlockSpec((3, BLOCK, gw), lambda b, j: (0, 0, 0)),
            pl.BlockSpec((1, ATT_WIDTH), lambda b, j: (0, 0)),
        ],
        out_specs=pl.BlockSpec((BLOCK, ATT_WIDTH), lambda b, j: (b * nblk + j, 0)),
        out_shape=jax.ShapeDtypeStruct((m, ATT_WIDTH), bf16),
        compiler_params=pltpu.CompilerParams(
            dimension_semantics=("arbitrary", "arbitrary"), vmem_limit_bytes=VMEM_LIMIT),
        name="swa_attention",
    )(sinks, proj, proj, proj, kv_meta, tab, tab, tab_meta, att_g)


def _cumsum_rows(x):
    n = x.shape[0]
    row = lax.broadcasted_iota(i32, x.shape, 0)
    k = 1
    while k < n:
        x = x + jnp.where(row >= k, pltpu.roll(x, k, 0), 0.0)
        k *= 2
    return x


def _expand_heads(col):
    lane = lax.broadcasted_iota(i32, col.shape, 1) // HEAD_DIM
    per_tile = LANES // HEAD_DIM
    return jnp.concatenate(
        [jnp.take_along_axis(col, lane + per_tile * j, axis=1) for j in range(SSM_GROUP_WIDTH // LANES)], axis=1)


def _ssd_body(proj_ref, dt_ref, halo_ref, s0_ref, cw_ref, cb_ref, dtb_ref, alog_ref, dskip_ref, ng_ref,
              *rest, vstart, emit_state):
    if emit_state:
        y_ref, sout_ref, state, halo, work = rest
    else:
        y_ref, state, halo, work = rest

    @pl.when(pl.program_id(1) == 0)
    def _():
        state[...] = s0_ref[...]
        halo[...] = halo_ref[...]

    if vstart == 0:
        keep_valid = lambda v: v
    else:
        valid = lax.broadcasted_iota(i32, (CHUNK, 1), 0) >= vstart
        keep_valid = lambda v: jnp.where(valid, v, 0.0)
    lane = lax.broadcasted_iota(i32, (CHUNK, LANES), 1)
    causal =(lax.broadcasted_iota(i32, (CHUNK, CHUNK), 0) >= lax.broadcasted_iota(i32, (CHUNK, CHUNK), 1))

    def conv(col, width):
        u = proj_ref[:, pl.ds(pl.multiple_of(XBC_COL + col, LANES), width)].astype(f32)
        cs = pl.ds(pl.multiple_of(col, LANES), width)
        work[0:8, 0:width] = halo[:, cs]
        work[8:8 + CHUNK, 0:width] = u
        w = cw_ref[:, cs]
        acc = cb_ref[:, cs] + u * w[CONV_WIDTH - 1:CONV_WIDTH]
        for jj in range(CONV_WIDTH - 1):
            acc = acc + work[5 + jj:5 + jj + CHUNK, 0:width] * w[jj:jj + 1]
        halo[:, cs] = u[CHUNK - 8:]
        return keep_valid(_silu(acc))

    def group(g, carry):
        gs = pl.ds(pl.multiple_of(g * SSM_GROUP_WIDTH, SSM_GROUP_WIDTH), SSM_GROUP_WIDTH)
        gl = pl.ds(pl.multiple_of(g * LANES, LANES), LANES)
        xs = conv(g * SSM_GROUP_WIDTH, SSM_GROUP_WIDTH)
        bm = conv(SSM_WIDTH + g * SSM_STATE, SSM_STATE)
        cm = conv(SSM_WIDTH + SSM_GROUPS * SSM_STATE + g * SSM_STATE, SSM_STATE)

        dt_raw = jnp.take_along_axis(dt_ref[...], (lane + g * SSM_HEADS_PER_GROUP) % LANES, axis=1)
        dt = keep_valid(_softplus(dt_raw + dtb_ref[:, gl]))
        a_cs = _cumsum_rows(dt * -jnp.exp(alog_ref[:, gl]))
        a_cs_t = a_cs.T
        a_last = a_cs[CHUNK - 1:CHUNK]

        xdt = xs * _expand_heads(dt)
        bmb = bm.astype(bf16)
        cmb = cm.astype(bf16)
        cb = lax.dot_general(cmb, bmb, (((1,), (1,)), ((), ())), preferred_element_type=f32)
        xdtb = xdt.astype(bf16)
        ys = []
        for i in range(SSM_HEADS_PER_GROUP):
            seg = a_cs[:, i:i + 1] - a_cs_t[i:i + 1, :]
            lmat = (cb * jnp.exp(jnp.where(causal, seg, -jnp.inf))).astype(bf16)
            ys.append(jnp.dot(lmat, xdtb[:, i * HEAD_DIM:(i + 1) * HEAD_DIM], preferred_element_type=f32))
        y = jnp.concatenate(ys, axis=1)

        st = state[g]
        y_off = jnp.dot(cmb, st.astype(bf16), preferred_element_type=f32)
        y = y + y_off * _expand_heads(jnp.exp(a_cs))

        x_end = (xdt * _expand_heads(jnp.exp(a_last - a_cs))).astype(bf16)
        upd = lax.dot_general(bmb, x_end, (((0,), (0,)), ((), ())), preferred_element_type=f32)
        keep = _expand_heads(jnp.exp(a_cs[CHUNK - SUBLANES:]))[SUBLANES - 1:]
        state[g] = st * keep + upd

        y = y + dskip_ref[:, gs] * xs
        y = y * _silu(proj_ref[:, pl.ds(pl.multiple_of(Z_COL + g * SSM_GROUP_WIDTH, SSM_GROUP_WIDTH),
                                       SSM_GROUP_WIDTH)].astype(f32))
        y = y * lax.rsqrt(jnp.mean(y * y, axis=-1, keepdims=True) + RMS_EPS) * ng_ref[:, gs]
        y_ref[:, gs] = y.astype(bf16)
        return carry

    lax.fori_loop(0, SSM_GROUPS, group, 0)
    if emit_state:
        sout_ref[...] = state[...]


def _ssd(proj, dtx, halo, s0, conv_w, conv_b, dt_bias_x, a_log_x, d_skip_ch, norm_g,
         *, batch, nchunk, vstart, emit_state):
    m = proj.shape[0]
    full = lambda shape: pl.BlockSpec(shape, lambda b, c: (0,) * len(shape))
    state_shape = (SSM_GROUPS, SSM_STATE, SSM_GROUP_WIDTH)
    in_specs = [
        pl.BlockSpec((CHUNK, MAIN_WIDTH), lambda b, c: (b * nchunk + c, 0)),
        pl.BlockSpec((CHUNK, LANES), lambda b, c: (b * nchunk + c, 0)),
        full((8, CONV_CH)),
        full(state_shape),
        full((CONV_WIDTH, CONV_CH)),
        full((1, CONV_CH)),
        full((1, DT_WIDTH)),
        full((1, DT_WIDTH)),
        full((1, SSM_WIDTH)),
        full((1, SSM_WIDTH)),
    ]
    out_specs = [pl.BlockSpec((CHUNK, SSM_WIDTH), lambda b, c: (b * nchunk + c, 0))]
    out_shape = [jax.ShapeDtypeStruct((m, SSM_WIDTH), bf16)]
    if emit_state:
        out_specs.append(full(state_shape))
        out_shape.append(jax.ShapeDtypeStruct(state_shape, f32))
    return pl.pallas_call(
        functools.partial(_ssd_body, vstart=vstart, emit_state=emit_state),
        grid=(batch, nchunk),
        in_specs=in_specs,
        out_specs=out_specs,
        out_shape=out_shape,
        scratch_shapes=[
            pltpu.VMEM(state_shape, f32),
            pltpu.VMEM((8, CONV_CH), f32),
            pltpu.VMEM((8 + CHUNK, SSM_GROUP_WIDTH), f32),
        ],
        compiler_params=pltpu.CompilerParams(
            dimension_semantics=("arbitrary", "arbitrary"), vmem_limit_bytes=VMEM_LIMIT),
        name="ssd_state" if emit_state else "ssd_scan",
    )(proj, dtx, halo, s0, conv_w, conv_b, dt_bias_x, a_log_x, d_skip_ch, norm_g)


def _outproj_body(att_ref, ssm_ref, w_ref, x_ref, gi_ref, bi_ref, g1_ref, b1_ref, h_ref, acc,
                  *, n_att_k):
    k = pl.program_id(1)

    @pl.when(k == 0)
    def _():
        acc[...] = jnp.zeros_like(acc)

    @pl.when(k < n_att_k)
    def _():
        acc[...] += jnp.dot(att_ref[...], w_ref[...], preferred_element_type=f32)

    @pl.when(k >= n_att_k)
    def _():
        acc[...] += jnp.dot(ssm_ref[...], w_ref[...], preferred_element_type=f32)

    @pl.when(k == pl.num_programs(1) - 1)
    def _():
        h0 = _layer_norm(x_ref[...], gi_ref[...], bi_ref[...])
        h_ref[...] = _layer_norm(DEEPNORM_ALPHA * h0 + acc[...], g1_ref[...], b1_ref[...])


def _outproj(att, ssm, w_out, x2d, gi, bi, g1, b1, *, tm, tk):
    m = att.shape[0]
    n_att_k = ATT_WIDTH // tk
    nk = (ATT_WIDTH + SSM_WIDTH) // tk
    return pl.pallas_call(
        functools.partial(_outproj_body, n_att_k=n_att_k),
        grid=(m // tm, nk),
        in_specs=[
            pl.BlockSpec((tm, tk), lambda i, k: (i, jnp.minimum(k, n_att_k - 1))),
            pl.BlockSpec((tm, tk), lambda i, k: (i, jnp.maximum(k - n_att_k, 0))),
            pl.BlockSpec((tk, D_MODEL), lambda i, k: (k, 0)),
            pl.BlockSpec((tm, D_MODEL), lambda i, k: (i, 0)),
            pl.BlockSpec((1, D_MODEL), lambda i, k: (0, 0)),
            pl.BlockSpec((1, D_MODEL), lambda i, k: (0, 0)),
            pl.BlockSpec((1, D_MODEL), lambda i, k: (0, 0)),
            pl.BlockSpec((1, D_MODEL), lambda i, k: (0, 0)),
        ],
        out_specs=pl.BlockSpec((tm, D_MODEL), lambda i, k: (i, 0)),
        out_shape=jax.ShapeDtypeStruct((m, D_MODEL), f32),
        scratch_shapes=[pltpu.VMEM((tm, D_MODEL), f32)],
        compiler_params=pltpu.CompilerParams(
            dimension_semantics=("arbitrary", "arbitrary"), vmem_limit_bytes=VMEM_LIMIT),
        name="outproj_ln1",
    )(att, ssm, w_out, x2d, gi, bi, g1, b1)


def _max01(v):
    return jnp.max(jnp.max(v, axis=1, keepdims=True), axis=0, keepdims=True)


def _router_body(h_ref, wr_ref, br_ref, tri_ref, eidx_ref, gate_ref, rank_ref, cnt_ref, running):
    tt = h_ref.shape[0]
    shape3 = (N_EXPERT_GROUPS, EXPERTS_PER_GROUP, tt)

    @pl.when(pl.program_id(0) == 0)
    def _():
        running[...] = jnp.zeros_like(running)

    logits = lax.dot_general(wr_ref[...], h_ref[...], (((1,), (1,)), ((), ())),
                             precision=lax.Precision.HIGHEST, preferred_element_type=f32)
    scores = (1.0 / (1.0 + jnp.exp(-logits)))
    sel3 = (scores + br_ref[...]).reshape(shape3)
    scores3 = scores.reshape(shape3)
    within = lax.broadcasted_iota(i32, shape3, 1).astype(f32)
    m1 = jnp.max(sel3, axis=1, keepdims=True)
    i1 = jnp.min(jnp.where(sel3 == m1, within, float(EXPERTS_PER_GROUP)), axis=1, keepdims=True)
    m2 = jnp.max(jnp.where(within == i1, -jnp.inf, sel3), axis=1, keepdims=True)
    gs = m1 + m2
    giota = lax.broadcasted_iota(i32, gs.shape, 0).astype(f32)
    gmask = jnp.zeros(gs.shape, f32)
    for _ in range(TOPK_GROUPS):
        gm = jnp.max(gs, axis=0, keepdims=True)
        gi = jnp.min(jnp.where(gs == gm, giota, float(N_EXPERT_GROUPS)), axis=0, keepdims=True)
        hit = giota == gi
        gmask = jnp.where(hit, 1.0, gmask)
        gs = jnp.where(hit, -jnp.inf, gs)
    selm = jnp.where(gmask > 0.0, sel3, -jnp.inf)
    eiota = (lax.broadcasted_iota(i32, shape3, 0) * EXPERTS_PER_GROUP
             + lax.broadcasted_iota(i32, shape3, 1)).astype(f32)
    eidx, gates, hits = [], [], []
    member = jnp.zeros(shape3, f32)
    gsum = jnp.zeros((1, 1, tt), f32)
    for _ in range(TOP_K):
        m = _max01(selm)
        ei = -_max01(-jnp.where(selm == m, eiota, float(N_EXPERTS)))
        hit = eiota == ei
        gk = jnp.sum(jnp.sum(jnp.where(hit, scores3, 0.0), axis=1, keepdims=True), axis=0, keepdims=True)
        eidx.append(ei)
        gates.append(gk)
        hits.append(hit)
        gsum = gsum + gk
        member = jnp.where(hit, 1.0, member)
        selm = jnp.where(hit, -jnp.inf, selm)
    member2 = member.reshape(N_EXPERTS, tt)
    incl = jnp.dot(member2.astype(bf16), tri_ref[...], preferred_element_type=f32)
    base = (running[...] + (incl - member2)).reshape(shape3)
    for k in range(TOP_K):
        rk = jnp.sum(jnp.sum(jnp.where(hits[k], base, 0.0), axis=1, keepdims=True), axis=0, keepdims=True)
        eidx_ref[k:k + 1, :] = eidx[k].reshape(1, tt).astype(i32)
        gate_ref[k:k + 1, :] = (gates[k] / gsum * ROUTED_SCALE).reshape(1, tt)
        rank_ref[k:k + 1, :] = rk.reshape(1, tt).astype(i32)
    total = running[...] + incl[:, tt - 1:tt]
    running[...] = total
    cnt_ref[...] = total.astype(i32)


def _router(h1, w_router_t, b_router_col, tri, *, tt):
    m = h1.shape[0]
    return pl.pallas_call(
        _router_body,
        grid=(m // tt,),
        in_specs=[
            pl.BlockSpec((tt, D_MODEL), lambda i: (i, 0)),
            pl.BlockSpec((N_EXPERTS, D_MODEL), lambda i: (0, 0)),
            pl.BlockSpec((N_EXPERTS, 1), lambda i: (0, 0)),
            pl.BlockSpec((tt, tt), lambda i: (0, 0)),
        ],
        out_specs=[
            pl.BlockSpec((TOP_K, tt), lambda i: (0, i)),
            pl.BlockSpec((TOP_K, tt), lambda i: (0, i)),
            pl.BlockSpec((TOP_K, tt), lambda i: (0, i)),
            pl.BlockSpec((N_EXPERTS, 1), lambda i: (0, 0)),
        ],
        out_shape=[
            jax.ShapeDtypeStruct((TOP_K, m), i32),
            jax.ShapeDtypeStruct((TOP_K, m), f32),
            jax.ShapeDtypeStruct((TOP_K, m), i32),
            jax.ShapeDtypeStruct((N_EXPERTS, 1), i32),
        ],
        scratch_shapes=[pltpu.VMEM((N_EXPERTS, 1), f32)],
        compiler_params=pltpu.CompilerParams(
            dimension_semantics=("arbitrary",), vmem_limit_bytes=VMEM_LIMIT),
        name="router",
    )(h1, w_router_t, b_router_col, tri)


FILL_SIZES = (128, 64, 32, 16, 8, 4, 2, 1)
PACKED_WIDTH = D_MODEL // 2
ROW_TILE = PACKED_WIDTH // LANES
HIGH_HALF = 0xFFFF0000
assert ROW_TILE == SUBLANES


def _row_copy(src_ref, src_row, dst_ref, dst_row, sem):
    src = src_ref.at[pl.ds(pl.multiple_of(src_row * ROW_TILE, ROW_TILE), ROW_TILE)]
    dst = dst_ref.at[pl.ds(pl.multiple_of(dst_row * ROW_TILE, ROW_TILE), ROW_TILE)]
    return pltpu.make_async_copy(src, dst, sem)


def _pack_bf16_pairs(v):
    w = v.shape[1] // 2
    bits = lambda t: lax.bitcast_convert_type(t.astype(bf16).astype(f32), jnp.uint32)
    return (bits(v[:, :w]) >> 16) | (bits(v[:, w:]) & jnp.uint32(HIGH_HALF))


def _unpack_bf16_pairs(p):
    lo = lax.bitcast_convert_type(p << 16, f32)
    hi = lax.bitcast_convert_type(p & jnp.uint32(HIGH_HALF), f32)
    return jnp.concatenate([lo, hi], axis=1)


def _store_tile_rows(ref, packed):
    n = packed.shape[0]
    for j in range(ROW_TILE):
        ref[pl.ds(j, n, stride=ROW_TILE), :] = packed[:, j * LANES:(j + 1) * LANES]


def _load_tile_rows(ref, n):
    return jnp.concatenate([ref[pl.ds(j, n, stride=ROW_TILE), :] for j in range(ROW_TILE)], axis=1)


def _dispatch_body(dest_ref, cnt_ref, pstart_ref, padded_ref, h_ref, xs_ref, packed, zbuf, sem, zsem, *, tt):
    i = pl.program_id(0)
    _store_tile_rows(packed, _pack_bf16_pairs(h_ref[...]))

    @pl.when(i == 0)
    def _():
        zbuf[...] = jnp.zeros_like(zbuf)

        def fill(e, wait):
            cnt = cnt_ref[e]
            first = pstart_ref[e] + cnt
            filler = padded_ref[e] - cnt
            for size in FILL_SIZES:
                @pl.when((filler & size) != 0)
                def _():
                    start = pl.multiple_of((first + (filler & (-2 * size))) * ROW_TILE, ROW_TILE)
                    cp = pltpu.make_async_copy(zbuf.at[pl.ds(0, size * ROW_TILE)],
                                               xs_ref.at[pl.ds(start, size * ROW_TILE)], zsem)
                    if wait:
                        cp.wait()
                    else:
                        cp.start()

        def start_fill(e, c):
            fill(e, False)
            return c

        def wait_fill(e, c):
            fill(e, True)
            return c

        lax.fori_loop(0, N_EXPERTS, start_fill, 0)
        lax.fori_loop(0, N_EXPERTS, wait_fill, 0)

    def issue(t, c):
        for k in range(TOP_K):
            _row_copy(packed, t, xs_ref, dest_ref[0, k, t], sem).start(priority=k % 2)
        return c

    def drain(t, c):
        for k in range(TOP_K):
            _row_copy(packed, 0, xs_ref, 0, sem).wait()
        return c

    lax.fori_loop(0, tt, issue, 0, unroll=8)
    lax.fori_loop(0, tt, drain, 0, unroll=8)


def _dispatch(dest3, counts, pstart, padded, h1, *, rows, tt):
    m = h1.shape[0]
    smem = pl.BlockSpec(memory_space=pltpu.SMEM)
    return pl.pallas_call(
        functools.partial(_dispatch_body, tt=tt),
        grid=(m // tt,),
        in_specs=[
            pl.BlockSpec((1, TOP_K, tt), lambda i: (i, 0, 0), memory_space=pltpu.SMEM),
            smem, smem, smem,
            pl.BlockSpec((tt, D_MODEL), lambda i: (i, 0)),
        ],
        out_specs=pl.BlockSpec(memory_space=pl.ANY),
        out_shape=jax.ShapeDtypeStruct((rows * ROW_TILE, LANES), jnp.uint32),
        scratch_shapes=[
            pltpu.VMEM((tt * ROW_TILE, LANES), jnp.uint32),
            pltpu.VMEM((FILL_SIZES[0] * ROW_TILE, LANES), jnp.uint32),
            pltpu.SemaphoreType.DMA(()),
            pltpu.SemaphoreType.DMA(()),
        ],
        compiler_params=pltpu.CompilerParams(
            dimension_semantics=("arbitrary",), vmem_limit_bytes=VMEM_LIMIT, has_side_effects=True),
        name="moe_dispatch",
    )(dest3, counts, pstart, padded, h1)


def _experts_body(blk_e_ref, nused_ref, first_ref, slot_ref, next_e_ref, x_ref, wg_hbm, wu_hbm, wd_hbm, y_ref,
                  rawg, rawu, rawd, wgb, wub, wdb, sem):
    i = pl.program_id(0)

    def weight_copies(e, s):
        return (pltpu.make_async_copy(wg_hbm.at[e], rawg.at[s], sem.at[s, 0]),
                pltpu.make_async_copy(wu_hbm.at[e], rawu.at[s], sem.at[s, 1]),
                pltpu.make_async_copy(wd_hbm.at[e], rawd.at[s], sem.at[s, 2]))

    @pl.when(i < nused_ref[0])
    def _():
        e = blk_e_ref[i]
        s = slot_ref[i]

        @pl.when(first_ref[i] == 1)
        def _():
            @pl.when(i == 0)
            def _():
                for cp in weight_copies(e, s):
                    cp.start()

            for cp in weight_copies(e, s):
                cp.wait()
            wgb[...] = rawg[s].astype(bf16)
            wub[...] = rawu[s].astype(bf16)
            wdb[...] = rawd[s].astype(bf16)

            @pl.when(next_e_ref[i] >= 0)
            def _():
                for cp in weight_copies(next_e_ref[i], 1 - s):
                    cp.start()

        x = _unpack_bf16_pairs(_load_tile_rows(x_ref, MOE_BLOCK)).astype(bf16)
        hg = jnp.dot(x, wgb[...], preferred_element_type=f32)
        hu = jnp.dot(x, wub[...], preferred_element_type=f32)
        hb = (_silu(hg) * hu).astype(bf16)
        _store_tile_rows(y_ref, _pack_bf16_pairs(jnp.dot(hb, wdb[...], preferred_element_type=f32)))


def _experts(blk_e, nused, first, slot, next_e, xs, w_gate, w_up, w_down):
    rows = xs.shape[0] // ROW_TILE
    nblk = rows // MOE_BLOCK
    blk = lambda i, be, nu, *_: (jnp.minimum(i, nu[0] - 1), 0)
    hbm = pl.BlockSpec(memory_space=pl.ANY)
    grid_spec = pltpu.PrefetchScalarGridSpec(
        num_scalar_prefetch=5,
        grid=(nblk,),
        in_specs=[pl.BlockSpec((MOE_BLOCK * ROW_TILE, LANES), blk), hbm, hbm, hbm],
        out_specs=pl.BlockSpec((MOE_BLOCK * ROW_TILE, LANES), blk),
        scratch_shapes=[
            pltpu.VMEM((2, D_MODEL, EXPERT_DIM), f32),
            pltpu.VMEM((2, D_MODEL, EXPERT_DIM), f32),
            pltpu.VMEM((2, EXPERT_DIM, D_MODEL), f32),
            pltpu.VMEM((D_MODEL, EXPERT_DIM), bf16),
            pltpu.VMEM((D_MODEL, EXPERT_DIM), bf16),
            pltpu.VMEM((EXPERT_DIM, D_MODEL), bf16),
            pltpu.SemaphoreType.DMA((2, 3)),
        ],
    )
    return pl.pallas_call(
        _experts_body,
        grid_spec=grid_spec,
        out_shape=jax.ShapeDtypeStruct((rows * ROW_TILE, LANES), jnp.uint32),
        compiler_params=pltpu.CompilerParams(
            dimension_semantics=("arbitrary",), vmem_limit_bytes=VMEM_LIMIT),
        name="routed_experts",
    )(blk_e, nused, first, slot, next_e, xs, w_gate, w_up, w_down)


def _combine_body(d0_ref, d1_ref, d2_ref, h_ref, gate_ref, wsg_ref, wsu_ref, wsd_ref, g2_ref, b2_ref, ys_ref,
                  o_ref, buf0, buf1, sem, *, tm):
    i = pl.program_id(0)

    def issue(d_ref, buf, s):
        for t in range(tm):
            for k in range(TOP_K):
                _row_copy(ys_ref, d_ref[0, k, t], buf.at[k], t, sem.at[s]).start(priority=k % 2)

    def drain(buf, s):
        def body(t, c):
            for k in range(TOP_K):
                _row_copy(ys_ref, 0, buf.at[k], 0, sem.at[s]).wait()
            return c
        lax.fori_loop(0, tm, body, 0, unroll=8)

    def tile(rows, buf):
        h = h_ref[rows]
        hb = h.astype(bf16)
        sg = jnp.dot(hb, wsg_ref[...], preferred_element_type=f32)
        su = jnp.dot(hb, wsu_ref[...], preferred_element_type=f32)
        ffn = jnp.dot((_silu(sg) * su).astype(bf16), wsd_ref[...], preferred_element_type=f32)
        gate = gate_ref[rows]
        for k in range(TOP_K):
            ffn = ffn + gate[:, k:k + 1] * _unpack_bf16_pairs(_load_tile_rows(buf.at[k], tm))
        o_ref[rows] = _layer_norm(DEEPNORM_ALPHA * h + ffn, g2_ref[...], b2_ref[...])

    @pl.when(i == 0)
    def _():
        issue(d0_ref, buf0, 0)

    drain(buf0, 0)
    issue(d1_ref, buf1, 1)
    tile(slice(0, tm), buf0)
    drain(buf1, 1)
    issue(d2_ref, buf0, 0)
    tile(slice(tm, 2 * tm), buf1)

    @pl.when(i == pl.num_programs(0) - 1)
    def _():
        drain(buf0, 0)


def _combine(dest3, h1, ys, gate_tok, wsg, wsu, wsd, g2, b2, *, tm):
    m = h1.shape[0]
    nt = m // tm
    dest_tile = lambda f: pl.BlockSpec((1, TOP_K, tm), lambda i: (f(i), 0, 0), memory_space=pltpu.SMEM)
    return pl.pallas_call(
        functools.partial(_combine_body, tm=tm),
        grid=(nt // 2,),
        in_specs=[
            dest_tile(lambda i: 2 * i),
            dest_tile(lambda i: 2 * i + 1),
            dest_tile(lambda i: jnp.minimum(2 * i + 2, nt - 1)),
            pl.BlockSpec((2 * tm, D_MODEL), lambda i: (i, 0)),
            pl.BlockSpec((2 * tm, TOP_K), lambda i: (i, 0)),
            pl.BlockSpec((D_MODEL, EXPERT_DIM), lambda i: (0, 0)),
            pl.BlockSpec((D_MODEL, EXPERT_DIM), lambda i: (0, 0)),
            pl.BlockSpec((EXPERT_DIM, D_MODEL), lambda i: (0, 0)),
            pl.BlockSpec((1, D_MODEL), lambda i: (0, 0)),
            pl.BlockSpec((1, D_MODEL), lambda i: (0, 0)),
            pl.BlockSpec(memory_space=pl.ANY),
        ],
        out_specs=pl.BlockSpec((2 * tm, D_MODEL), lambda i: (i, 0)),
        out_shape=jax.ShapeDtypeStruct((m, D_MODEL), f32),
        scratch_shapes=[
            pltpu.VMEM((TOP_K, tm * ROW_TILE, LANES), jnp.uint32),
            pltpu.VMEM((TOP_K, tm * ROW_TILE, LANES), jnp.uint32),
            pltpu.SemaphoreType.DMA((2,)),
        ],
        compiler_params=pltpu.CompilerParams(
            dimension_semantics=("arbitrary",), vmem_limit_bytes=VMEM_LIMIT),
        name="combine_ln2",
    )(dest3, dest3, dest3, h1, gate_tok, wsg, wsu, wsd, g2, b2, ys)


def _rope_tables(pos, width):
    half = ROPE_DIM // 2
    inv_freq = jnp.power(ROPE_THETA, -jnp.arange(0, ROPE_DIM, 2, dtype=f32) / ROPE_DIM)
    ang = pos.astype(f32)[:, None] * inv_freq[None, :]
    cos, sin = jnp.cos(ang), jnp.sin(ang)
    n = pos.shape[0]
    pad = jnp.zeros((n, HEAD_DIM - ROPE_DIM), f32)
    zero = jnp.zeros((n, half), f32)
    c = jnp.concatenate([cos, cos, pad + 1.0], axis=1)
    s1 = jnp.concatenate([-sin, zero, pad], axis=1)
    s2 = jnp.concatenate([zero, sin, pad], axis=1)
    tab = jnp.stack([c, s1, s2])
    return jnp.tile(tab, (1, 1, width // HEAD_DIM))


def _group_lanes(v):
    v = v.reshape(SSM_GROUPS, SSM_HEADS_PER_GROUP)
    return jnp.pad(v, ((0, 0), (0, LANES - SSM_HEADS_PER_GROUP))).reshape(1, DT_WIDTH)


def kernel(x, meta_tokens, ln_in_g, ln_in_b, w_in, conv_w, conv_b, dt_bias, a_log, d_skip, ssm_norm_g, att_norm_g, attn_sinks, w_out, ln1_g, ln1_b, w_router, b_router, w_gate, w_up, w_down, ws_gate, ws_up, ws_down, ln2_g, ln2_b):
    batch, seq, d = x.shape
    assert d == D_MODEL and seq % BLOCK == 0 and meta_tokens.shape == (N_META, D_MODEL)
    assert w_in.shape[0] == 1, "single layer"
    n_tok = batch * seq
    nblk = seq // BLOCK
    row = lambda v: v.reshape(1, -1).astype(f32)

    x2d = x.reshape(n_tok, D_MODEL)
    gi, bi = row(ln_in_g), row(ln_in_b)
    w_main = _cast_bf16(w_in, cols=MAIN_WIDTH, bk=D_MODEL, bn=512)
    w_dt = jnp.pad(w_in[0, :, MAIN_WIDTH:], ((0, 0), (0, LANES - SSM_HEADS))).astype(bf16)

    proj, dtx = _ln_inproj(x2d, gi, bi, w_main, w_dt, tm=1024, tn=512)
    proj_m, dtx_m = _ln_inproj(meta_tokens.astype(f32), gi, bi, w_main, w_dt, tm=N_META, tn=512)
    proj_m = jnp.pad(proj_m, ((BLOCK - N_META, 0), (0, 0)))
    dtx_m = jnp.pad(dtx_m, ((CHUNK - N_META, 0), (0, 0)))

    gw = ATT_GROUP * HEAD_DIM
    tab = _rope_tables(N_META + jnp.arange(seq), gw)
    tab_meta = _rope_tables(jnp.arange(BLOCK), gw)
    kv_meta = jnp.roll(proj_m[:, ATT_WIDTH:ATT_WIDTH + 2 * KV_WIDTH], N_META, axis=0)
    att = _attention(proj, kv_meta, tab, tab_meta, attn_sinks[0].astype(f32), row(att_norm_g[0]),
                     batch=batch, nblk=nblk)

    conv_w0 = conv_w[0].astype(f32)
    conv_b0 = row(conv_b[0])
    dtb_x = _group_lanes(dt_bias[0].astype(f32))
    alog_x = _group_lanes(a_log[0].astype(f32))
    dskip_ch = jnp.repeat(d_skip[0].astype(f32), HEAD_DIM).reshape(1, SSM_WIDTH)
    ng = row(ssm_norm_g[0])
    zeros_halo = jnp.zeros((8, CONV_CH), f32)
    zeros_state = jnp.zeros((SSM_GROUPS, SSM_STATE, SSM_GROUP_WIDTH), f32)
    _, s_meta = _ssd(proj_m, dtx_m, zeros_halo, zeros_state, conv_w0, conv_b0, dtb_x, alog_x, dskip_ch, ng,
                     batch=1, nchunk=1, vstart=CHUNK - N_META, emit_state=True)
    halo = proj_m[CHUNK - 8:, XBC_COL:].astype(f32)
    (ssm,) = _ssd(proj, dtx, halo, s_meta, conv_w0, conv_b0, dtb_x, alog_x, dskip_ch, ng,
                  batch=batch, nchunk=nblk, vstart=0, emit_state=False)

    w_out_b = _cast_bf16(w_out, cols=D_MODEL, bk=1024, bn=D_MODEL)
    h1 = _outproj(att, ssm, w_out_b, x2d, gi, bi, row(ln1_g[0]), row(ln1_b[0]),
                  tm=512, tk=1024)

    tt = 512
    tri = (jnp.arange(tt)[:, None] <= jnp.arange(tt)[None, :]).astype(bf16)
    eidx, gate, rank, counts = _router(h1, w_router[0].T.astype(f32), b_router[0].reshape(N_EXPERTS, 1).astype(f32),
                                       tri, tt=tt)
    counts = counts.reshape(N_EXPERTS)
    padded = (counts + MOE_BLOCK - 1) // MOE_BLOCK * MOE_BLOCK
    pend = jnp.cumsum(padded)
    pstart = pend - padded
    first_row = jnp.sum(jnp.where(eidx[..., None] == jnp.arange(N_EXPERTS, dtype=i32), pstart.astype(i32), 0), axis=-1)
    dest = first_row + rank
    n_blocks = n_tok * TOP_K // MOE_BLOCK + N_EXPERTS
    blk_first = jnp.arange(n_blocks, dtype=i32) * MOE_BLOCK
    blk_e = jnp.minimum(jnp.sum(pend[None, :] <= blk_first[:, None], axis=1), N_EXPERTS - 1).astype(i32)
    nused = (pend[-1] // MOE_BLOCK).astype(i32).reshape(1)
    rows = n_blocks * MOE_BLOCK
    tiles = lambda t: dest.reshape(TOP_K, n_tok // t, t).transpose(1, 0, 2)

    td, tc = 256, 128
    xs = _dispatch(tiles(td), counts, pstart.astype(i32), padded.astype(i32), h1, rows=rows, tt=td)
    eids = jnp.arange(N_EXPERTS, dtype=i32)
    nonempty = counts > 0
    ordinal = jnp.cumsum(nonempty.astype(i32)) - nonempty.astype(i32)
    later = (eids[None, :] > eids[:, None]) & nonempty[None, :]
    next_nonempty = jnp.min(jnp.where(later, eids[None, :], N_EXPERTS), axis=1)
    next_nonempty = jnp.where(next_nonempty == N_EXPERTS, -1, next_nonempty).astype(i32)
    onehot_e = blk_e[:, None] == eids[None, :]
    pick = lambda table: jnp.sum(jnp.where(onehot_e, table[None, :], 0), axis=1).astype(i32)
    first = jnp.concatenate([jnp.ones((1,), i32), (blk_e[1:] != blk_e[:-1]).astype(i32)])
    ys = _experts(blk_e, nused, first, pick(ordinal) % 2, pick(next_nonempty), xs, w_gate[0], w_up[0], w_down[0])
    out = _combine(tiles(tc), h1, ys, gate.T, ws_gate[0].astype(bf16), ws_up[0].astype(bf16),
                   ws_down[0].astype(bf16), row(ln2_g[0]), row(ln2_b[0]), tm=tc)
    return out.reshape(batch, seq, D_MODEL)
```

```python
import functools
import math

import jax
import jax.numpy as jnp
from jax import lax
from jax.experimental import pallas as pl
from jax.experimental.pallas import tpu as pltpu

f32 = jnp.float32
bf16 = jnp.bfloat16
i32 = jnp.int32

D_MODEL = 2048
N_META = 16
HEAD_DIM = 64
ATT_HEADS = 32
ATT_KV_HEADS = 4
ATT_GROUP = ATT_HEADS // ATT_KV_HEADS
ATT_WIDTH = 2048
KV_WIDTH = 256
BLOCK = 128
ROPE_DIM = 16
ROPE_THETA = 500000.0
SSM_WIDTH = 4096
SSM_HEADS = 64
SSM_GROUPS = 8
SSM_HEADS_PER_GROUP = SSM_HEADS // SSM_GROUPS
SSM_STATE = 128
SSM_GROUP_WIDTH = SSM_WIDTH // SSM_GROUPS
CONV_WIDTH = 4
CHUNK = 128
CONV_CH = SSM_WIDTH + 2 * SSM_GROUPS * SSM_STATE
MAIN_WIDTH = ATT_WIDTH + 2 * KV_WIDTH + SSM_WIDTH + CONV_CH
Z_COL = ATT_WIDTH + 2 * KV_WIDTH
XBC_COL = Z_COL + SSM_WIDTH
N_EXPERTS = 64
EXPERT_DIM = 512
TOP_K = 8
N_EXPERT_GROUPS = 8
EXPERTS_PER_GROUP = N_EXPERTS // N_EXPERT_GROUPS
TOPK_GROUPS = 4
ROUTED_SCALE = 2.5
MOE_BLOCK = 256
DEEPNORM_ALPHA = 2.0 ** 0.25
LN_EPS = 1e-5
RMS_EPS = 1e-6
NEG_INF = -1e30
LANES = 128
SUBLANES = 8
DT_WIDTH = SSM_GROUPS * LANES
VMEM_LIMIT = 56 * 1024 * 1024


def _layer_norm(x, g, b):
    mu = jnp.mean(x, axis=-1, keepdims=True)
    xc = x - mu
    var = jnp.mean(xc * xc, axis=-1, keepdims=True)
    return xc * lax.rsqrt(var + LN_EPS) * g + b


def _silu(x):
    return x * (1.0 / (1.0 + jnp.exp(-x)))


def _softplus(x):
    return jnp.maximum(x, 0.0) + jnp.log1p(jnp.exp(-jnp.abs(x)))


def _ln_inproj_body(x_ref, g_ref, b_ref, w_ref, wdt_ref, o_ref, dt_ref, h_scr):
    @pl.when(pl.program_id(1) == 0)
    def _():
        h = _layer_norm(x_ref[...], g_ref[...], b_ref[...]).astype(bf16)
        h_scr[...] = h
        dt_ref[...] = jnp.dot(h, wdt_ref[...], preferred_element_type=f32)

    o_ref[...] = jnp.dot(h_scr[...], w_ref[...], preferred_element_type=f32).astype(bf16)


def _ln_inproj(x2d, g, b, w_main, w_dt, *, tm, tn):
    m = x2d.shape[0]
    return pl.pallas_call(
        _ln_inproj_body,
        grid=(m // tm, MAIN_WIDTH // tn),
        in_specs=[
            pl.BlockSpec((tm, D_MODEL), lambda i, j: (i, 0)),
            pl.BlockSpec((1, D_MODEL), lambda i, j: (0, 0)),
            pl.BlockSpec((1, D_MODEL), lambda i, j: (0, 0)),
            pl.BlockSpec((D_MODEL, tn), lambda i, j: (0, j)),
            pl.BlockSpec((D_MODEL, LANES), lambda i, j: (0, 0)),
        ],
        out_specs=[
            pl.BlockSpec((tm, tn), lambda i, j: (i, j)),
            pl.BlockSpec((tm, LANES), lambda i, j: (i, 0)),
        ],
        out_shape=[
            jax.ShapeDtypeStruct((m, MAIN_WIDTH), bf16),
            jax.ShapeDtypeStruct((m, LANES), f32),
        ],
        scratch_shapes=[pltpu.VMEM((tm, D_MODEL), bf16)],
        compiler_params=pltpu.CompilerParams(
            dimension_semantics=("arbitrary", "arbitrary"), vmem_limit_bytes=VMEM_LIMIT),
        name="ln_inproj",
    )(x2d, g, b, w_main, w_dt)


def _rotate(t, tab):
    w = t.shape[-1]
    half = ROPE_DIM // 2
    return (t * tab[0]
            + pltpu.roll(t, w - half, 1) * tab[1]
            + pltpu.roll(t, half, 1) * tab[2])


def _attention_body(sink_ref, q_ref, kvc_ref, kvp_ref, kvm_ref, tabc_ref, tabp_ref, tabm_ref,
                    g_ref, o_ref):
    j = pl.program_id(1)
    tabc = tabc_ref[...]
    kc = _rotate(kvc_ref[:, :KV_WIDTH].astype(f32), tabc[:, :, :KV_WIDTH]).astype(bf16)
    kp = _rotate(kvp_ref[:, :KV_WIDTH].astype(f32), tabp_ref[:, :, :KV_WIDTH]).astype(bf16)
    km = _rotate(kvm_ref[:, :KV_WIDTH].astype(f32), tabm_ref[:, :, :KV_WIDTH]).astype(bf16)
    v_all = jnp.concatenate([kvp_ref[:, KV_WIDTH:], kvc_ref[:, KV_WIDTH:], kvm_ref[:, KV_WIDTH:]], axis=0)

    rows = ATT_GROUP * BLOCK
    r = lax.broadcasted_iota(i32, (rows, BLOCK), 0) % BLOCK
    c = lax.broadcasted_iota(i32, (rows, BLOCK), 1)
    lower = c <= r
    prev_ok = (c > r) & (j > 0)
    meta_ok = c < N_META
    head_of_row = lax.broadcasted_iota(i32, (rows, 1), 0) // BLOCK
    nt_dims = (((1,), (1,)), ((), ()))

    outs = []
    for g in range(ATT_KV_HEADS):
        gw = ATT_GROUP * HEAD_DIM
        hs = slice(g * HEAD_DIM, (g + 1) * HEAD_DIM)
        qg = _rotate(q_ref[:, g * gw:(g + 1) * gw].astype(f32), tabc) * (HEAD_DIM ** -0.5)
        qs = jnp.concatenate([qg[:, h * HEAD_DIM:(h + 1) * HEAD_DIM] for h in range(ATT_GROUP)],
                             axis=0).astype(bf16)
        s_cur = lax.dot_general(qs, kc[:, hs], nt_dims, preferred_element_type=f32)
        s_prev = lax.dot_general(qs, kp[:, hs], nt_dims, preferred_element_type=f32)
        s_meta = lax.dot_general(qs, km[:, hs], nt_dims, preferred_element_type=f32)
        s_band = jnp.where(lower, s_cur, jnp.where(prev_ok, s_prev, NEG_INF))
        s_meta = jnp.where(meta_ok, s_meta, NEG_INF)
        sink = jnp.zeros((rows, 1), f32)
        for h in range(ATT_GROUP):
            sink = jnp.where(head_of_row == h, sink_ref[g * ATT_GROUP + h], sink)
        m = jnp.maximum(jnp.max(jnp.maximum(s_band, s_meta), axis=1, keepdims=True), sink)
        p_band = jnp.exp(s_band - m)
        p_meta = jnp.exp(s_meta - m)
        denom = jnp.sum(p_band + p_meta, axis=1, keepdims=True) + jnp.exp(sink - m)
        p_cur = jnp.where(lower, p_band, 0.0)
        p = jnp.concatenate([p_band - p_cur, p_cur, p_meta], axis=1).astype(bf16)
        o = jnp.dot(p, v_all[:, hs], preferred_element_type=f32) / denom
        outs.append(jnp.concatenate([o[h * BLOCK:(h + 1) * BLOCK] for h in range(ATT_GROUP)], axis=1))
    att = jnp.concatenate(outs, axis=1)
    att = att * lax.rsqrt(jnp.mean(att * att, axis=-1, keepdims=True) + RMS_EPS) * g_ref[...]
    o_ref[...] = att.astype(bf16)


def _attention(proj, kv_meta, tab, tab_meta, sinks, att_g, *, batch, nblk):
    m = proj.shape[0]
    kvb = ATT_WIDTH // (2 * KV_WIDTH)
    gw = ATT_GROUP * HEAD_DIM
    return pl.pallas_call(
        _attention_body,
        grid=(batch, nblk),
        in_specs=[
            pl.BlockSpec(memory_space=pltpu.SMEM),
            pl.BlockSpec((BLOCK, ATT_WIDTH), lambda b, j: (b * nblk + j, 0)),
            pl.BlockSpec((BLOCK, 2 * KV_WIDTH), lambda b, j: (b * nblk + j, kvb)),
            pl.BlockSpec((BLOCK, 2 * KV_WIDTH), lambda b, j: (b * nblk + jnp.maximum(j - 1, 0), kvb)),
            pl.BlockSpec((BLOCK, 2 * KV_WIDTH), lambda b, j: (0, 0)),
            pl.BlockSpec((3, BLOCK, gw), lambda b, j: (0, j, 0)),
            pl.BlockSpec((3, BLOCK, gw), lambda b, j: (0, jnp.maximum(j - 1, 0), 0)),
            pl.BlockSpec((3, BLOCK, gw), lambda b, j: (0, 0, 0)),
            pl.BlockSpec((1, ATT_WIDTH), lambda b, j: (0, 0)),
        ],
        out_specs=pl.BlockSpec((BLOCK, ATT_WIDTH), lambda b, j: (b * nblk + j, 0)),
        out_shape=jax.ShapeDtypeStruct((m, ATT_WIDTH), bf16),
        compiler_params=pltpu.CompilerParams(
            dimension_semantics=("arbitrary", "arbitrary"), vmem_limit_bytes=VMEM_LIMIT),
        name="swa_attention",
    )(sinks, proj, proj, proj, kv_meta, tab, tab, tab_meta, att_g)


def _cumsum_rows(x):
    n = x.shape[0]
    row = lax.broadcasted_iota(i32, x.shape, 0)
    k = 1
    while k < n:
        x = x + jnp.where(row >= k, pltpu.roll(x, k, 0), 0.0)
        k *= 2
    return x


def _expand_heads(col):
    lane = lax.broadcasted_iota(i32, col.shape, 1) // HEAD_DIM
    per_tile = LANES // HEAD_DIM
    return jnp.concatenate(
        [jnp.take_along_axis(col, lane + per_tile * j, axis=1) for j in range(SSM_GROUP_WIDTH // LANES)], axis=1)


def _ssd_body(proj_ref, dt_ref, halo_ref, s0_ref, cw_ref, cb_ref, dtb_ref, alog_ref, dskip_ref, ng_ref,
              *rest, vstart, emit_state):
    if emit_state:
        y_ref, sout_ref, state, halo, work = rest
    else:
        y_ref, state, halo, work = rest

    @pl.when(pl.program_id(1) == 0)
    def _():
        state[...] = s0_ref[...]
        halo[...] = halo_ref[...]

    if vstart == 0:
        keep_valid = lambda v: v
    else:
        valid = lax.broadcasted_iota(i32, (CHUNK, 1), 0) >= vstart
        keep_valid = lambda v: jnp.where(valid, v, 0.0)
    lane = lax.broadcasted_iota(i32, (CHUNK, LANES), 1)
    causal =(lax.broadcasted_iota(i32, (CHUNK, CHUNK), 0) >= lax.broadcasted_iota(i32, (CHUNK, CHUNK), 1))

    def conv(col, width):
        u = proj_ref[:, pl.ds(pl.multiple_of(XBC_COL + col, LANES), width)].astype(f32)
        cs = pl.ds(pl.multiple_of(col, LANES), width)
        work[0:8, 0:width] = halo[:, cs]
        work[8:8 + CHUNK, 0:width] = u
        w = cw_ref[:, cs]
        acc = cb_ref[:, cs] + u * w[CONV_WIDTH - 1:CONV_WIDTH]
        for jj in range(CONV_WIDTH - 1):
            acc = acc + work[5 + jj:5 + jj + CHUNK, 0:width] * w[jj:jj + 1]
        halo[:, cs] = u[CHUNK - 8:]
        return keep_valid(_silu(acc))

    def group(g, carry):
        gs = pl.ds(pl.multiple_of(g * SSM_GROUP_WIDTH, SSM_GROUP_WIDTH), SSM_GROUP_WIDTH)
        gl = pl.ds(pl.multiple_of(g * LANES, LANES), LANES)
        xs = conv(g * SSM_GROUP_WIDTH, SSM_GROUP_WIDTH)
        bm = conv(SSM_WIDTH + g * SSM_STATE, SSM_STATE)
        cm = conv(SSM_WIDTH + SSM_GROUPS * SSM_STATE + g * SSM_STATE, SSM_STATE)

        dt_raw = jnp.take_along_axis(dt_ref[...], (lane + g * SSM_HEADS_PER_GROUP) % LANES, axis=1)
        dt = keep_valid(_softplus(dt_raw + dtb_ref[:, gl]))
        a_cs = _cumsum_rows(dt * -jnp.exp(alog_ref[:, gl]))
        a_cs_t = a_cs.T
        a_last = a_cs[CHUNK - 1:CHUNK]

        xdt = xs * _expand_heads(dt)
        bmb = bm.astype(bf16)
        cmb = cm.astype(bf16)
        cb = lax.dot_general(cmb, bmb, (((1,), (1,)), ((), ())), preferred_element_type=f32)
        xdtb = xdt.astype(bf16)
        ys = []
        for i in range(SSM_HEADS_PER_GROUP):
            seg = a_cs[:, i:i + 1] - a_cs_t[i:i + 1, :]
            lmat = (cb * jnp.exp(jnp.where(causal, seg, -jnp.inf))).astype(bf16)
            ys.append(jnp.dot(lmat, xdtb[:, i * HEAD_DIM:(i + 1) * HEAD_DIM], preferred_element_type=f32))
        y = jnp.concatenate(ys, axis=1)

        st = state[g]
        y_off = jnp.dot(cmb, st.astype(bf16), preferred_element_type=f32)
        y = y + y_off * _expand_heads(jnp.exp(a_cs))

        x_end = (xdt * _expand_heads(jnp.exp(a_last - a_cs))).astype(bf16)
        upd = lax.dot_general(bmb, x_end, (((0,), (0,)), ((), ())), preferred_element_type=f32)
        keep = _expand_heads(jnp.exp(a_cs[CHUNK - SUBLANES:]))[SUBLANES - 1:]
        state[g] = st * keep + upd

        y = y + dskip_ref[:, gs] * xs
        y = y * _silu(proj_ref[:, pl.ds(pl.multiple_of(Z_COL + g * SSM_GROUP_WIDTH, SSM_GROUP_WIDTH),
                                       SSM_GROUP_WIDTH)].astype(f32))
        y = y * lax.rsqrt(jnp.mean(y * y, axis=-1, keepdims=True) + RMS_EPS) * ng_ref[:, gs]
        y_ref[:, gs] = y.astype(bf16)
        return carry

    lax.fori_loop(0, SSM_GROUPS, group, 0)
    if emit_state:
        sout_ref[...] = state[...]


def _ssd(proj, dtx, halo, s0, conv_w, conv_b, dt_bias_x, a_log_x, d_skip_ch, norm_g,
         *, batch, nchunk, vstart, emit_state):
    m = proj.shape[0]
    full = lambda shape: pl.BlockSpec(shape, lambda b, c: (0,) * len(shape))
    state_shape = (SSM_GROUPS, SSM_STATE, SSM_GROUP_WIDTH)
    in_specs = [
        pl.BlockSpec((CHUNK, MAIN_WIDTH), lambda b, c: (b * nchunk + c, 0)),
        pl.BlockSpec((CHUNK, LANES), lambda b, c: (b * nchunk + c, 0)),
        full((8, CONV_CH)),
        full(state_shape),
        full((CONV_WIDTH, CONV_CH)),
        full((1, CONV_CH)),
        full((1, DT_WIDTH)),
        full((1, DT_WIDTH)),
        full((1, SSM_WIDTH)),
        full((1, SSM_WIDTH)),
    ]
    out_specs = [pl.BlockSpec((CHUNK, SSM_WIDTH), lambda b, c: (b * nchunk + c, 0))]
    out_shape = [jax.ShapeDtypeStruct((m, SSM_WIDTH), bf16)]
    if emit_state:
        out_specs.append(full(state_shape))
        out_shape.append(jax.ShapeDtypeStruct(state_shape, f32))
    return pl.pallas_call(
        functools.partial(_ssd_body, vstart=vstart, emit_state=emit_state),
        grid=(batch, nchunk),
        in_specs=in_specs,
        out_specs=out_specs,
        out_shape=out_shape,
        scratch_shapes=[
            pltpu.VMEM(state_shape, f32),
            pltpu.VMEM((8, CONV_CH), f32),
            pltpu.VMEM((8 + CHUNK, SSM_GROUP_WIDTH), f32),
        ],
        compiler_params=pltpu.CompilerParams(
            dimension_semantics=("arbitrary", "arbitrary"), vmem_limit_bytes=VMEM_LIMIT),
        name="ssd_state" if emit_state else "ssd_scan",
    )(proj, dtx, halo, s0, conv_w, conv_b, dt_bias_x, a_log_x, d_skip_ch, norm_g)


def _outproj_body(att_ref, ssm_ref, w_ref, x_ref, gi_ref, bi_ref, g1_ref, b1_ref, h_ref, acc,
                  *, n_att_k):
    k = pl.program_id(1)

    @pl.when(k == 0)
    def _():
        acc[...] = jnp.zeros_like(acc)

    @pl.when(k < n_att_k)
    def _():
        acc[...] += jnp.dot(att_ref[...], w_ref[...], preferred_element_type=f32)

    @pl.when(k >= n_att_k)
    def _():
        acc[...] += jnp.dot(ssm_ref[...], w_ref[...], preferred_element_type=f32)

    @pl.when(k == pl.num_programs(1) - 1)
    def _():
        h0 = _layer_norm(x_ref[...], gi_ref[...], bi_ref[...])
        h_ref[...] = _layer_norm(DEEPNORM_ALPHA * h0 + acc[...], g1_ref[...], b1_ref[...])


def _outproj(att, ssm, w_out, x2d, gi, bi, g1, b1, *, tm, tk):
    m = att.shape[0]
    n_att_k = ATT_WIDTH // tk
    nk = (ATT_WIDTH + SSM_WIDTH) // tk
    return pl.pallas_call(
        functools.partial(_outproj_body, n_att_k=n_att_k),
        grid=(m // tm, nk),
        in_specs=[
            pl.BlockSpec((tm, tk), lambda i, k: (i, jnp.minimum(k, n_att_k - 1))),
            pl.BlockSpec((tm, tk), lambda i, k: (i, jnp.maximum(k - n_att_k, 0))),
            pl.BlockSpec((tk, D_MODEL), lambda i, k: (k, 0)),
            pl.BlockSpec((tm, D_MODEL), lambda i, k: (i, 0)),
            pl.BlockSpec((1, D_MODEL), lambda i, k: (0, 0)),
            pl.BlockSpec((1, D_MODEL), lambda i, k: (0, 0)),
            pl.BlockSpec((1, D_MODEL), lambda i, k: (0, 0)),
            pl.BlockSpec((1, D_MODEL), lambda i, k: (0, 0)),
        ],
        out_specs=pl.BlockSpec((tm, D_MODEL), lambda i, k: (i, 0)),
        out_shape=jax.ShapeDtypeStruct((m, D_MODEL), f32),
        scratch_shapes=[pltpu.VMEM((tm, D_MODEL), f32)],
        compiler_params=pltpu.CompilerParams(
            dimension_semantics=("arbitrary", "arbitrary"), vmem_limit_bytes=VMEM_LIMIT),
        name="outproj_ln1",
    )(att, ssm, w_out, x2d, gi, bi, g1, b1)


def _max01(v):
    return jnp.max(jnp.max(v, axis=1, keepdims=True), axis=0, keepdims=True)


def _router_body(h_ref, wr_ref, br_ref, tri_ref, eidx_ref, gate_ref, rank_ref, cnt_ref, running):
    tt = h_ref.shape[0]
    shape3 = (N_EXPERT_GROUPS, EXPERTS_PER_GROUP, tt)

    @pl.when(pl.program_id(0) == 0)
    def _():
        running[...] = jnp.zeros_like(running)

    logits = lax.dot_general(wr_ref[...], h_ref[...], (((1,), (1,)), ((), ())),
                             precision=lax.Precision.HIGHEST, preferred_element_type=f32)
    scores = (1.0 / (1.0 + jnp.exp(-logits)))
    sel3 = (scores + br_ref[...]).reshape(shape3)
    scores3 = scores.reshape(shape3)
    within = lax.broadcasted_iota(i32, shape3, 1).astype(f32)
    m1 = jnp.max(sel3, axis=1, keepdims=True)
    i1 = jnp.min(jnp.where(sel3 == m1, within, float(EXPERTS_PER_GROUP)), axis=1, keepdims=True)
    m2 = jnp.max(jnp.where(within == i1, -jnp.inf, sel3), axis=1, keepdims=True)
    gs = m1 + m2
    giota = lax.broadcasted_iota(i32, gs.shape, 0).astype(f32)
    gmask = jnp.zeros(gs.shape, f32)
    for _ in range(TOPK_GROUPS):
        gm = jnp.max(gs, axis=0, keepdims=True)
        gi = jnp.min(jnp.where(gs == gm, giota, float(N_EXPERT_GROUPS)), axis=0, keepdims=True)
        hit = giota == gi
        gmask = jnp.where(hit, 1.0, gmask)
        gs = jnp.where(hit, -jnp.inf, gs)
    selm = jnp.where(gmask > 0.0, sel3, -jnp.inf)
    eiota = (lax.broadcasted_iota(i32, shape3, 0) * EXPERTS_PER_GROUP
             + lax.broadcasted_iota(i32, shape3, 1)).astype(f32)
    eidx, gates, hits = [], [], []
    member = jnp.zeros(shape3, f32)
    gsum = jnp.zeros((1, 1, tt), f32)
    for _ in range(TOP_K):
        m = _max01(selm)
        ei = -_max01(-jnp.where(selm == m, eiota, float(N_EXPERTS)))
        hit = eiota == ei
        gk = jnp.sum(jnp.sum(jnp.where(hit, scores3, 0.0), axis=1, keepdims=True), axis=0, keepdims=True)
        eidx.append(ei)
        gates.append(gk)
        hits.append(hit)
        gsum = gsum + gk
        member = jnp.where(hit, 1.0, member)
        selm = jnp.where(hit, -jnp.inf, selm)
    member2 = member.reshape(N_EXPERTS, tt)
    incl = jnp.dot(member2.astype(bf16), tri_ref[...], preferred_element_type=f32)
    base = (running[...] + (incl - member2)).reshape(shape3)
    for k in range(TOP_K):
        rk = jnp.sum(jnp.sum(jnp.where(hits[k], base, 0.0), axis=1, keepdims=True), axis=0, keepdims=True)
        eidx_ref[k:k + 1, :] = eidx[k].reshape(1, tt).astype(i32)
        gate_ref[k:k + 1, :] = (gates[k] / gsum * ROUTED_SCALE).reshape(1, tt)
        rank_ref[k:k + 1, :] = rk.reshape(1, tt).astype(i32)
    total = running[...] + incl[:, tt - 1:tt]
    running[...] = total
    cnt_ref[...] = total.astype(i32)


def _router(h1, w_router_t, b_router_col, tri, *, tt):
    m = h1.shape[0]
    return pl.pallas_call(
        _router_body,
        grid=(m // tt,),
        in_specs=[
            pl.BlockSpec((tt, D_MODEL), lambda i: (i, 0)),
            pl.BlockSpec((N_EXPERTS, D_MODEL), lambda i: (0, 0)),
            pl.BlockSpec((N_EXPERTS, 1), lambda i: (0, 0)),
            pl.BlockSpec((tt, tt), lambda i: (0, 0)),
        ],
        out_specs=[
            pl.BlockSpec((TOP_K, tt), lambda i: (0, i)),
            pl.BlockSpec((TOP_K, tt), lambda i: (0, i)),
            pl.BlockSpec((TOP_K, tt), lambda i: (0, i)),
            pl.BlockSpec((N_EXPERTS, 1), lambda i: (0, 0)),
        ],
        out_shape=[
            jax.ShapeDtypeStruct((TOP_K, m), i32),
            jax.ShapeDtypeStruct((TOP_K, m), f32),
            jax.ShapeDtypeStruct((TOP_K, m), i32),
            jax.ShapeDtypeStruct((N_EXPERTS, 1), i32),
        ],
        scratch_shapes=[pltpu.VMEM((N_EXPERTS, 1), f32)],
        compiler_params=pltpu.CompilerParams(
            dimension_semantics=("arbitrary",), vmem_limit_bytes=VMEM_LIMIT),
        name="router",
    )(h1, w_router_t, b_router_col, tri)


FILL_SIZES = (128, 64, 32, 16, 8, 4, 2, 1)
PACKED_WIDTH = D_MODEL // 2
ROW_TILE = PACKED_WIDTH // LANES
HIGH_HALF = 0xFFFF0000
assert ROW_TILE == SUBLANES


def _row_copy(src_ref, src_row, dst_ref, dst_row, sem):
    src = src_ref.at[pl.ds(pl.multiple_of(src_row * ROW_TILE, ROW_TILE), ROW_TILE)]
    dst = dst_ref.at[pl.ds(pl.multiple_of(dst_row * ROW_TILE, ROW_TILE), ROW_TILE)]
    return pltpu.make_async_copy(src, dst, sem)


def _pack_bf16_pairs(v):
    w = v.shape[1] // 2
    bits = lambda t: lax.bitcast_convert_type(t.astype(bf16).astype(f32), jnp.uint32)
    return (bits(v[:, :w]) >> 16) | (bits(v[:, w:]) & jnp.uint32(HIGH_HALF))


def _unpack_bf16_pairs(p):
    lo = lax.bitcast_convert_type(p << 16, f32)
    hi = lax.bitcast_convert_type(p & jnp.uint32(HIGH_HALF), f32)
    return jnp.concatenate([lo, hi], axis=1)


def _store_tile_rows(ref, packed):
    n = packed.shape[0]
    for j in range(ROW_TILE):
        ref[pl.ds(j, n, stride=ROW_TILE), :] = packed[:, j * LANES:(j + 1) * LANES]


def _load_tile_rows(ref, n):
    return jnp.concatenate([ref[pl.ds(j, n, stride=ROW_TILE), :] for j in range(ROW_TILE)], axis=1)


def _dispatch_body(dest_ref, cnt_ref, pstart_ref, padded_ref, h_ref, xs_ref, packed, zbuf, sem, zsem, *, tt):
    i = pl.program_id(0)
    _store_tile_rows(packed, _pack_bf16_pairs(h_ref[...]))

    @pl.when(i == 0)
    def _():
        zbuf[...] = jnp.zeros_like(zbuf)

        def fill(e, wait):
            cnt = cnt_ref[e]
            first = pstart_ref[e] + cnt
            filler = padded_ref[e] - cnt
            for size in FILL_SIZES:
                @pl.when((filler & size) != 0)
                def _():
                    start = pl.multiple_of((first + (filler & (-2 * size))) * ROW_TILE, ROW_TILE)
                    cp = pltpu.make_async_copy(zbuf.at[pl.ds(0, size * ROW_TILE)],
                                               xs_ref.at[pl.ds(start, size * ROW_TILE)], zsem)
                    if wait:
                        cp.wait()
                    else:
                        cp.start()

        def start_fill(e, c):
            fill(e, False)
            return c

        def wait_fill(e, c):
            fill(e, True)
            return c

        lax.fori_loop(0, N_EXPERTS, start_fill, 0)
        lax.fori_loop(0, N_EXPERTS, wait_fill, 0)

    def issue(t, c):
        for k in range(TOP_K):
            _row_copy(packed, t, xs_ref, dest_ref[0, k, t], sem).start(priority=k % 2)
        return c

    def drain(t, c):
        for k in range(TOP_K):
            _row_copy(packed, 0, xs_ref, 0, sem).wait()
        return c

    lax.fori_loop(0, tt, issue, 0, unroll=8)
    lax.fori_loop(0, tt, drain, 0, unroll=8)


def _dispatch(dest3, counts, pstart, padded, h1, *, rows, tt):
    m = h1.shape[0]
    smem = pl.BlockSpec(memory_space=pltpu.SMEM)
    return pl.pallas_call(
        functools.partial(_dispatch_body, tt=tt),
        grid=(m // tt,),
        in_specs=[
            pl.BlockSpec((1, TOP_K, tt), lambda i: (i, 0, 0), memory_space=pltpu.SMEM),
            smem, smem, smem,
            pl.BlockSpec((tt, D_MODEL), lambda i: (i, 0)),
        ],
        out_specs=pl.BlockSpec(memory_space=pl.ANY),
        out_shape=jax.ShapeDtypeStruct((rows * ROW_TILE, LANES), jnp.uint32),
        scratch_shapes=[
            pltpu.VMEM((tt * ROW_TILE, LANES), jnp.uint32),
            pltpu.VMEM((FILL_SIZES[0] * ROW_TILE, LANES), jnp.uint32),
            pltpu.SemaphoreType.DMA(()),
            pltpu.SemaphoreType.DMA(()),
        ],
        compiler_params=pltpu.CompilerParams(
            dimension_semantics=("arbitrary",), vmem_limit_bytes=VMEM_LIMIT, has_side_effects=True),
        name="moe_dispatch",
    )(dest3, counts, pstart, padded, h1)


def _experts_body(blk_e_ref, nused_ref, first_ref, slot_ref, next_e_ref, x_ref, wg_hbm, wu_hbm, wd_hbm, y_ref,
                  rawg, rawu, rawd, wgb, wub, wdb, sem):
    i = pl.program_id(0)

    def weight_copies(e, s):
        return (pltpu.make_async_copy(wg_hbm.at[e], rawg.at[s], sem.at[s, 0]),
                pltpu.make_async_copy(wu_hbm.at[e], rawu.at[s], sem.at[s, 1]),
                pltpu.make_async_copy(wd_hbm.at[e], rawd.at[s], sem.at[s, 2]))

    @pl.when(i < nused_ref[0])
    def _():
        e = blk_e_ref[i]
        s = slot_ref[i]

        @pl.when(first_ref[i] == 1)
        def _():
            @pl.when(i == 0)
            def _():
                for cp in weight_copies(e, s):
                    cp.start()

            for cp in weight_copies(e, s):
                cp.wait()
            wgb[...] = rawg[s].astype(bf16)
            wub[...] = rawu[s].astype(bf16)
            wdb[...] = rawd[s].astype(bf16)

            @pl.when(next_e_ref[i] >= 0)
            def _():
                for cp in weight_copies(next_e_ref[i], 1 - s):
                    cp.start()

        x = _unpack_bf16_pairs(_load_tile_rows(x_ref, MOE_BLOCK)).astype(bf16)
        hg = jnp.dot(x, wgb[...], preferred_element_type=f32)
        hu = jnp.dot(x, wub[...], preferred_element_type=f32)
        hb = (_silu(hg) * hu).astype(bf16)
        _store_tile_rows(y_ref, _pack_bf16_pairs(jnp.dot(hb, wdb[...], preferred_element_type=f32)))


def _experts(blk_e, nused, first, slot, next_e, xs, w_gate, w_up, w_down):
    rows = xs.shape[0] // ROW_TILE
    nblk = rows // MOE_BLOCK
    blk = lambda i, be, nu, *_: (jnp.minimum(i, nu[0] - 1), 0)
    hbm = pl.BlockSpec(memory_space=pl.ANY)
    grid_spec = pltpu.PrefetchScalarGridSpec(
        num_scalar_prefetch=5,
        grid=(nblk,),
        in_specs=[pl.BlockSpec((MOE_BLOCK * ROW_TILE, LANES), blk), hbm, hbm, hbm],
        out_specs=pl.BlockSpec((MOE_BLOCK * ROW_TILE, LANES), blk),
        scratch_shapes=[
            pltpu.VMEM((2, D_MODEL, EXPERT_DIM), f32),
            pltpu.VMEM((2, D_MODEL, EXPERT_DIM), f32),
            pltpu.VMEM((2, EXPERT_DIM, D_MODEL), f32),
            pltpu.VMEM((D_MODEL, EXPERT_DIM), bf16),
            pltpu.VMEM((D_MODEL, EXPERT_DIM), bf16),
            pltpu.VMEM((EXPERT_DIM, D_MODEL), bf16),
            pltpu.SemaphoreType.DMA((2, 3)),
        ],
    )
    return pl.pallas_call(
        _experts_body,
        grid_spec=grid_spec,
        out_shape=jax.ShapeDtypeStruct((rows * ROW_TILE, LANES), jnp.uint32),
        compiler_params=pltpu.CompilerParams(
            dimension_semantics=("arbitrary",), vmem_limit_bytes=VMEM_LIMIT),
        name="routed_experts",
    )(blk_e, nused, first, slot, next_e, xs, w_gate, w_up, w_down)


def _combine_body(d0_ref, d1_ref, d2_ref, h_ref, gate_ref, wsg_ref, wsu_ref, wsd_ref, g2_ref, b2_ref, ys_ref,
                  o_ref, buf0, buf1, sem, *, tm):
    i = pl.program_id(0)

    def issue(d_ref, buf, s):
        for t in range(tm):
            for k in range(TOP_K):
                _row_copy(ys_ref, d_ref[0, k, t], buf.at[k], t, sem.at[s]).start(priority=k % 2)

    def drain(buf, s):
        def body(t, c):
            for k in range(TOP_K):
                _row_copy(ys_ref, 0, buf.at[k], 0, sem.at[s]).wait()
            return c
        lax.fori_loop(0, tm, body, 0, unroll=8)

    def tile(rows, buf):
        h = h_ref[rows]
        hb = h.astype(bf16)
        sg = jnp.dot(hb, wsg_ref[...], preferred_element_type=f32)
        su = jnp.dot(hb, wsu_ref[...], preferred_element_type=f32)
        ffn = jnp.dot((_silu(sg) * su).astype(bf16), wsd_ref[...], preferred_element_type=f32)
        gate = gate_ref[rows]
        for k in range(TOP_K):
            ffn = ffn + gate[:, k:k + 1] * _unpack_bf16_pairs(_load_tile_rows(buf.at[k], tm))
        o_ref[rows] = _layer_norm(DEEPNORM_ALPHA * h + ffn, g2_ref[...], b2_ref[...])

    @pl.when(i == 0)
    def _():
        issue(d0_ref, buf0, 0)

    drain(buf0, 0)
    issue(d1_ref, buf1, 1)
    tile(slice(0, tm), buf0)
    drain(buf1, 1)
    issue(d2_ref, buf0, 0)
    tile(slice(tm, 2 * tm), buf1)

    @pl.when(i == pl.num_programs(0) - 1)
    def _():
        drain(buf0, 0)


def _combine(dest3, h1, ys, gate_tok, wsg, wsu, wsd, g2, b2, *, tm):
    m = h1.shape[0]
    nt = m // tm
    dest_tile = lambda f: pl.BlockSpec((1, TOP_K, tm), lambda i: (f(i), 0, 0), memory_space=pltpu.SMEM)
    return pl.pallas_call(
        functools.partial(_combine_body, tm=tm),
        grid=(nt // 2,),
        in_specs=[
            dest_tile(lambda i: 2 * i),
            dest_tile(lambda i: 2 * i + 1),
            dest_tile(lambda i: jnp.minimum(2 * i + 2, nt - 1)),
            pl.BlockSpec((2 * tm, D_MODEL), lambda i: (i, 0)),
            pl.BlockSpec((2 * tm, TOP_K), lambda i: (i, 0)),
            pl.BlockSpec((D_MODEL, EXPERT_DIM), lambda i: (0, 0)),
            pl.BlockSpec((D_MODEL, EXPERT_DIM), lambda i: (0, 0)),
            pl.BlockSpec((EXPERT_DIM, D_MODEL), lambda i: (0, 0)),
            pl.BlockSpec((1, D_MODEL), lambda i: (0, 0)),
            pl.BlockSpec((1, D_MODEL), lambda i: (0, 0)),
            pl.BlockSpec(memory_space=pl.ANY),
        ],
        out_specs=pl.BlockSpec((2 * tm, D_MODEL), lambda i: (i, 0)),
        out_shape=jax.ShapeDtypeStruct((m, D_MODEL), f32),
        scratch_shapes=[
            pltpu.VMEM((TOP_K, tm * ROW_TILE, LANES), jnp.uint32),
            pltpu.VMEM((TOP_K, tm * ROW_TILE, LANES), jnp.uint32),
            pltpu.SemaphoreType.DMA((2,)),
        ],
        compiler_params=pltpu.CompilerParams(
            dimension_semantics=("arbitrary",), vmem_limit_bytes=VMEM_LIMIT),
        name="combine_ln2",
    )(dest3, dest3, dest3, h1, gate_tok, wsg, wsu, wsd, g2, b2, ys)


def _rope_tables(pos, width):
    half = ROPE_DIM // 2
    inv_freq = jnp.power(ROPE_THETA, -jnp.arange(0, ROPE_DIM, 2, dtype=f32) / ROPE_DIM)
    ang = pos.astype(f32)[:, None] * inv_freq[None, :]
    cos, sin = jnp.cos(ang), jnp.sin(ang)
    n = pos.shape[0]
    pad = jnp.zeros((n, HEAD_DIM - ROPE_DIM), f32)
    zero = jnp.zeros((n, half), f32)
    c = jnp.concatenate([cos, cos, pad + 1.0], axis=1)
    s1 = jnp.concatenate([-sin, zero, pad], axis=1)
    s2 = jnp.concatenate([zero, sin, pad], axis=1)
    tab = jnp.stack([c, s1, s2])
    return jnp.tile(tab, (1, 1, width // HEAD_DIM))


def _group_lanes(v):
    v = v.reshape(SSM_GROUPS, SSM_HEADS_PER_GROUP)
    return jnp.pad(v, ((0, 0), (0, LANES - SSM_HEADS_PER_GROUP))).reshape(1, DT_WIDTH)


def kernel(x, meta_tokens, ln_in_g, ln_in_b, w_in, conv_w, conv_b, dt_bias, a_log, d_skip, ssm_norm_g, att_norm_g, attn_sinks, w_out, ln1_g, ln1_b, w_router, b_router, w_gate, w_up, w_down, ws_gate, ws_up, ws_down, ln2_g, ln2_b):
    batch, seq, d = x.shape
    assert d == D_MODEL and seq % BLOCK == 0 and meta_tokens.shape == (N_META, D_MODEL)
    assert w_in.shape[0] == 1, "single layer"
    n_tok = batch * seq
    nblk = seq // BLOCK
    row = lambda v: v.reshape(1, -1).astype(f32)

    x2d = x.reshape(n_tok, D_MODEL)
    gi, bi = row(ln_in_g), row(ln_in_b)
    w_main = w_in[0].astype(bf16)
    w_dt = jnp.pad(w_in[0, :, MAIN_WIDTH:], ((0, 0), (0, LANES - SSM_HEADS))).astype(bf16)

    proj, dtx = _ln_inproj(x2d, gi, bi, w_main, w_dt, tm=1024, tn=512)
    proj_m, dtx_m = _ln_inproj(meta_tokens.astype(f32), gi, bi, w_main, w_dt, tm=N_META, tn=512)
    proj_m = jnp.pad(proj_m, ((BLOCK - N_META, 0), (0, 0)))
    dtx_m = jnp.pad(dtx_m, ((CHUNK - N_META, 0), (0, 0)))

    gw = ATT_GROUP * HEAD_DIM
    tab = _rope_tables(N_META + jnp.arange(seq), gw)
    tab_meta = _rope_tables(jnp.arange(BLOCK), gw)
    kv_meta = jnp.roll(proj_m[:, ATT_WIDTH:ATT_WIDTH + 2 * KV_WIDTH], N_META, axis=0)
    att = _attention(proj, kv_meta, tab, tab_meta, attn_sinks[0].astype(f32), row(att_norm_g[0]),
                     batch=batch, nblk=nblk)

    conv_w0 = conv_w[0].astype(f32)
    conv_b0 = row(conv_b[0])
    dtb_x = _group_lanes(dt_bias[0].astype(f32))
    alog_x = _group_lanes(a_log[0].astype(f32))
    dskip_ch = jnp.repeat(d_skip[0].astype(f32), HEAD_DIM).reshape(1, SSM_WIDTH)
    ng = row(ssm_norm_g[0])
    zeros_halo = jnp.zeros((8, CONV_CH), f32)
    zeros_state = jnp.zeros((SSM_GROUPS, SSM_STATE, SSM_GROUP_WIDTH), f32)
    _, s_meta = _ssd(proj_m, dtx_m, zeros_halo, zeros_state, conv_w0, conv_b0, dtb_x, alog_x, dskip_ch, ng,
                     batch=1, nchunk=1, vstart=CHUNK - N_META, emit_state=True)
    halo = proj_m[CHUNK - 8:, XBC_COL:].astype(f32)
    (ssm,) = _ssd(proj, dtx, halo, s_meta, conv_w0, conv_b0, dtb_x, alog_x, dskip_ch, ng,
                  batch=batch, nchunk=nblk, vstart=0, emit_state=False)

    h1 = _outproj(att, ssm, w_out[0].astype(bf16), x2d, gi, bi, row(ln1_g[0]), row(ln1_b[0]),
                  tm=512, tk=1024)

    tt = 512
    tri = (jnp.arange(tt)[:, None] <= jnp.arange(tt)[None, :]).astype(bf16)
    eidx, gate, rank, counts = _router(h1, w_router[0].T.astype(f32), b_router[0].reshape(N_EXPERTS, 1).astype(f32),
                                       tri, tt=tt)
    counts = counts.reshape(N_EXPERTS)
    padded = (counts + MOE_BLOCK - 1) // MOE_BLOCK * MOE_BLOCK
    pend = jnp.cumsum(padded)
    pstart = pend - padded
    first_row = jnp.sum(jnp.where(eidx[..., None] == jnp.arange(N_EXPERTS, dtype=i32), pstart.astype(i32), 0), axis=-1)
    dest = first_row + rank
    n_blocks = n_tok * TOP_K // MOE_BLOCK + N_EXPERTS
    blk_first = jnp.arange(n_blocks, dtype=i32) * MOE_BLOCK
    blk_e = jnp.minimum(jnp.sum(pend[None, :] <= blk_first[:, None], axis=1), N_EXPERTS - 1).astype(i32)
    nused = (pend[-1] // MOE_BLOCK).astype(i32).reshape(1)
    rows = n_blocks * MOE_BLOCK
    tiles = lambda t: dest.reshape(TOP_K, n_tok // t, t).transpose(1, 0, 2)

    td, tc = 256, 128
    xs = _dispatch(tiles(td), counts, pstart.astype(i32), padded.astype(i32), h1, rows=rows, tt=td)
    eids = jnp.arange(N_EXPERTS, dtype=i32)
    nonempty = counts > 0
    ordinal = jnp.cumsum(nonempty.astype(i32)) - nonempty.astype(i32)
    later = (eids[None, :] > eids[:, None]) & nonempty[None, :]
    next_nonempty = jnp.min(jnp.where(later, eids[None, :], N_EXPERTS), axis=1)
    next_nonempty = jnp.where(next_nonempty == N_EXPERTS, -1, next_nonempty).astype(i32)
    onehot_e = blk_e[:, None] == eids[None, :]
    pick = lambda table: jnp.sum(jnp.where(onehot_e, table[None, :], 0), axis=1).astype(i32)
    first = jnp.concatenate([jnp.ones((1,), i32), (blk_e[1:] != blk_e[:-1]).astype(i32)])
    ys = _experts(blk_e, nused, first, pick(ordinal) % 2, pick(next_nonempty), xs, w_gate[0], w_up[0], w_down[0])
    out = _combine(tiles(tc), h1, ys, gate.T, ws_gate[0].astype(bf16), ws_up[0].astype(bf16),
                   ws_down[0].astype(bf16), row(ln2_g[0]), row(ln2_b[0]), tm=tc)
    return out.reshape(batch, seq, D_MODEL)
```

```python
import functools
import math

import jax
import jax.numpy as jnp
from jax import lax
from jax.experimental import pallas as pl
from jax.experimental.pallas import tpu as pltpu

f32 = jnp.float32
bf16 = jnp.bfloat16
i32 = jnp.int32

D_MODEL = 2048
N_META = 16
HEAD_DIM = 64
ATT_HEADS = 32
ATT_KV_HEADS = 4
ATT_GROUP = ATT_HEADS // ATT_KV_HEADS
ATT_WIDTH = 2048
KV_WIDTH = 256
BLOCK = 128
ROPE_DIM = 16
ROPE_THETA = 500000.0
SSM_WIDTH = 4096
SSM_HEADS = 64
SSM_GROUPS = 8
SSM_HEADS_PER_GROUP = SSM_HEADS // SSM_GROUPS
SSM_STATE = 128
SSM_GROUP_WIDTH = SSM_WIDTH // SSM_GROUPS
CONV_WIDTH = 4
CHUNK = 128
CONV_CH = SSM_WIDTH + 2 * SSM_GROUPS * SSM_STATE
MAIN_WIDTH = ATT_WIDTH + 2 * KV_WIDTH + SSM_WIDTH + CONV_CH
Z_COL = ATT_WIDTH + 2 * KV_WIDTH
XBC_COL = Z_COL + SSM_WIDTH
N_EXPERTS = 64
EXPERT_DIM = 512
TOP_K = 8
N_EXPERT_GROUPS = 8
EXPERTS_PER_GROUP = N_EXPERTS // N_EXPERT_GROUPS
TOPK_GROUPS = 4
ROUTED_SCALE = 2.5
MOE_BLOCK = 256
DEEPNORM_ALPHA = 2.0 ** 0.25
LN_EPS = 1e-5
RMS_EPS = 1e-6
NEG_INF = -1e30
LANES = 128
SUBLANES = 8
DT_WIDTH = SSM_GROUPS * LANES
VMEM_LIMIT = 56 * 1024 * 1024


def _layer_norm(x, g, b):
    mu = jnp.mean(x, axis=-1, keepdims=True)
    xc = x - mu
    var = jnp.mean(xc * xc, axis=-1, keepdims=True)
    return xc * lax.rsqrt(var + LN_EPS) * g + b


def _silu(x):
    return x * (1.0 / (1.0 + jnp.exp(-x)))


def _softplus(x):
    return jnp.maximum(x, 0.0) + jnp.log1p(jnp.exp(-jnp.abs(x)))


def _ln_inproj_body(x_ref, g_ref, b_ref, w_ref, wdt_ref, o_ref, dt_ref, h_scr):
    @pl.when(pl.program_id(1) == 0)
    def _():
        h = _layer_norm(x_ref[...], g_ref[...], b_ref[...]).astype(bf16)
        h_scr[...] = h
        dt_ref[...] = jnp.dot(h, wdt_ref[...], preferred_element_type=f32)

    o_ref[...] = jnp.dot(h_scr[...], w_ref[...], preferred_element_type=f32).astype(bf16)


def _ln_inproj(x2d, g, b, w_main, w_dt, *, tm, tn):
    m = x2d.shape[0]
    return pl.pallas_call(
        _ln_inproj_body,
        grid=(m // tm, MAIN_WIDTH // tn),
        in_specs=[
            pl.BlockSpec((tm, D_MODEL), lambda i, j: (i, 0)),
            pl.BlockSpec((1, D_MODEL), lambda i, j: (0, 0)),
            pl.BlockSpec((1, D_MODEL), lambda i, j: (0, 0)),
            pl.BlockSpec((D_MODEL, tn), lambda i, j: (0, j)),
            pl.BlockSpec((D_MODEL, LANES), lambda i, j: (0, 0)),
        ],
        out_specs=[
            pl.BlockSpec((tm, tn), lambda i, j: (i, j)),
            pl.BlockSpec((tm, LANES), lambda i, j: (i, 0)),
        ],
        out_shape=[
            jax.ShapeDtypeStruct((m, MAIN_WIDTH), bf16),
            jax.ShapeDtypeStruct((m, LANES), f32),
        ],
        scratch_shapes=[pltpu.VMEM((tm, D_MODEL), bf16)],
        compiler_params=pltpu.CompilerParams(
            dimension_semantics=("arbitrary", "arbitrary"), vmem_limit_bytes=VMEM_LIMIT),
        name="ln_inproj",
    )(x2d, g, b, w_main, w_dt)


def _rotate(t, tab):
    w = t.shape[-1]
    half = ROPE_DIM // 2
    return (t * tab[0]
            + pltpu.roll(t, w - half, 1) * tab[1]
            + pltpu.roll(t, half, 1) * tab[2])


def _attention_body(sink_ref, q_ref, kvc_ref, kvp_ref, kvm_ref, tabc_ref, tabp_ref, tabm_ref,
                    g_ref, o_ref):
    j = pl.program_id(1)
    tabc = tabc_ref[...]
    kc = _rotate(kvc_ref[:, :KV_WIDTH].astype(f32), tabc[:, :, :KV_WIDTH]).astype(bf16)
    kp = _rotate(kvp_ref[:, :KV_WIDTH].astype(f32), tabp_ref[:, :, :KV_WIDTH]).astype(bf16)
    km = _rotate(kvm_ref[:, :KV_WIDTH].astype(f32), tabm_ref[:, :, :KV_WIDTH]).astype(bf16)
    v_all = jnp.concatenate([kvp_ref[:, KV_WIDTH:], kvc_ref[:, KV_WIDTH:], kvm_ref[:, KV_WIDTH:]], axis=0)

    rows = ATT_GROUP * BLOCK
    r = lax.broadcasted_iota(i32, (rows, BLOCK), 0) % BLOCK
    c = lax.broadcasted_iota(i32, (rows, BLOCK), 1)
    lower = c <= r
    prev_ok = (c > r) & (j > 0)
    meta_ok = c < N_META
    head_of_row = lax.broadcasted_iota(i32, (rows, 1), 0) // BLOCK
    nt_dims = (((1,), (1,)), ((), ()))

    outs = []
    for g in range(ATT_KV_HEADS):
        gw = ATT_GROUP * HEAD_DIM
        hs = slice(g * HEAD_DIM, (g + 1) * HEAD_DIM)
        qg = _rotate(q_ref[:, g * gw:(g + 1) * gw].astype(f32), tabc) * (HEAD_DIM ** -0.5)
        qs = jnp.concatenate([qg[:, h * HEAD_DIM:(h + 1) * HEAD_DIM] for h in range(ATT_GROUP)],
                             axis=0).astype(bf16)
        s_cur = lax.dot_general(qs, kc[:, hs], nt_dims, preferred_element_type=f32)
        s_prev = lax.dot_general(qs, kp[:, hs], nt_dims, preferred_element_type=f32)
        s_meta = lax.dot_general(qs, km[:, hs], nt_dims, preferred_element_type=f32)
        s_band = jnp.where(lower, s_cur, jnp.where(prev_ok, s_prev, NEG_INF))
        s_meta = jnp.where(meta_ok, s_meta, NEG_INF)
        sink = jnp.zeros((rows, 1), f32)
        for h in range(ATT_GROUP):
            sink = jnp.where(head_of_row == h, sink_ref[g * ATT_GROUP + h], sink)
        m = jnp.maximum(jnp.max(jnp.maximum(s_band, s_meta), axis=1, keepdims=True), sink)
        p_band = jnp.exp(s_band - m)
        p_meta = jnp.exp(s_meta - m)
        denom = jnp.sum(p_band + p_meta, axis=1, keepdims=True) + jnp.exp(sink - m)
        p_cur = jnp.where(lower, p_band, 0.0)
        p = jnp.concatenate([p_band - p_cur, p_cur, p_meta], axis=1).astype(bf16)
        o = jnp.dot(p, v_all[:, hs], preferred_element_type=f32) / denom
        outs.append(jnp.concatenate([o[h * BLOCK:(h + 1) * BLOCK] for h in range(ATT_GROUP)], axis=1))
    att = jnp.concatenate(outs, axis=1)
    att = att * lax.rsqrt(jnp.mean(att * att, axis=-1, keepdims=True) + RMS_EPS) * g_ref[...]
    o_ref[...] = att.astype(bf16)


def _attention(proj, kv_meta, tab, tab_meta, sinks, att_g, *, batch, nblk):
    m = proj.shape[0]
    kvb = ATT_WIDTH // (2 * KV_WIDTH)
    gw = ATT_GROUP * HEAD_DIM
    return pl.pallas_call(
        _attention_body,
        grid=(batch, nblk),
        in_specs=[
            pl.BlockSpec(memory_space=pltpu.SMEM),
            pl.BlockSpec((BLOCK, ATT_WIDTH), lambda b, j: (b * nblk + j, 0)),
            pl.BlockSpec((BLOCK, 2 * KV_WIDTH), lambda b, j: (b * nblk + j, kvb)),
            pl.BlockSpec((BLOCK, 2 * KV_WIDTH), lambda b, j: (b * nblk + jnp.maximum(j - 1, 0), kvb)),
            pl.BlockSpec((BLOCK, 2 * KV_WIDTH), lambda b, j: (0, 0)),
            pl.BlockSpec((3, BLOCK, gw), lambda b, j: (0, j, 0)),
            pl.BlockSpec((3, BLOCK, gw), lambda b, j: (0, jnp.maximum(j - 1, 0), 0)),
            pl.BlockSpec((3, BLOCK, gw), lambda b, j: (0, 0, 0)),
            pl.BlockSpec((1, ATT_WIDTH), lambda b, j: (0, 0)),
        ],
        out_specs=pl.BlockSpec((BLOCK, ATT_WIDTH), lambda b, j: (b * nblk + j, 0)),
        out_shape=jax.ShapeDtypeStruct((m, ATT_WIDTH), bf16),
        compiler_params=pltpu.CompilerParams(
            dimension_semantics=("arbitrary", "arbitrary"), vmem_limit_bytes=VMEM_LIMIT),
        name="swa_attention",
    )(sinks, proj, proj, proj, kv_meta, tab, tab, tab_meta, att_g)


def _cumsum_rows(x):
    n = x.shape[0]
    row = lax.broadcasted_iota(i32, x.shape, 0)
    k = 1
    while k < n:
        x = x + jnp.where(row >= k, pltpu.roll(x, k, 0), 0.0)
        k *= 2
    return x


def _expand_heads(col):
    lane = lax.broadcasted_iota(i32, col.shape, 1) // HEAD_DIM
    per_tile = LANES // HEAD_DIM
    return jnp.concatenate(
        [jnp.take_along_axis(col, lane + per_tile * j, axis=1) for j in range(SSM_GROUP_WIDTH // LANES)], axis=1)


def _ssd_body(proj_ref, dt_ref, halo_ref, s0_ref, cw_ref, cb_ref, dtb_ref, alog_ref, dskip_ref, ng_ref,
              *rest, vstart, emit_state):
    if emit_state:
        y_ref, sout_ref, state, halo, work = rest
    else:
        y_ref, state, halo, work = rest

    @pl.when(pl.program_id(1) == 0)
    def _():
        state[...] = s0_ref[...]
        halo[...] = halo_ref[...]

    if vstart == 0:
        keep_valid = lambda v: v
    else:
        valid = lax.broadcasted_iota(i32, (CHUNK, 1), 0) >= vstart
        keep_valid = lambda v: jnp.where(valid, v, 0.0)
    lane = lax.broadcasted_iota(i32, (CHUNK, LANES), 1)
    causal =(lax.broadcasted_iota(i32, (CHUNK, CHUNK), 0) >= lax.broadcasted_iota(i32, (CHUNK, CHUNK), 1))

    def conv(col, width):
        u = proj_ref[:, pl.ds(pl.multiple_of(XBC_COL + col, LANES), width)].astype(f32)
        cs = pl.ds(pl.multiple_of(col, LANES), width)
        work[0:8, 0:width] = halo[:, cs]
        work[8:8 + CHUNK, 0:width] = u
        w = cw_ref[:, cs]
        acc = cb_ref[:, cs] + u * w[CONV_WIDTH - 1:CONV_WIDTH]
        for jj in range(CONV_WIDTH - 1):
            acc = acc + work[5 + jj:5 + jj + CHUNK, 0:width] * w[jj:jj + 1]
        halo[:, cs] = u[CHUNK - 8:]
        return keep_valid(_silu(acc))

    def group(g, carry):
        gs = pl.ds(pl.multiple_of(g * SSM_GROUP_WIDTH, SSM_GROUP_WIDTH), SSM_GROUP_WIDTH)
        gl = pl.ds(pl.multiple_of(g * LANES, LANES), LANES)
        xs = conv(g * SSM_GROUP_WIDTH, SSM_GROUP_WIDTH)
        bm = conv(SSM_WIDTH + g * SSM_STATE, SSM_STATE)
        cm = conv(SSM_WIDTH + SSM_GROUPS * SSM_STATE + g * SSM_STATE, SSM_STATE)

        dt_raw = jnp.take_along_axis(dt_ref[...], (lane + g * SSM_HEADS_PER_GROUP) % LANES, axis=1)
        dt = keep_valid(_softplus(dt_raw + dtb_ref[:, gl]))
        a_cs = _cumsum_rows(dt * -jnp.exp(alog_ref[:, gl]))
        a_cs_t = a_cs.T
        a_last = a_cs[CHUNK - 1:CHUNK]

        xdt = xs * _expand_heads(dt)
        bmb = bm.astype(bf16)
        cmb = cm.astype(bf16)
        cb = lax.dot_general(cmb, bmb, (((1,), (1,)), ((), ())), preferred_element_type=f32)
        xdtb = xdt.astype(bf16)
        ys = []
        for i in range(SSM_HEADS_PER_GROUP):
            seg = a_cs[:, i:i + 1] - a_cs_t[i:i + 1, :]
            lmat = (cb * jnp.exp(jnp.where(causal, seg, -jnp.inf))).astype(bf16)
            ys.append(jnp.dot(lmat, xdtb[:, i * HEAD_DIM:(i + 1) * HEAD_DIM], preferred_element_type=f32))
        y = jnp.concatenate(ys, axis=1)

        st = state[g]
        y_off = jnp.dot(cmb, st.astype(bf16), preferred_element_type=f32)
        y = y + y_off * _expand_heads(jnp.exp(a_cs))

        x_end = (xdt * _expand_heads(jnp.exp(a_last - a_cs))).astype(bf16)
        upd = lax.dot_general(bmb, x_end, (((0,), (0,)), ((), ())), preferred_element_type=f32)
        keep = _expand_heads(jnp.exp(a_cs[CHUNK - SUBLANES:]))[SUBLANES - 1:]
        state[g] = st * keep + upd

        y = y + dskip_ref[:, gs] * xs
        y = y * _silu(proj_ref[:, pl.ds(pl.multiple_of(Z_COL + g * SSM_GROUP_WIDTH, SSM_GROUP_WIDTH),
                                       SSM_GROUP_WIDTH)].astype(f32))
        y = y * lax.rsqrt(jnp.mean(y * y, axis=-1, keepdims=True) + RMS_EPS) * ng_ref[:, gs]
        y_ref[:, gs] = y.astype(bf16)
        return carry

    lax.fori_loop(0, SSM_GROUPS, group, 0)
    if emit_state:
        sout_ref[...] = state[...]


def _ssd(proj, dtx, halo, s0, conv_w, conv_b, dt_bias_x, a_log_x, d_skip_ch, norm_g,
         *, batch, nchunk, vstart, emit_state):
    m = proj.shape[0]
    full = lambda shape: pl.BlockSpec(shape, lambda b, c: (0,) * len(shape))
    state_shape = (SSM_GROUPS, SSM_STATE, SSM_GROUP_WIDTH)
    in_specs = [
        pl.BlockSpec((CHUNK, MAIN_WIDTH), lambda b, c: (b * nchunk + c, 0)),
        pl.BlockSpec((CHUNK, LANES), lambda b, c: (b * nchunk + c, 0)),
        full((8, CONV_CH)),
        full(state_shape),
        full((CONV_WIDTH, CONV_CH)),
        full((1, CONV_CH)),
        full((1, DT_WIDTH)),
        full((1, DT_WIDTH)),
        full((1, SSM_WIDTH)),
        full((1, SSM_WIDTH)),
    ]
    out_specs = [pl.BlockSpec((CHUNK, SSM_WIDTH), lambda b, c: (b * nchunk + c, 0))]
    out_shape = [jax.ShapeDtypeStruct((m, SSM_WIDTH), bf16)]
    if emit_state:
        out_specs.append(full(state_shape))
        out_shape.append(jax.ShapeDtypeStruct(state_shape, f32))
    return pl.pallas_call(
        functools.partial(_ssd_body, vstart=vstart, emit_state=emit_state),
        grid=(batch, nchunk),
        in_specs=in_specs,
        out_specs=out_specs,
        out_shape=out_shape,
        scratch_shapes=[
            pltpu.VMEM(state_shape, f32),
            pltpu.VMEM((8, CONV_CH), f32),
            pltpu.VMEM((8 + CHUNK, SSM_GROUP_WIDTH), f32),
        ],
        compiler_params=pltpu.CompilerParams(
            dimension_semantics=("arbitrary", "arbitrary"), vmem_limit_bytes=VMEM_LIMIT),
        name="ssd_state" if emit_state else "ssd_scan",
    )(proj, dtx, halo, s0, conv_w, conv_b, dt_bias_x, a_log_x, d_skip_ch, norm_g)


def _outproj_body(att_ref, ssm_ref, w_ref, x_ref, gi_ref, bi_ref, g1_ref, b1_ref, h_ref, acc,
                  *, n_att_k):
    k = pl.program_id(1)

    @pl.when(k == 0)
    def _():
        acc[...] = jnp.zeros_like(acc)

    @pl.when(k < n_att_k)
    def _():
        acc[...] += jnp.dot(att_ref[...], w_ref[...], preferred_element_type=f32)

    @pl.when(k >= n_att_k)
    def _():
        acc[...] += jnp.dot(ssm_ref[...], w_ref[...], preferred_element_type=f32)

    @pl.when(k == pl.num_programs(1) - 1)
    def _():
        h0 = _layer_norm(x_ref[...], gi_ref[...], bi_ref[...])
        h_ref[...] = _layer_norm(DEEPNORM_ALPHA * h0 + acc[...], g1_ref[...], b1_ref[...])


def _outproj(att, ssm, w_out, x2d, gi, bi, g1, b1, *, tm, tk):
    m = att.shape[0]
    n_att_k = ATT_WIDTH // tk
    nk = (ATT_WIDTH + SSM_WIDTH) // tk
    return pl.pallas_call(
        functools.partial(_outproj_body, n_att_k=n_att_k),
        grid=(m // tm, nk),
        in_specs=[
            pl.BlockSpec((tm, tk), lambda i, k: (i, jnp.minimum(k, n_att_k - 1))),
            pl.BlockSpec((tm, tk), lambda i, k: (i, jnp.maximum(k - n_att_k, 0))),
            pl.BlockSpec((tk, D_MODEL), lambda i, k: (k, 0)),
            pl.BlockSpec((tm, D_MODEL), lambda i, k: (i, 0)),
            pl.BlockSpec((1, D_MODEL), lambda i, k: (0, 0)),
            pl.BlockSpec((1, D_MODEL), lambda i, k: (0, 0)),
            pl.BlockSpec((1, D_MODEL), lambda i, k: (0, 0)),
            pl.BlockSpec((1, D_MODEL), lambda i, k: (0, 0)),
        ],
        out_specs=pl.BlockSpec((tm, D_MODEL), lambda i, k: (i, 0)),
        out_shape=jax.ShapeDtypeStruct((m, D_MODEL), f32),
        scratch_shapes=[pltpu.VMEM((tm, D_MODEL), f32)],
        compiler_params=pltpu.CompilerParams(
            dimension_semantics=("arbitrary", "arbitrary"), vmem_limit_bytes=VMEM_LIMIT),
        name="outproj_ln1",
    )(att, ssm, w_out, x2d, gi, bi, g1, b1)


def _max01(v):
    return jnp.max(jnp.max(v, axis=1, keepdims=True), axis=0, keepdims=True)


def _router_body(h_ref, wr_ref, br_ref, tri_ref, eidx_ref, gate_ref, rank_ref, cnt_ref, running):
    tt = h_ref.shape[0]
    shape3 = (N_EXPERT_GROUPS, EXPERTS_PER_GROUP, tt)

    @pl.when(pl.program_id(0) == 0)
    def _():
        running[...] = jnp.zeros_like(running)

    logits = lax.dot_general(wr_ref[...], h_ref[...], (((1,), (1,)), ((), ())),
                             precision=lax.Precision.HIGHEST, preferred_element_type=f32)
    scores = (1.0 / (1.0 + jnp.exp(-logits)))
    sel3 = (scores + br_ref[...]).reshape(shape3)
    scores3 = scores.reshape(shape3)
    within = lax.broadcasted_iota(i32, shape3, 1).astype(f32)
    m1 = jnp.max(sel3, axis=1, keepdims=True)
    i1 = jnp.min(jnp.where(sel3 == m1, within, float(EXPERTS_PER_GROUP)), axis=1, keepdims=True)
    m2 = jnp.max(jnp.where(within == i1, -jnp.inf, sel3), axis=1, keepdims=True)
    gs = m1 + m2
    giota = lax.broadcasted_iota(i32, gs.shape, 0).astype(f32)
    gmask = jnp.zeros(gs.shape, f32)
    for _ in range(TOPK_GROUPS):
        gm = jnp.max(gs, axis=0, keepdims=True)
        gi = jnp.min(jnp.where(gs == gm, giota, float(N_EXPERT_GROUPS)), axis=0, keepdims=True)
        hit = giota == gi
        gmask = jnp.where(hit, 1.0, gmask)
        gs = jnp.where(hit, -jnp.inf, gs)
    selm = jnp.where(gmask > 0.0, sel3, -jnp.inf)
    eiota = (lax.broadcasted_iota(i32, shape3, 0) * EXPERTS_PER_GROUP
             + lax.broadcasted_iota(i32, shape3, 1)).astype(f32)
    eidx, gates, hits = [], [], []
    member = jnp.zeros(shape3, f32)
    gsum = jnp.zeros((1, 1, tt), f32)
    for _ in range(TOP_K):
        m = _max01(selm)
        ei = -_max01(-jnp.where(selm == m, eiota, float(N_EXPERTS)))
        hit = eiota == ei
        gk = jnp.sum(jnp.sum(jnp.where(hit, scores3, 0.0), axis=1, keepdims=True), axis=0, keepdims=True)
        eidx.append(ei)
        gates.append(gk)
        hits.append(hit)
        gsum = gsum + gk
        member = jnp.where(hit, 1.0, member)
        selm = jnp.where(hit, -jnp.inf, selm)
    member2 = member.reshape(N_EXPERTS, tt)
    incl = jnp.dot(member2.astype(bf16), tri_ref[...], preferred_element_type=f32)
    base = (running[...] + (incl - member2)).reshape(shape3)
    for k in range(TOP_K):
        rk = jnp.sum(jnp.sum(jnp.where(hits[k], base, 0.0), axis=1, keepdims=True), axis=0, keepdims=True)
        eidx_ref[k:k + 1, :] = eidx[k].reshape(1, tt).astype(i32)
        gate_ref[k:k + 1, :] = (gates[k] / gsum * ROUTED_SCALE).reshape(1, tt)
        rank_ref[k:k + 1, :] = rk.reshape(1, tt).astype(i32)
    total = running[...] + incl[:, tt - 1:tt]
    running[...] = total
    cnt_ref[...] = total.astype(i32)


def _router(h1, w_router_t, b_router_col, tri, *, tt):
    m = h1.shape[0]
    return pl.pallas_call(
        _router_body,
        grid=(m // tt,),
        in_specs=[
            pl.BlockSpec((tt, D_MODEL), lambda i: (i, 0)),
            pl.BlockSpec((N_EXPERTS, D_MODEL), lambda i: (0, 0)),
            pl.BlockSpec((N_EXPERTS, 1), lambda i: (0, 0)),
            pl.BlockSpec((tt, tt), lambda i: (0, 0)),
        ],
        out_specs=[
            pl.BlockSpec((TOP_K, tt), lambda i: (0, i)),
            pl.BlockSpec((TOP_K, tt), lambda i: (0, i)),
            pl.BlockSpec((TOP_K, tt), lambda i: (0, i)),
            pl.BlockSpec((N_EXPERTS, 1), lambda i: (0, 0)),
        ],
        out_shape=[
            jax.ShapeDtypeStruct((TOP_K, m), i32),
            jax.ShapeDtypeStruct((TOP_K, m), f32),
            jax.ShapeDtypeStruct((TOP_K, m), i32),
            jax.ShapeDtypeStruct((N_EXPERTS, 1), i32),
        ],
        scratch_shapes=[pltpu.VMEM((N_EXPERTS, 1), f32)],
        compiler_params=pltpu.CompilerParams(
            dimension_semantics=("arbitrary",), vmem_limit_bytes=VMEM_LIMIT),
        name="router",
    )(h1, w_router_t, b_router_col, tri)


FILL_SIZES = (128, 64, 32, 16, 8, 4, 2, 1)
PACKED_WIDTH = D_MODEL // 2
ROW_TILE = PACKED_WIDTH // LANES
HIGH_HALF = 0xFFFF0000
assert ROW_TILE == SUBLANES


def _row_copy(src_ref, src_row, dst_ref, dst_row, sem):
    src = src_ref.at[pl.ds(pl.multiple_of(src_row * ROW_TILE, ROW_TILE), ROW_TILE)]
    dst = dst_ref.at[pl.ds(pl.multiple_of(dst_row * ROW_TILE, ROW_TILE), ROW_TILE)]
    return pltpu.make_async_copy(src, dst, sem)


def _pack_bf16_pairs(v):
    w = v.shape[1] // 2
    bits = lambda t: lax.bitcast_convert_type(t.astype(bf16).astype(f32), jnp.uint32)
    return (bits(v[:, :w]) >> 16) | (bits(v[:, w:]) & jnp.uint32(HIGH_HALF))


def _unpack_bf16_pairs(p):
    lo = lax.bitcast_convert_type(p << 16, f32)
    hi = lax.bitcast_convert_type(p & jnp.uint32(HIGH_HALF), f32)
    return jnp.concatenate([lo, hi], axis=1)


def _store_tile_rows(ref, packed):
    n = packed.shape[0]
    for j in range(ROW_TILE):
        ref[pl.ds(j, n, stride=ROW_TILE), :] = packed[:, j * LANES:(j + 1) * LANES]


def _load_tile_rows(ref, n):
    return jnp.concatenate([ref[pl.ds(j, n, stride=ROW_TILE), :] for j in range(ROW_TILE)], axis=1)


def _dispatch_body(dest_ref, cnt_ref, pstart_ref, padded_ref, h_ref, xs_ref, packed, zbuf, sem, zsem, *, tt):
    i = pl.program_id(0)
    _store_tile_rows(packed, _pack_bf16_pairs(h_ref[...]))

    @pl.when(i == 0)
    def _():
        zbuf[...] = jnp.zeros_like(zbuf)

        def fill(e, wait):
            cnt = cnt_ref[e]
            first = pstart_ref[e] + cnt
            filler = padded_ref[e] - cnt
            for size in FILL_SIZES:
                @pl.when((filler & size) != 0)
                def _():
                    start = pl.multiple_of((first + (filler & (-2 * size))) * ROW_TILE, ROW_TILE)
                    cp = pltpu.make_async_copy(zbuf.at[pl.ds(0, size * ROW_TILE)],
                                               xs_ref.at[pl.ds(start, size * ROW_TILE)], zsem)
                    if wait:
                        cp.wait()
                    else:
                        cp.start()

        def start_fill(e, c):
            fill(e, False)
            return c

        def wait_fill(e, c):
            fill(e, True)
            return c

        lax.fori_loop(0, N_EXPERTS, start_fill, 0)
        lax.fori_loop(0, N_EXPERTS, wait_fill, 0)

    def issue(t, c):
        for k in range(TOP_K):
            _row_copy(packed, t, xs_ref, dest_ref[0, k, t], sem).start(priority=k % 2)
        return c

    def drain(t, c):
        for k in range(TOP_K):
            _row_copy(packed, 0, xs_ref, 0, sem).wait()
        return c

    lax.fori_loop(0, tt, issue, 0, unroll=8)
    lax.fori_loop(0, tt, drain, 0, unroll=8)


def _dispatch(dest3, counts, pstart, padded, h1, *, rows, tt):
    m = h1.shape[0]
    smem = pl.BlockSpec(memory_space=pltpu.SMEM)
    return pl.pallas_call(
        functools.partial(_dispatch_body, tt=tt),
        grid=(m // tt,),
        in_specs=[
            pl.BlockSpec((1, TOP_K, tt), lambda i: (i, 0, 0), memory_space=pltpu.SMEM),
            smem, smem, smem,
            pl.BlockSpec((tt, D_MODEL), lambda i: (i, 0)),
        ],
        out_specs=pl.BlockSpec(memory_space=pl.ANY),
        out_shape=jax.ShapeDtypeStruct((rows * ROW_TILE, LANES), jnp.uint32),
        scratch_shapes=[
            pltpu.VMEM((tt * ROW_TILE, LANES), jnp.uint32),
            pltpu.VMEM((FILL_SIZES[0] * ROW_TILE, LANES), jnp.uint32),
            pltpu.SemaphoreType.DMA(()),
            pltpu.SemaphoreType.DMA(()),
        ],
        compiler_params=pltpu.CompilerParams(
            dimension_semantics=("arbitrary",), vmem_limit_bytes=VMEM_LIMIT, has_side_effects=True),
        name="moe_dispatch",
    )(dest3, counts, pstart, padded, h1)


def _experts_body(blk_e_ref, nused_ref, first_ref, slot_ref, next_e_ref, x_ref, wg_hbm, wu_hbm, wd_hbm, y_ref,
                  rawg, rawu, rawd, wgb, wub, wdb, sem):
    i = pl.program_id(0)

    def weight_copies(e, s):
        return (pltpu.make_async_copy(wg_hbm.at[e], rawg.at[s], sem.at[s, 0]),
                pltpu.make_async_copy(wu_hbm.at[e], rawu.at[s], sem.at[s, 1]),
                pltpu.make_async_copy(wd_hbm.at[e], rawd.at[s], sem.at[s, 2]))

    @pl.when(i < nused_ref[0])
    def _():
        e = blk_e_ref[i]
        s = slot_ref[i]

        @pl.when(first_ref[i] == 1)
        def _():
            @pl.when(i == 0)
            def _():
                for cp in weight_copies(e, s):
                    cp.start()

            for cp in weight_copies(e, s):
                cp.wait()
            wgb[...] = rawg[s].astype(bf16)
            wub[...] = rawu[s].astype(bf16)
            wdb[...] = rawd[s].astype(bf16)

            @pl.when(next_e_ref[i] >= 0)
            def _():
                for cp in weight_copies(next_e_ref[i], 1 - s):
                    cp.start()

        x = _unpack_bf16_pairs(_load_tile_rows(x_ref, MOE_BLOCK)).astype(bf16)
        hg = jnp.dot(x, wgb[...], preferred_element_type=f32)
        hu = jnp.dot(x, wub[...], preferred_element_type=f32)
        hb = (_silu(hg) * hu).astype(bf16)
        _store_tile_rows(y_ref, _pack_bf16_pairs(jnp.dot(hb, wdb[...], preferred_element_type=f32)))


def _experts(blk_e, nused, first, slot, next_e, xs, w_gate, w_up, w_down):
    rows = xs.shape[0] // ROW_TILE
    nblk = rows // MOE_BLOCK
    blk = lambda i, be, nu, *_: (jnp.minimum(i, nu[0] - 1), 0)
    hbm = pl.BlockSpec(memory_space=pl.ANY)
    grid_spec = pltpu.PrefetchScalarGridSpec(
        num_scalar_prefetch=5,
        grid=(nblk,),
        in_specs=[pl.BlockSpec((MOE_BLOCK * ROW_TILE, LANES), blk), hbm, hbm, hbm],
        out_specs=pl.BlockSpec((MOE_BLOCK * ROW_TILE, LANES), blk),
        scratch_shapes=[
            pltpu.VMEM((2, D_MODEL, EXPERT_DIM), f32),
            pltpu.VMEM((2, D_MODEL, EXPERT_DIM), f32),
            pltpu.VMEM((2, EXPERT_DIM, D_MODEL), f32),
            pltpu.VMEM((D_MODEL, EXPERT_DIM), bf16),
            pltpu.VMEM((D_MODEL, EXPERT_DIM), bf16),
            pltpu.VMEM((EXPERT_DIM, D_MODEL), bf16),
            pltpu.SemaphoreType.DMA((2, 3)),
        ],
    )
    return pl.pallas_call(
        _experts_body,
        grid_spec=grid_spec,
        out_shape=jax.ShapeDtypeStruct((rows * ROW_TILE, LANES), jnp.uint32),
        compiler_params=pltpu.CompilerParams(
            dimension_semantics=("arbitrary",), vmem_limit_bytes=VMEM_LIMIT),
        name="routed_experts",
    )(blk_e, nused, first, slot, next_e, xs, w_gate, w_up, w_down)


def _combine_body(d0_ref, d1_ref, d2_ref, h_ref, gate_ref, wsg_ref, wsu_ref, wsd_ref, g2_ref, b2_ref, ys_ref,
                  o_ref, buf0, buf1, sem, *, tm):
    i = pl.program_id(0)

    def issue(d_ref, buf, s):
        for t in range(tm):
            for k in range(TOP_K):
                _row_copy(ys_ref, d_ref[0, k, t], buf.at[k], t, sem.at[s]).start(priority=k % 2)

    def drain(buf, s):
        def body(t, c):
            for k in range(TOP_K):
                _row_copy(ys_ref, 0, buf.at[k], 0, sem.at[s]).wait()
            return c
        lax.fori_loop(0, tm, body, 0, unroll=8)

    def tile(rows, buf):
        h = h_ref[rows]
        hb = h.astype(bf16)
        sg = jnp.dot(hb, wsg_ref[...], preferred_element_type=f32)
        su = jnp.dot(hb, wsu_ref[...], preferred_element_type=f32)
        ffn = jnp.dot((_silu(sg) * su).astype(bf16), wsd_ref[...], preferred_element_type=f32)
        gate = gate_ref[rows]
        for k in range(TOP_K):
            ffn = ffn + gate[:, k:k + 1] * _unpack_bf16_pairs(_load_tile_rows(buf.at[k], tm))
        o_ref[rows] = _layer_norm(DEEPNORM_ALPHA * h + ffn, g2_ref[...], b2_ref[...])

    @pl.when(i == 0)
    def _():
        issue(d0_ref, buf0, 0)

    drain(buf0, 0)
    issue(d1_ref, buf1, 1)
    tile(slice(0, tm), buf0)
    drain(buf1, 1)
    issue(d2_ref, buf0, 0)
    tile(slice(tm, 2 * tm), buf1)

    @pl.when(i == pl.num_programs(0) - 1)
    def _():
        drain(buf0, 0)


def _combine(dest3, h1, ys, gate_tok, wsg, wsu, wsd, g2, b2, *, tm):
    m = h1.shape[0]
    nt = m // tm
    dest_tile = lambda f: pl.BlockSpec((1, TOP_K, tm), lambda i: (f(i), 0, 0), memory_space=pltpu.SMEM)
    return pl.pallas_call(
        functools.partial(_combine_body, tm=tm),
        grid=(nt // 2,),
        in_specs=[
            dest_tile(lambda i: 2 * i),
            dest_tile(lambda i: 2 * i + 1),
            dest_tile(lambda i: jnp.minimum(2 * i + 2, nt - 1)),
            pl.BlockSpec((2 * tm, D_MODEL), lambda i: (i, 0)),
            pl.BlockSpec((2 * tm, TOP_K), lambda i: (i, 0)),
            pl.BlockSpec((D_MODEL, EXPERT_DIM), lambda i: (0, 0)),
            pl.BlockSpec((D_MODEL, EXPERT_DIM), lambda i: (0, 0)),
            pl.BlockSpec((EXPERT_DIM, D_MODEL), lambda i: (0, 0)),
            pl.BlockSpec((1, D_MODEL), lambda i: (0, 0)),
            pl.BlockSpec((1, D_MODEL), lambda i: (0, 0)),
            pl.BlockSpec(memory_space=pl.ANY),
        ],
        out_specs=pl.BlockSpec((2 * tm, D_MODEL), lambda i: (i, 0)),
        out_shape=jax.ShapeDtypeStruct((m, D_MODEL), f32),
        scratch_shapes=[
            pltpu.VMEM((TOP_K, tm * ROW_TILE, LANES), jnp.uint32),
            pltpu.VMEM((TOP_K, tm * ROW_TILE, LANES), jnp.uint32),
            pltpu.SemaphoreType.DMA((2,)),
        ],
        compiler_params=pltpu.CompilerParams(
            dimension_semantics=("arbitrary",), vmem_limit_bytes=VMEM_LIMIT),
        name="combine_ln2",
    )(dest3, dest3, dest3, h1, gate_tok, wsg, wsu, wsd, g2, b2, ys)


def _rope_tables(pos, width):
    half = ROPE_DIM // 2
    inv_freq = jnp.power(ROPE_THETA, -jnp.arange(0, ROPE_DIM, 2, dtype=f32) / ROPE_DIM)
    ang = pos.astype(f32)[:, None] * inv_freq[None, :]
    cos, sin = jnp.cos(ang), jnp.sin(ang)
    n = pos.shape[0]
    pad = jnp.zeros((n, HEAD_DIM - ROPE_DIM), f32)
    zero = jnp.zeros((n, half), f32)
    c = jnp.concatenate([cos, cos, pad + 1.0], axis=1)
    s1 = jnp.concatenate([-sin, zero, pad], axis=1)
    s2 = jnp.concatenate([zero, sin, pad], axis=1)
    tab = jnp.stack([c, s1, s2])
    return jnp.tile(tab, (1, 1, width // HEAD_DIM))


def _group_lanes(v):
    v = v.reshape(SSM_GROUPS, SSM_HEADS_PER_GROUP)
    return jnp.pad(v, ((0, 0), (0, LANES - SSM_HEADS_PER_GROUP))).reshape(1, DT_WIDTH)


def kernel(x, meta_tokens, ln_in_g, ln_in_b, w_in, conv_w, conv_b, dt_bias, a_log, d_skip, ssm_norm_g, att_norm_g, attn_sinks, w_out, ln1_g, ln1_b, w_router, b_router, w_gate, w_up, w_down, ws_gate, ws_up, ws_down, ln2_g, ln2_b):
    batch, seq, d = x.shape
    assert d == D_MODEL and seq % BLOCK == 0 and meta_tokens.shape == (N_META, D_MODEL)
    assert w_in.shape[0] == 1, "single layer"
    n_tok = batch * seq
    nblk = seq // BLOCK
    row = lambda v: v.reshape(1, -1).astype(f32)

    x2d = x.reshape(n_tok, D_MODEL)
    gi, bi = row(ln_in_g), row(ln_in_b)
    w_main = w_in[0].astype(bf16)
    w_dt = jnp.pad(w_in[0, :, MAIN_WIDTH:], ((0, 0), (0, LANES - SSM_HEADS))).astype(bf16)

    proj, dtx = _ln_inproj(x2d, gi, bi, w_main, w_dt, tm=1024, tn=1280)
    proj_m, dtx_m = _ln_inproj(meta_tokens.astype(f32), gi, bi, w_main, w_dt, tm=N_META, tn=512)
    proj_m = jnp.pad(proj_m, ((BLOCK - N_META, 0), (0, 0)))
    dtx_m = jnp.pad(dtx_m, ((CHUNK - N_META, 0), (0, 0)))

    gw = ATT_GROUP * HEAD_DIM
    tab = _rope_tables(N_META + jnp.arange(seq), gw)
    tab_meta = _rope_tables(jnp.arange(BLOCK), gw)
    kv_meta = jnp.roll(proj_m[:, ATT_WIDTH:ATT_WIDTH + 2 * KV_WIDTH], N_META, axis=0)
    att = _attention(proj, kv_meta, tab, tab_meta, attn_sinks[0].astype(f32), row(att_norm_g[0]),
                     batch=batch, nblk=nblk)

    conv_w0 = conv_w[0].astype(f32)
    conv_b0 = row(conv_b[0])
    dtb_x = _group_lanes(dt_bias[0].astype(f32))
    alog_x = _group_lanes(a_log[0].astype(f32))
    dskip_ch = jnp.repeat(d_skip[0].astype(f32), HEAD_DIM).reshape(1, SSM_WIDTH)
    ng = row(ssm_norm_g[0])
    zeros_halo = jnp.zeros((8, CONV_CH), f32)
    zeros_state = jnp.zeros((SSM_GROUPS, SSM_STATE, SSM_GROUP_WIDTH), f32)
    _, s_meta = _ssd(proj_m, dtx_m, zeros_halo, zeros_state, conv_w0, conv_b0, dtb_x, alog_x, dskip_ch, ng,
                     batch=1, nchunk=1, vstart=CHUNK - N_META, emit_state=True)
    halo = proj_m[CHUNK - 8:, XBC_COL:].astype(f32)
    (ssm,) = _ssd(proj, dtx, halo, s_meta, conv_w0, conv_b0, dtb_x, alog_x, dskip_ch, ng,
                  batch=batch, nchunk=nblk, vstart=0, emit_state=False)

    h1 = _outproj(att, ssm, w_out[0].astype(bf16), x2d, gi, bi, row(ln1_g[0]), row(ln1_b[0]),
                  tm=512, tk=1024)

    tt = 512
    tri = (jnp.arange(tt)[:, None] <= jnp.arange(tt)[None, :]).astype(bf16)
    eidx, gate, rank, counts = _router(h1, w_router[0].T.astype(f32), b_router[0].reshape(N_EXPERTS, 1).astype(f32),
                                       tri, tt=tt)
    counts = counts.reshape(N_EXPERTS)
    padded = (counts + MOE_BLOCK - 1) // MOE_BLOCK * MOE_BLOCK
    pend = jnp.cumsum(padded)
    pstart = pend - padded
    first_row = jnp.sum(jnp.where(eidx[..., None] == jnp.arange(N_EXPERTS, dtype=i32), pstart.astype(i32), 0), axis=-1)
    dest = first_row + rank
    n_blocks = n_tok * TOP_K // MOE_BLOCK + N_EXPERTS
    blk_first = jnp.arange(n_blocks, dtype=i32) * MOE_BLOCK
    blk_e = jnp.minimum(jnp.sum(pend[None, :] <= blk_first[:, None], axis=1), N_EXPERTS - 1).astype(i32)
    nused = (pend[-1] // MOE_BLOCK).astype(i32).reshape(1)
    rows = n_blocks * MOE_BLOCK
    tiles = lambda t: dest.reshape(TOP_K, n_tok // t, t).transpose(1, 0, 2)

    td, tc = 512, 128
    xs = _dispatch(tiles(td), counts, pstart.astype(i32), padded.astype(i32), h1, rows=rows, tt=td)
    eids = jnp.arange(N_EXPERTS, dtype=i32)
    nonempty = counts > 0
    ordinal = jnp.cumsum(nonempty.astype(i32)) - nonempty.astype(i32)
    later = (eids[None, :] > eids[:, None]) & nonempty[None, :]
    next_nonempty = jnp.min(jnp.where(later, eids[None, :], N_EXPERTS), axis=1)
    next_nonempty = jnp.where(next_nonempty == N_EXPERTS, -1, next_nonempty).astype(i32)
    onehot_e = blk_e[:, None] == eids[None, :]
    pick = lambda table: jnp.sum(jnp.where(onehot_e, table[None, :], 0), axis=1).astype(i32)
    first = jnp.concatenate([jnp.ones((1,), i32), (blk_e[1:] != blk_e[:-1]).astype(i32)])
    ys = _experts(blk_e, nused, first, pick(ordinal) % 2, pick(next_nonempty), xs, w_gate[0], w_up[0], w_down[0])
    out = _combine(tiles(tc), h1, ys, gate.T, ws_gate[0].astype(bf16), ws_up[0].astype(bf16),
                   ws_down[0].astype(bf16), row(ln2_g[0]), row(ln2_b[0]), tm=tc)
    return out.reshape(batch, seq, D_MODEL)
```

```python
import functools
import math

import jax
import jax.numpy as jnp
from jax import lax
from jax.experimental import pallas as pl
from jax.experimental.pallas import tpu as pltpu

f32 = jnp.float32
bf16 = jnp.bfloat16
i32 = jnp.int32

D_MODEL = 2048
N_META = 16
HEAD_DIM = 64
ATT_HEADS = 32
ATT_KV_HEADS = 4
ATT_GROUP = ATT_HEADS // ATT_KV_HEADS
ATT_WIDTH = 2048
KV_WIDTH = 256
BLOCK = 128
ROPE_DIM = 16
ROPE_THETA = 500000.0
SSM_WIDTH = 4096
SSM_HEADS = 64
SSM_GROUPS = 8
SSM_HEADS_PER_GROUP = SSM_HEADS // SSM_GROUPS
SSM_STATE = 128
SSM_GROUP_WIDTH = SSM_WIDTH // SSM_GROUPS
CONV_WIDTH = 4
CHUNK = 128
CONV_CH = SSM_WIDTH + 2 * SSM_GROUPS * SSM_STATE
MAIN_WIDTH = ATT_WIDTH + 2 * KV_WIDTH + SSM_WIDTH + CONV_CH
Z_COL = ATT_WIDTH + 2 * KV_WIDTH
XBC_COL = Z_COL + SSM_WIDTH
N_EXPERTS = 64
EXPERT_DIM = 512
TOP_K = 8
N_EXPERT_GROUPS = 8
EXPERTS_PER_GROUP = N_EXPERTS // N_EXPERT_GROUPS
TOPK_GROUPS = 4
ROUTED_SCALE = 2.5
MOE_BLOCK = 256
DEEPNORM_ALPHA = 2.0 ** 0.25
LN_EPS = 1e-5
RMS_EPS = 1e-6
NEG_INF = -1e30
LANES = 128
SUBLANES = 8
DT_WIDTH = SSM_GROUPS * LANES
VMEM_LIMIT = 56 * 1024 * 1024


def _layer_norm(x, g, b):
    mu = jnp.mean(x, axis=-1, keepdims=True)
    xc = x - mu
    var = jnp.mean(xc * xc, axis=-1, keepdims=True)
    return xc * lax.rsqrt(var + LN_EPS) * g + b


def _silu(x):
    return x * (1.0 / (1.0 + jnp.exp(-x)))


def _softplus(x):
    return jnp.maximum(x, 0.0) + jnp.log1p(jnp.exp(-jnp.abs(x)))


def _ln_inproj_body(x_ref, g_ref, b_ref, w_ref, wdt_ref, o_ref, dt_ref, h_scr):
    @pl.when(pl.program_id(1) == 0)
    def _():
        h = _layer_norm(x_ref[...], g_ref[...], b_ref[...]).astype(bf16)
        h_scr[...] = h
        dt_ref[...] = jnp.dot(h, wdt_ref[...], preferred_element_type=f32)

    o_ref[...] = jnp.dot(h_scr[...], w_ref[...], preferred_element_type=f32).astype(bf16)


def _ln_inproj(x2d, g, b, w_main, w_dt, *, tm, tn):
    m = x2d.shape[0]
    return pl.pallas_call(
        _ln_inproj_body,
        grid=(m // tm, MAIN_WIDTH // tn),
        in_specs=[
            pl.BlockSpec((tm, D_MODEL), lambda i, j: (i, 0)),
            pl.BlockSpec((1, D_MODEL), lambda i, j: (0, 0)),
            pl.BlockSpec((1, D_MODEL), lambda i, j: (0, 0)),
            pl.BlockSpec((D_MODEL, tn), lambda i, j: (0, j)),
            pl.BlockSpec((D_MODEL, LANES), lambda i, j: (0, 0)),
        ],
        out_specs=[
            pl.BlockSpec((tm, tn), lambda i, j: (i, j)),
            pl.BlockSpec((tm, LANES), lambda i, j: (i, 0)),
        ],
        out_shape=[
            jax.ShapeDtypeStruct((m, MAIN_WIDTH), bf16),
            jax.ShapeDtypeStruct((m, LANES), f32),
        ],
        scratch_shapes=[pltpu.VMEM((tm, D_MODEL), bf16)],
        compiler_params=pltpu.CompilerParams(
            dimension_semantics=("arbitrary", "arbitrary"), vmem_limit_bytes=VMEM_LIMIT),
        name="ln_inproj",
    )(x2d, g, b, w_main, w_dt)


def _rotate(t, tab):
    w = t.shape[-1]
    half = ROPE_DIM // 2
    return (t * tab[0]
            + pltpu.roll(t, w - half, 1) * tab[1]
            + pltpu.roll(t, half, 1) * tab[2])


def _attention_body(sink_ref, q_ref, kvc_ref, kvp_ref, kvm_ref, tabc_ref, tabp_ref, tabm_ref,
                    g_ref, o_ref):
    j = pl.program_id(1)
    tabc = tabc_ref[...]
    kc = _rotate(kvc_ref[:, :KV_WIDTH].astype(f32), tabc[:, :, :KV_WIDTH]).astype(bf16)
    kp = _rotate(kvp_ref[:, :KV_WIDTH].astype(f32), tabp_ref[:, :, :KV_WIDTH]).astype(bf16)
    km = _rotate(kvm_ref[:, :KV_WIDTH].astype(f32), tabm_ref[:, :, :KV_WIDTH]).astype(bf16)
    v_all = jnp.concatenate([kvp_ref[:, KV_WIDTH:], kvc_ref[:, KV_WIDTH:], kvm_ref[:, KV_WIDTH:]], axis=0)

    rows = ATT_GROUP * BLOCK
    r = lax.broadcasted_iota(i32, (rows, BLOCK), 0) % BLOCK
    c = lax.broadcasted_iota(i32, (rows, BLOCK), 1)
    lower = c <= r
    prev_ok = (c > r) & (j > 0)
    meta_ok = c < N_META
    head_of_row = lax.broadcasted_iota(i32, (rows, 1), 0) // BLOCK
    nt_dims = (((1,), (1,)), ((), ()))

    outs = []
    for g in range(ATT_KV_HEADS):
        gw = ATT_GROUP * HEAD_DIM
        hs = slice(g * HEAD_DIM, (g + 1) * HEAD_DIM)
        qg = _rotate(q_ref[:, g * gw:(g + 1) * gw].astype(f32), tabc) * (HEAD_DIM ** -0.5)
        qs = jnp.concatenate([qg[:, h * HEAD_DIM:(h + 1) * HEAD_DIM] for h in range(ATT_GROUP)],
                             axis=0).astype(bf16)
        s_cur = lax.dot_general(qs, kc[:, hs], nt_dims, preferred_element_type=f32)
        s_prev = lax.dot_general(qs, kp[:, hs], nt_dims, preferred_element_type=f32)
        s_meta = lax.dot_general(qs, km[:, hs], nt_dims, preferred_element_type=f32)
        s_band = jnp.where(lower, s_cur, jnp.where(prev_ok, s_prev, NEG_INF))
        s_meta = jnp.where(meta_ok, s_meta, NEG_INF)
        sink = jnp.zeros((rows, 1), f32)
        for h in range(ATT_GROUP):
            sink = jnp.where(head_of_row == h, sink_ref[g * ATT_GROUP + h], sink)
        m = jnp.maximum(jnp.max(jnp.maximum(s_band, s_meta), axis=1, keepdims=True), sink)
        p_band = jnp.exp(s_band - m)
        p_meta = jnp.exp(s_meta - m)
        denom = jnp.sum(p_band + p_meta, axis=1, keepdims=True) + jnp.exp(sink - m)
        p_cur = jnp.where(lower, p_band, 0.0)
        p = jnp.concatenate([p_band - p_cur, p_cur, p_meta], axis=1).astype(bf16)
        o = jnp.dot(p, v_all[:, hs], preferred_element_type=f32) / denom
        outs.append(jnp.concatenate([o[h * BLOCK:(h + 1) * BLOCK] for h in range(ATT_GROUP)], axis=1))
    att = jnp.concatenate(outs, axis=1)
    att = att * lax.rsqrt(jnp.mean(att * att, axis=-1, keepdims=True) + RMS_EPS) * g_ref[...]
    o_ref[...] = att.astype(bf16)


def _attention(proj, kv_meta, tab, tab_meta, sinks, att_g, *, batch, nblk):
    m = proj.shape[0]
    kvb = ATT_WIDTH // (2 * KV_WIDTH)
    gw = ATT_GROUP * HEAD_DIM
    return pl.pallas_call(
        _attention_body,
        grid=(batch, nblk),
        in_specs=[
            pl.BlockSpec(memory_space=pltpu.SMEM),
            pl.BlockSpec((BLOCK, ATT_WIDTH), lambda b, j: (b * nblk + j, 0)),
            pl.BlockSpec((BLOCK, 2 * KV_WIDTH), lambda b, j: (b * nblk + j, kvb)),
            pl.BlockSpec((BLOCK, 2 * KV_WIDTH), lambda b, j: (b * nblk + jnp.maximum(j - 1, 0), kvb)),
            pl.BlockSpec((BLOCK, 2 * KV_WIDTH), lambda b, j: (0, 0)),
            pl.BlockSpec((3, BLOCK, gw), lambda b, j: (0, j, 0)),
            pl.BlockSpec((3, BLOCK, gw), lambda b, j: (0, jnp.maximum(j - 1, 0), 0)),
            pl.BlockSpec((3, BLOCK, gw), lambda b, j: (0, 0, 0)),
            pl.BlockSpec((1, ATT_WIDTH), lambda b, j: (0, 0)),
        ],
        out_specs=pl.BlockSpec((BLOCK, ATT_WIDTH), lambda b, j: (b * nblk + j, 0)),
        out_shape=jax.ShapeDtypeStruct((m, ATT_WIDTH), bf16),
        compiler_params=pltpu.CompilerParams(
            dimension_semantics=("arbitrary", "arbitrary"), vmem_limit_bytes=VMEM_LIMIT),
        name="swa_attention",
    )(sinks, proj, proj, proj, kv_meta, tab, tab, tab_meta, att_g)


def _cumsum_rows(x):
    n = x.shape[0]
    row = lax.broadcasted_iota(i32, x.shape, 0)
    k = 1
    while k < n:
        x = x + jnp.where(row >= k, pltpu.roll(x, k, 0), 0.0)
        k *= 2
    return x


def _expand_heads(col):
    lane = lax.broadcasted_iota(i32, col.shape, 1) // HEAD_DIM
    per_tile = LANES // HEAD_DIM
    return jnp.concatenate(
        [jnp.take_along_axis(col, lane + per_tile * j, axis=1) for j in range(SSM_GROUP_WIDTH // LANES)], axis=1)


def _ssd_body(proj_ref, dt_ref, halo_ref, s0_ref, cw_ref, cb_ref, dtb_ref, alog_ref, dskip_ref, ng_ref,
              *rest, vstart, emit_state):
    if emit_state:
        y_ref, sout_ref, state, halo, work = rest
    else:
        y_ref, state, halo, work = rest

    @pl.when(pl.program_id(1) == 0)
    def _():
        state[...] = s0_ref[...]
        halo[...] = halo_ref[...]

    if vstart == 0:
        keep_valid = lambda v: v
    else:
        valid = lax.broadcasted_iota(i32, (CHUNK, 1), 0) >= vstart
        keep_valid = lambda v: jnp.where(valid, v, 0.0)
    lane = lax.broadcasted_iota(i32, (CHUNK, LANES), 1)
    causal =(lax.broadcasted_iota(i32, (CHUNK, CHUNK), 0) >= lax.broadcasted_iota(i32, (CHUNK, CHUNK), 1))

    def conv(col, width):
        u = proj_ref[:, pl.ds(pl.multiple_of(XBC_COL + col, LANES), width)].astype(f32)
        cs = pl.ds(pl.multiple_of(col, LANES), width)
        work[0:8, 0:width] = halo[:, cs]
        work[8:8 + CHUNK, 0:width] = u
        w = cw_ref[:, cs]
        acc = cb_ref[:, cs] + u * w[CONV_WIDTH - 1:CONV_WIDTH]
        for jj in range(CONV_WIDTH - 1):
            acc = acc + work[5 + jj:5 + jj + CHUNK, 0:width] * w[jj:jj + 1]
        halo[:, cs] = u[CHUNK - 8:]
        return keep_valid(_silu(acc))

    def group(g, carry):
        gs = pl.ds(pl.multiple_of(g * SSM_GROUP_WIDTH, SSM_GROUP_WIDTH), SSM_GROUP_WIDTH)
        gl = pl.ds(pl.multiple_of(g * LANES, LANES), LANES)
        xs = conv(g * SSM_GROUP_WIDTH, SSM_GROUP_WIDTH)
        bm = conv(SSM_WIDTH + g * SSM_STATE, SSM_STATE)
        cm = conv(SSM_WIDTH + SSM_GROUPS * SSM_STATE + g * SSM_STATE, SSM_STATE)

        dt_raw = jnp.take_along_axis(dt_ref[...], (lane + g * SSM_HEADS_PER_GROUP) % LANES, axis=1)
        dt = keep_valid(_softplus(dt_raw + dtb_ref[:, gl]))
        a_cs = _cumsum_rows(dt * -jnp.exp(alog_ref[:, gl]))
        a_cs_t = a_cs.T
        a_last = a_cs[CHUNK - 1:CHUNK]

        xdt = xs * _expand_heads(dt)
        bmb = bm.astype(bf16)
        cmb = cm.astype(bf16)
        cb = lax.dot_general(cmb, bmb, (((1,), (1,)), ((), ())), preferred_element_type=f32)
        xdtb = xdt.astype(bf16)
        ys = []
        for i in range(SSM_HEADS_PER_GROUP):
            seg = a_cs[:, i:i + 1] - a_cs_t[i:i + 1, :]
            lmat = (cb * jnp.exp(jnp.where(causal, seg, -jnp.inf))).astype(bf16)
            ys.append(jnp.dot(lmat, xdtb[:, i * HEAD_DIM:(i + 1) * HEAD_DIM], preferred_element_type=f32))
        y = jnp.concatenate(ys, axis=1)

        st = state[g]
        y_off = jnp.dot(cmb, st.astype(bf16), preferred_element_type=f32)
        y = y + y_off * _expand_heads(jnp.exp(a_cs))

        x_end = (xdt * _expand_heads(jnp.exp(a_last - a_cs))).astype(bf16)
        upd = lax.dot_general(bmb, x_end, (((0,), (0,)), ((), ())), preferred_element_type=f32)
        keep = _expand_heads(jnp.exp(a_cs[CHUNK - SUBLANES:]))[SUBLANES - 1:]
        state[g] = st * keep + upd

        y = y + dskip_ref[:, gs] * xs
        y = y * _silu(proj_ref[:, pl.ds(pl.multiple_of(Z_COL + g * SSM_GROUP_WIDTH, SSM_GROUP_WIDTH),
                                       SSM_GROUP_WIDTH)].astype(f32))
        y = y * lax.rsqrt(jnp.mean(y * y, axis=-1, keepdims=True) + RMS_EPS) * ng_ref[:, gs]
        y_ref[:, gs] = y.astype(bf16)
        return carry

    lax.fori_loop(0, SSM_GROUPS, group, 0)
    if emit_state:
        sout_ref[...] = state[...]


def _ssd(proj, dtx, halo, s0, conv_w, conv_b, dt_bias_x, a_log_x, d_skip_ch, norm_g,
         *, batch, nchunk, vstart, emit_state):
    m = proj.shape[0]
    full = lambda shape: pl.BlockSpec(shape, lambda b, c: (0,) * len(shape))
    state_shape = (SSM_GROUPS, SSM_STATE, SSM_GROUP_WIDTH)
    in_specs = [
        pl.BlockSpec((CHUNK, MAIN_WIDTH), lambda b, c: (b * nchunk + c, 0)),
        pl.BlockSpec((CHUNK, LANES), lambda b, c: (b * nchunk + c, 0)),
        full((8, CONV_CH)),
        full(state_shape),
        full((CONV_WIDTH, CONV_CH)),
        full((1, CONV_CH)),
        full((1, DT_WIDTH)),
        full((1, DT_WIDTH)),
        full((1, SSM_WIDTH)),
        full((1, SSM_WIDTH)),
    ]
    out_specs = [pl.BlockSpec((CHUNK, SSM_WIDTH), lambda b, c: (b * nchunk + c, 0))]
    out_shape = [jax.ShapeDtypeStruct((m, SSM_WIDTH), bf16)]
    if emit_state:
        out_specs.append(full(state_shape))
        out_shape.append(jax.ShapeDtypeStruct(state_shape, f32))
    return pl.pallas_call(
        functools.partial(_ssd_body, vstart=vstart, emit_state=emit_state),
        grid=(batch, nchunk),
        in_specs=in_specs,
        out_specs=out_specs,
        out_shape=out_shape,
        scratch_shapes=[
            pltpu.VMEM(state_shape, f32),
            pltpu.VMEM((8, CONV_CH), f32),
            pltpu.VMEM((8 + CHUNK, SSM_GROUP_WIDTH), f32),
        ],
        compiler_params=pltpu.CompilerParams(
            dimension_semantics=("arbitrary", "arbitrary"), vmem_limit_bytes=VMEM_LIMIT),
        name="ssd_state" if emit_state else "ssd_scan",
    )(proj, dtx, halo, s0, conv_w, conv_b, dt_bias_x, a_log_x, d_skip_ch, norm_g)


def _outproj_body(att_ref, ssm_ref, w_ref, x_ref, gi_ref, bi_ref, g1_ref, b1_ref, h_ref, acc,
                  *, n_att_k):
    k = pl.program_id(1)

    @pl.when(k == 0)
    def _():
        acc[...] = jnp.zeros_like(acc)

    @pl.when(k < n_att_k)
    def _():
        acc[...] += jnp.dot(att_ref[...], w_ref[...], preferred_element_type=f32)

    @pl.when(k >= n_att_k)
    def _():
        acc[...] += jnp.dot(ssm_ref[...], w_ref[...], preferred_element_type=f32)

    @pl.when(k == pl.num_programs(1) - 1)
    def _():
        h0 = _layer_norm(x_ref[...], gi_ref[...], bi_ref[...])
        h_ref[...] = _layer_norm(DEEPNORM_ALPHA * h0 + acc[...], g1_ref[...], b1_ref[...])


def _outproj(att, ssm, w_out, x2d, gi, bi, g1, b1, *, tm, tk):
    m = att.shape[0]
    n_att_k = ATT_WIDTH // tk
    nk = (ATT_WIDTH + SSM_WIDTH) // tk
    return pl.pallas_call(
        functools.partial(_outproj_body, n_att_k=n_att_k),
        grid=(m // tm, nk),
        in_specs=[
            pl.BlockSpec((tm, tk), lambda i, k: (i, jnp.minimum(k, n_att_k - 1))),
            pl.BlockSpec((tm, tk), lambda i, k: (i, jnp.maximum(k - n_att_k, 0))),
            pl.BlockSpec((tk, D_MODEL), lambda i, k: (k, 0)),
            pl.BlockSpec((tm, D_MODEL), lambda i, k: (i, 0)),
            pl.BlockSpec((1, D_MODEL), lambda i, k: (0, 0)),
            pl.BlockSpec((1, D_MODEL), lambda i, k: (0, 0)),
            pl.BlockSpec((1, D_MODEL), lambda i, k: (0, 0)),
            pl.BlockSpec((1, D_MODEL), lambda i, k: (0, 0)),
        ],
        out_specs=pl.BlockSpec((tm, D_MODEL), lambda i, k: (i, 0)),
        out_shape=jax.ShapeDtypeStruct((m, D_MODEL), f32),
        scratch_shapes=[pltpu.VMEM((tm, D_MODEL), f32)],
        compiler_params=pltpu.CompilerParams(
            dimension_semantics=("arbitrary", "arbitrary"), vmem_limit_bytes=VMEM_LIMIT),
        name="outproj_ln1",
    )(att, ssm, w_out, x2d, gi, bi, g1, b1)


def _max01(v):
    return jnp.max(jnp.max(v, axis=1, keepdims=True), axis=0, keepdims=True)


def _router_body(h_ref, wr_ref, br_ref, tri_ref, eidx_ref, gate_ref, rank_ref, cnt_ref, running):
    tt = h_ref.shape[0]
    shape3 = (N_EXPERT_GROUPS, EXPERTS_PER_GROUP, tt)

    @pl.when(pl.program_id(0) == 0)
    def _():
        running[...] = jnp.zeros_like(running)

    logits = lax.dot_general(wr_ref[...], h_ref[...], (((1,), (1,)), ((), ())),
                             precision=lax.Precision.HIGHEST, preferred_element_type=f32)
    scores = (1.0 / (1.0 + jnp.exp(-logits)))
    sel3 = (scores + br_ref[...]).reshape(shape3)
    scores3 = scores.reshape(shape3)
    within = lax.broadcasted_iota(i32, shape3, 1).astype(f32)
    m1 = jnp.max(sel3, axis=1, keepdims=True)
    i1 = jnp.min(jnp.where(sel3 == m1, within, float(EXPERTS_PER_GROUP)), axis=1, keepdims=True)
    m2 = jnp.max(jnp.where(within == i1, -jnp.inf, sel3), axis=1, keepdims=True)
    gs = m1 + m2
    giota = lax.broadcasted_iota(i32, gs.shape, 0).astype(f32)
    gmask = jnp.zeros(gs.shape, f32)
    for _ in range(TOPK_GROUPS):
        gm = jnp.max(gs, axis=0, keepdims=True)
        gi = jnp.min(jnp.where(gs == gm, giota, float(N_EXPERT_GROUPS)), axis=0, keepdims=True)
        hit = giota == gi
        gmask = jnp.where(hit, 1.0, gmask)
        gs = jnp.where(hit, -jnp.inf, gs)
    selm = jnp.where(gmask > 0.0, sel3, -jnp.inf)
    eiota = (lax.broadcasted_iota(i32, shape3, 0) * EXPERTS_PER_GROUP
             + lax.broadcasted_iota(i32, shape3, 1)).astype(f32)
    eidx, gates, hits = [], [], []
    member = jnp.zeros(shape3, f32)
    gsum = jnp.zeros((1, 1, tt), f32)
    for _ in range(TOP_K):
        m = _max01(selm)
        ei = -_max01(-jnp.where(selm == m, eiota, float(N_EXPERTS)))
        hit = eiota == ei
        gk = jnp.sum(jnp.sum(jnp.where(hit, scores3, 0.0), axis=1, keepdims=True), axis=0, keepdims=True)
        eidx.append(ei)
        gates.append(gk)
        hits.append(hit)
        gsum = gsum + gk
        member = jnp.where(hit, 1.0, member)
        selm = jnp.where(hit, -jnp.inf, selm)
    member2 = member.reshape(N_EXPERTS, tt)
    incl = jnp.dot(member2.astype(bf16), tri_ref[...], preferred_element_type=f32)
    base = (running[...] + (incl - member2)).reshape(shape3)
    for k in range(TOP_K):
        rk = jnp.sum(jnp.sum(jnp.where(hits[k], base, 0.0), axis=1, keepdims=True), axis=0, keepdims=True)
        eidx_ref[k:k + 1, :] = eidx[k].reshape(1, tt).astype(i32)
        gate_ref[k:k + 1, :] = (gates[k] / gsum * ROUTED_SCALE).reshape(1, tt)
        rank_ref[k:k + 1, :] = rk.reshape(1, tt).astype(i32)
    total = running[...] + incl[:, tt - 1:tt]
    running[...] = total
    cnt_ref[...] = total.astype(i32)


def _router(h1, w_router_t, b_router_col, tri, *, tt):
    m = h1.shape[0]
    return pl.pallas_call(
        _router_body,
        grid=(m // tt,),
        in_specs=[
            pl.BlockSpec((tt, D_MODEL), lambda i: (i, 0)),
            pl.BlockSpec((N_EXPERTS, D_MODEL), lambda i: (0, 0)),
            pl.BlockSpec((N_EXPERTS, 1), lambda i: (0, 0)),
            pl.BlockSpec((tt, tt), lambda i: (0, 0)),
        ],
        out_specs=[
            pl.BlockSpec((TOP_K, tt), lambda i: (0, i)),
            pl.BlockSpec((TOP_K, tt), lambda i: (0, i)),
            pl.BlockSpec((TOP_K, tt), lambda i: (0, i)),
            pl.BlockSpec((N_EXPERTS, 1), lambda i: (0, 0)),
        ],
        out_shape=[
            jax.ShapeDtypeStruct((TOP_K, m), i32),
            jax.ShapeDtypeStruct((TOP_K, m), f32),
            jax.ShapeDtypeStruct((TOP_K, m), i32),
            jax.ShapeDtypeStruct((N_EXPERTS, 1), i32),
        ],
        scratch_shapes=[pltpu.VMEM((N_EXPERTS, 1), f32)],
        compiler_params=pltpu.CompilerParams(
            dimension_semantics=("arbitrary",), vmem_limit_bytes=VMEM_LIMIT),
        name="router",
    )(h1, w_router_t, b_router_col, tri)


FILL_SIZES = (128, 64, 32, 16, 8, 4, 2, 1)
PACKED_WIDTH = D_MODEL // 2
ROW_TILE = PACKED_WIDTH // LANES
HIGH_HALF = 0xFFFF0000
assert ROW_TILE == SUBLANES


def _row_copy(src_ref, src_row, dst_ref, dst_row, sem):
    src = src_ref.at[pl.ds(pl.multiple_of(src_row * ROW_TILE, ROW_TILE), ROW_TILE)]
    dst = dst_ref.at[pl.ds(pl.multiple_of(dst_row * ROW_TILE, ROW_TILE), ROW_TILE)]
    return pltpu.make_async_copy(src, dst, sem)


def _pack_bf16_pairs(v):
    w = v.shape[1] // 2
    bits = lambda t: lax.bitcast_convert_type(t.astype(bf16).astype(f32), jnp.uint32)
    return (bits(v[:, :w]) >> 16) | (bits(v[:, w:]) & jnp.uint32(HIGH_HALF))


def _unpack_bf16_pairs(p):
    lo = lax.bitcast_convert_type(p << 16, f32)
    hi = lax.bitcast_convert_type(p & jnp.uint32(HIGH_HALF), f32)
    return jnp.concatenate([lo, hi], axis=1)


def _store_tile_rows(ref, packed):
    n = packed.shape[0]
    for j in range(ROW_TILE):
        ref[pl.ds(j, n, stride=ROW_TILE), :] = packed[:, j * LANES:(j + 1) * LANES]


def _load_tile_rows(ref, n):
    return jnp.concatenate([ref[pl.ds(j, n, stride=ROW_TILE), :] for j in range(ROW_TILE)], axis=1)


def _dispatch_body(dest_ref, cnt_ref, pstart_ref, padded_ref, h_ref, xs_ref, packed, zbuf, sem, zsem, *, tt):
    i = pl.program_id(0)
    _store_tile_rows(packed, _pack_bf16_pairs(h_ref[...]))

    @pl.when(i == 0)
    def _():
        zbuf[...] = jnp.zeros_like(zbuf)

        def fill(e, wait):
            cnt = cnt_ref[e]
            first = pstart_ref[e] + cnt
            filler = padded_ref[e] - cnt
            for size in FILL_SIZES:
                @pl.when((filler & size) != 0)
                def _():
                    start = pl.multiple_of((first + (filler & (-2 * size))) * ROW_TILE, ROW_TILE)
                    cp = pltpu.make_async_copy(zbuf.at[pl.ds(0, size * ROW_TILE)],
                                               xs_ref.at[pl.ds(start, size * ROW_TILE)], zsem)
                    if wait:
                        cp.wait()
                    else:
                        cp.start()

        def start_fill(e, c):
            fill(e, False)
            return c

        def wait_fill(e, c):
            fill(e, True)
            return c

        lax.fori_loop(0, N_EXPERTS, start_fill, 0)
        lax.fori_loop(0, N_EXPERTS, wait_fill, 0)

    def issue(t, c):
        for k in range(TOP_K):
            _row_copy(packed, t, xs_ref, dest_ref[0, k, t], sem).start(priority=k % 2)
        return c

    def drain(t, c):
        for k in range(TOP_K):
            _row_copy(packed, 0, xs_ref, 0, sem).wait()
        return c

    lax.fori_loop(0, tt, issue, 0, unroll=8)
    lax.fori_loop(0, tt, drain, 0, unroll=8)


def _dispatch(dest3, counts, pstart, padded, h1, *, rows, tt):
    m = h1.shape[0]
    smem = pl.BlockSpec(memory_space=pltpu.SMEM)
    return pl.pallas_call(
        functools.partial(_dispatch_body, tt=tt),
        grid=(m // tt,),
        in_specs=[
            pl.BlockSpec((1, TOP_K, tt), lambda i: (i, 0, 0), memory_space=pltpu.SMEM),
            smem, smem, smem,
            pl.BlockSpec((tt, D_MODEL), lambda i: (i, 0)),
        ],
        out_specs=pl.BlockSpec(memory_space=pl.ANY),
        out_shape=jax.ShapeDtypeStruct((rows * ROW_TILE, LANES), jnp.uint32),
        scratch_shapes=[
            pltpu.VMEM((tt * ROW_TILE, LANES), jnp.uint32),
            pltpu.VMEM((FILL_SIZES[0] * ROW_TILE, LANES), jnp.uint32),
            pltpu.SemaphoreType.DMA(()),
            pltpu.SemaphoreType.DMA(()),
        ],
        compiler_params=pltpu.CompilerParams(
            dimension_semantics=("arbitrary",), vmem_limit_bytes=VMEM_LIMIT, has_side_effects=True),
        name="moe_dispatch",
    )(dest3, counts, pstart, padded, h1)


def _experts_body(blk_e_ref, nused_ref, first_ref, slot_ref, next_e_ref, x_ref, wg_hbm, wu_hbm, wd_hbm, y_ref,
                  rawg, rawu, rawd, wgb, wub, wdb, sem):
    i = pl.program_id(0)

    def weight_copies(e, s):
        return (pltpu.make_async_copy(wg_hbm.at[e], rawg.at[s], sem.at[s, 0]),
                pltpu.make_async_copy(wu_hbm.at[e], rawu.at[s], sem.at[s, 1]),
                pltpu.make_async_copy(wd_hbm.at[e], rawd.at[s], sem.at[s, 2]))

    @pl.when(i < nused_ref[0])
    def _():
        e = blk_e_ref[i]
        s = slot_ref[i]

        @pl.when(first_ref[i] == 1)
        def _():
            @pl.when(i == 0)
            def _():
                for cp in weight_copies(e, s):
                    cp.start()

            for cp in weight_copies(e, s):
                cp.wait()
            wgb[...] = rawg[s].astype(bf16)
            wub[...] = rawu[s].astype(bf16)
            wdb[...] = rawd[s].astype(bf16)

            @pl.when(next_e_ref[i] >= 0)
            def _():
                for cp in weight_copies(next_e_ref[i], 1 - s):
                    cp.start()

        x = _unpack_bf16_pairs(_load_tile_rows(x_ref, MOE_BLOCK)).astype(bf16)
        hg = jnp.dot(x, wgb[...], preferred_element_type=f32)
        hu = jnp.dot(x, wub[...], preferred_element_type=f32)
        hb = (_silu(hg) * hu).astype(bf16)
        _store_tile_rows(y_ref, _pack_bf16_pairs(jnp.dot(hb, wdb[...], preferred_element_type=f32)))


def _experts(blk_e, nused, first, slot, next_e, xs, w_gate, w_up, w_down):
    rows = xs.shape[0] // ROW_TILE
    nblk = rows // MOE_BLOCK
    blk = lambda i, be, nu, *_: (jnp.minimum(i, nu[0] - 1), 0)
    hbm = pl.BlockSpec(memory_space=pl.ANY)
    grid_spec = pltpu.PrefetchScalarGridSpec(
        num_scalar_prefetch=5,
        grid=(nblk,),
        in_specs=[pl.BlockSpec((MOE_BLOCK * ROW_TILE, LANES), blk), hbm, hbm, hbm],
        out_specs=pl.BlockSpec((MOE_BLOCK * ROW_TILE, LANES), blk),
        scratch_shapes=[
            pltpu.VMEM((2, D_MODEL, EXPERT_DIM), f32),
            pltpu.VMEM((2, D_MODEL, EXPERT_DIM), f32),
            pltpu.VMEM((2, EXPERT_DIM, D_MODEL), f32),
            pltpu.VMEM((D_MODEL, EXPERT_DIM), bf16),
            pltpu.VMEM((D_MODEL, EXPERT_DIM), bf16),
            pltpu.VMEM((EXPERT_DIM, D_MODEL), bf16),
            pltpu.SemaphoreType.DMA((2, 3)),
        ],
    )
    return pl.pallas_call(
        _experts_body,
        grid_spec=grid_spec,
        out_shape=jax.ShapeDtypeStruct((rows * ROW_TILE, LANES), jnp.uint32),
        compiler_params=pltpu.CompilerParams(
            dimension_semantics=("arbitrary",), vmem_limit_bytes=VMEM_LIMIT),
        name="routed_experts",
    )(blk_e, nused, first, slot, next_e, xs, w_gate, w_up, w_down)


def _combine_body(d0_ref, d1_ref, d2_ref, h_ref, gate_ref, wsg_ref, wsu_ref, wsd_ref, g2_ref, b2_ref, ys_ref,
                  o_ref, buf0, buf1, sem, *, tm):
    i = pl.program_id(0)

    def issue(d_ref, buf, s):
        for t in range(tm):
            for k in range(TOP_K):
                _row_copy(ys_ref, d_ref[0, k, t], buf.at[k], t, sem.at[s]).start(priority=k % 2)

    def drain(buf, s):
        def body(t, c):
            for k in range(TOP_K):
                _row_copy(ys_ref, 0, buf.at[k], 0, sem.at[s]).wait()
            return c
        lax.fori_loop(0, tm, body, 0, unroll=8)

    def tile(rows, buf):
        h = h_ref[rows]
        hb = h.astype(bf16)
        sg = jnp.dot(hb, wsg_ref[...], preferred_element_type=f32)
        su = jnp.dot(hb, wsu_ref[...], preferred_element_type=f32)
        ffn = jnp.dot((_silu(sg) * su).astype(bf16), wsd_ref[...], preferred_element_type=f32)
        gate = gate_ref[rows]
        for k in range(TOP_K):
            ffn = ffn + gate[:, k:k + 1] * _unpack_bf16_pairs(_load_tile_rows(buf.at[k], tm))
        o_ref[rows] = _layer_norm(DEEPNORM_ALPHA * h + ffn, g2_ref[...], b2_ref[...])

    @pl.when(i == 0)
    def _():
        issue(d0_ref, buf0, 0)

    drain(buf0, 0)
    issue(d1_ref, buf1, 1)
    tile(slice(0, tm), buf0)
    drain(buf1, 1)
    issue(d2_ref, buf0, 0)
    tile(slice(tm, 2 * tm), buf1)

    @pl.when(i == pl.num_programs(0) - 1)
    def _():
        drain(buf0, 0)


def _combine(dest3, h1, ys, gate_tok, wsg, wsu, wsd, g2, b2, *, tm):
    m = h1.shape[0]
    nt = m // tm
    dest_tile = lambda f: pl.BlockSpec((1, TOP_K, tm), lambda i: (f(i), 0, 0), memory_space=pltpu.SMEM)
    return pl.pallas_call(
        functools.partial(_combine_body, tm=tm),
        grid=(nt // 2,),
        in_specs=[
            dest_tile(lambda i: 2 * i),
            dest_tile(lambda i: 2 * i + 1),
            dest_tile(lambda i: jnp.minimum(2 * i + 2, nt - 1)),
            pl.BlockSpec((2 * tm, D_MODEL), lambda i: (i, 0)),
            pl.BlockSpec((2 * tm, TOP_K), lambda i: (i, 0)),
            pl.BlockSpec((D_MODEL, EXPERT_DIM), lambda i: (0, 0)),
            pl.BlockSpec((D_MODEL, EXPERT_DIM), lambda i: (0, 0)),
            pl.BlockSpec((EXPERT_DIM, D_MODEL), lambda i: (0, 0)),
            pl.BlockSpec((1, D_MODEL), lambda i: (0, 0)),
            pl.BlockSpec((1, D_MODEL), lambda i: (0, 0)),
            pl.BlockSpec(memory_space=pl.ANY),
        ],
        out_specs=pl.BlockSpec((2 * tm, D_MODEL), lambda i: (i, 0)),
        out_shape=jax.ShapeDtypeStruct((m, D_MODEL), f32),
        scratch_shapes=[
            pltpu.VMEM((TOP_K, tm * ROW_TILE, LANES), jnp.uint32),
            pltpu.VMEM((TOP_K, tm * ROW_TILE, LANES), jnp.uint32),
            pltpu.SemaphoreType.DMA((2,)),
        ],
        compiler_params=pltpu.CompilerParams(
            dimension_semantics=("arbitrary",), vmem_limit_bytes=VMEM_LIMIT),
        name="combine_ln2",
    )(dest3, dest3, dest3, h1, gate_tok, wsg, wsu, wsd, g2, b2, ys)


def _rope_tables(pos, width):
    half = ROPE_DIM // 2
    inv_freq = jnp.power(ROPE_THETA, -jnp.arange(0, ROPE_DIM, 2, dtype=f32) / ROPE_DIM)
    ang = pos.astype(f32)[:, None] * inv_freq[None, :]
    cos, sin = jnp.cos(ang), jnp.sin(ang)
    n = pos.shape[0]
    pad = jnp.zeros((n, HEAD_DIM - ROPE_DIM), f32)
    zero = jnp.zeros((n, half), f32)
    c = jnp.concatenate([cos, cos, pad + 1.0], axis=1)
    s1 = jnp.concatenate([-sin, zero, pad], axis=1)
    s2 = jnp.concatenate([zero, sin, pad], axis=1)
    tab = jnp.stack([c, s1, s2])
    return jnp.tile(tab, (1, 1, width // HEAD_DIM))


def _group_lanes(v):
    v = v.reshape(SSM_GROUPS, SSM_HEADS_PER_GROUP)
    return jnp.pad(v, ((0, 0), (0, LANES - SSM_HEADS_PER_GROUP))).reshape(1, DT_WIDTH)


def kernel(x, meta_tokens, ln_in_g, ln_in_b, w_in, conv_w, conv_b, dt_bias, a_log, d_skip, ssm_norm_g, att_norm_g, attn_sinks, w_out, ln1_g, ln1_b, w_router, b_router, w_gate, w_up, w_down, ws_gate, ws_up, ws_down, ln2_g, ln2_b):
    batch, seq, d = x.shape
    assert d == D_MODEL and seq % BLOCK == 0 and meta_tokens.shape == (N_META, D_MODEL)
    assert w_in.shape[0] == 1, "single layer"
    n_tok = batch * seq
    nblk = seq // BLOCK
    row = lambda v: v.reshape(1, -1).astype(f32)

    x2d = x.reshape(n_tok, D_MODEL)
    gi, bi = row(ln_in_g), row(ln_in_b)
    w_main = w_in[0].astype(bf16)
    w_dt = jnp.pad(w_in[0, :, MAIN_WIDTH:], ((0, 0), (0, LANES - SSM_HEADS))).astype(bf16)

    proj, dtx = _ln_inproj(x2d, gi, bi, w_main, w_dt, tm=1024, tn=1280)
    proj_m, dtx_m = _ln_inproj(meta_tokens.astype(f32), gi, bi, w_main, w_dt, tm=N_META, tn=512)
    proj_m = jnp.pad(proj_m, ((BLOCK - N_META, 0), (0, 0)))
    dtx_m = jnp.pad(dtx_m, ((CHUNK - N_META, 0), (0, 0)))

    gw = ATT_GROUP * HEAD_DIM
    tab = _rope_tables(N_META + jnp.arange(seq), gw)
    tab_meta = _rope_tables(jnp.arange(BLOCK), gw)
    kv_meta = jnp.roll(proj_m[:, ATT_WIDTH:ATT_WIDTH + 2 * KV_WIDTH], N_META, axis=0)
    att = _attention(proj, kv_meta, tab, tab_meta, attn_sinks[0].astype(f32), row(att_norm_g[0]),
                     batch=batch, nblk=nblk)

    conv_w0 = conv_w[0].astype(f32)
    conv_b0 = row(conv_b[0])
    dtb_x = _group_lanes(dt_bias[0].astype(f32))
    alog_x = _group_lanes(a_log[0].astype(f32))
    dskip_ch = jnp.repeat(d_skip[0].astype(f32), HEAD_DIM).reshape(1, SSM_WIDTH)
    ng = row(ssm_norm_g[0])
    zeros_halo = jnp.zeros((8, CONV_CH), f32)
    zeros_state = jnp.zeros((SSM_GROUPS, SSM_STATE, SSM_GROUP_WIDTH), f32)
    _, s_meta = _ssd(proj_m, dtx_m, zeros_halo, zeros_state, conv_w0, conv_b0, dtb_x, alog_x, dskip_ch, ng,
                     batch=1, nchunk=1, vstart=CHUNK - N_META, emit_state=True)
    halo = proj_m[CHUNK - 8:, XBC_COL:].astype(f32)
    (ssm,) = _ssd(proj, dtx, halo, s_meta, conv_w0, conv_b0, dtb_x, alog_x, dskip_ch, ng,
                  batch=batch, nchunk=nblk, vstart=0, emit_state=False)

    h1 = _outproj(att, ssm, w_out[0].astype(bf16), x2d, gi, bi, row(ln1_g[0]), row(ln1_b[0]),
                  tm=512, tk=2048)

    tt = 512
    tri = (jnp.arange(tt)[:, None] <= jnp.arange(tt)[None, :]).astype(bf16)
    eidx, gate, rank, counts = _router(h1, w_router[0].T.astype(f32), b_router[0].reshape(N_EXPERTS, 1).astype(f32),
                                       tri, tt=tt)
    counts = counts.reshape(N_EXPERTS)
    padded = (counts + MOE_BLOCK - 1) // MOE_BLOCK * MOE_BLOCK
    pend = jnp.cumsum(padded)
    pstart = pend - padded
    first_row = jnp.sum(jnp.where(eidx[..., None] == jnp.arange(N_EXPERTS, dtype=i32), pstart.astype(i32), 0), axis=-1)
    dest = first_row + rank
    n_blocks = n_tok * TOP_K // MOE_BLOCK + N_EXPERTS
    blk_first = jnp.arange(n_blocks, dtype=i32) * MOE_BLOCK
    blk_e = jnp.minimum(jnp.sum(pend[None, :] <= blk_first[:, None], axis=1), N_EXPERTS - 1).astype(i32)
    nused = (pend[-1] // MOE_BLOCK).astype(i32).reshape(1)
    rows = n_blocks * MOE_BLOCK
    tiles = lambda t: dest.reshape(TOP_K, n_tok // t, t).transpose(1, 0, 2)

    td, tc = 512, 128
    xs = _dispatch(tiles(td), counts, pstart.astype(i32), padded.astype(i32), h1, rows=rows, tt=td)
    eids = jnp.arange(N_EXPERTS, dtype=i32)
    nonempty = counts > 0
    ordinal = jnp.cumsum(nonempty.astype(i32)) - nonempty.astype(i32)
    later = (eids[None, :] > eids[:, None]) & nonempty[None, :]
    next_nonempty = jnp.min(jnp.where(later, eids[None, :], N_EXPERTS), axis=1)
    next_nonempty = jnp.where(next_nonempty == N_EXPERTS, -1, next_nonempty).astype(i32)
    onehot_e = blk_e[:, None] == eids[None, :]
    pick = lambda table: jnp.sum(jnp.where(onehot_e, table[None, :], 0), axis=1).astype(i32)
    first = jnp.concatenate([jnp.ones((1,), i32), (blk_e[1:] != blk_e[:-1]).astype(i32)])
    ys = _experts(blk_e, nused, first, pick(ordinal) % 2, pick(next_nonempty), xs, w_gate[0], w_up[0], w_down[0])
    out = _combine(tiles(tc), h1, ys, gate.T, ws_gate[0].astype(bf16), ws_up[0].astype(bf16),
                   ws_down[0].astype(bf16), row(ln2_g[0]), row(ln2_b[0]), tm=tc)
    return out.reshape(batch, seq, D_MODEL)
```

```python
import functools
import math

import jax
import jax.numpy as jnp
from jax import lax
from jax.experimental import pallas as pl
from jax.experimental.pallas import tpu as pltpu

f32 = jnp.float32
bf16 = jnp.bfloat16
i32 = jnp.int32

D_MODEL = 2048
N_META = 16
HEAD_DIM = 64
ATT_HEADS = 32
ATT_KV_HEADS = 4
ATT_GROUP = ATT_HEADS // ATT_KV_HEADS
ATT_WIDTH = 2048
KV_WIDTH = 256
BLOCK = 128
ROPE_DIM = 16
ROPE_THETA = 500000.0
SSM_WIDTH = 4096
SSM_HEADS = 64
SSM_GROUPS = 8
SSM_HEADS_PER_GROUP = SSM_HEADS // SSM_GROUPS
SSM_STATE = 128
SSM_GROUP_WIDTH = SSM_WIDTH // SSM_GROUPS
CONV_WIDTH = 4
CHUNK = 128
CONV_CH = SSM_WIDTH + 2 * SSM_GROUPS * SSM_STATE
MAIN_WIDTH = ATT_WIDTH + 2 * KV_WIDTH + SSM_WIDTH + CONV_CH
Z_COL = ATT_WIDTH + 2 * KV_WIDTH
XBC_COL = Z_COL + SSM_WIDTH
N_EXPERTS = 64
EXPERT_DIM = 512
TOP_K = 8
N_EXPERT_GROUPS = 8
EXPERTS_PER_GROUP = N_EXPERTS // N_EXPERT_GROUPS
TOPK_GROUPS = 4
ROUTED_SCALE = 2.5
MOE_BLOCK = 512
DEEPNORM_ALPHA = 2.0 ** 0.25
LN_EPS = 1e-5
RMS_EPS = 1e-6
NEG_INF = -1e30
LANES = 128
SUBLANES = 8
DT_WIDTH = SSM_GROUPS * LANES
VMEM_LIMIT = 56 * 1024 * 1024


def _layer_norm(x, g, b):
    mu = jnp.mean(x, axis=-1, keepdims=True)
    xc = x - mu
    var = jnp.mean(xc * xc, axis=-1, keepdims=True)
    return xc * lax.rsqrt(var + LN_EPS) * g + b


def _silu(x):
    return x * (1.0 / (1.0 + jnp.exp(-x)))


def _softplus(x):
    return jnp.maximum(x, 0.0) + jnp.log1p(jnp.exp(-jnp.abs(x)))


def _ln_inproj_body(x_ref, g_ref, b_ref, w_ref, wdt_ref, o_ref, dt_ref, h_scr):
    @pl.when(pl.program_id(1) == 0)
    def _():
        h = _layer_norm(x_ref[...], g_ref[...], b_ref[...]).astype(bf16)
        h_scr[...] = h
        dt_ref[...] = jnp.dot(h, wdt_ref[...], preferred_element_type=f32)

    o_ref[...] = jnp.dot(h_scr[...], w_ref[...], preferred_element_type=f32).astype(bf16)


def _ln_inproj(x2d, g, b, w_main, w_dt, *, tm, tn):
    m = x2d.shape[0]
    return pl.pallas_call(
        _ln_inproj_body,
        grid=(m // tm, MAIN_WIDTH // tn),
        in_specs=[
            pl.BlockSpec((tm, D_MODEL), lambda i, j: (i, 0)),
            pl.BlockSpec((1, D_MODEL), lambda i, j: (0, 0)),
            pl.BlockSpec((1, D_MODEL), lambda i, j: (0, 0)),
            pl.BlockSpec((D_MODEL, tn), lambda i, j: (0, j)),
            pl.BlockSpec((D_MODEL, LANES), lambda i, j: (0, 0)),
        ],
        out_specs=[
            pl.BlockSpec((tm, tn), lambda i, j: (i, j)),
            pl.BlockSpec((tm, LANES), lambda i, j: (i, 0)),
        ],
        out_shape=[
            jax.ShapeDtypeStruct((m, MAIN_WIDTH), bf16),
            jax.ShapeDtypeStruct((m, LANES), f32),
        ],
        scratch_shapes=[pltpu.VMEM((tm, D_MODEL), bf16)],
        compiler_params=pltpu.CompilerParams(
            dimension_semantics=("arbitrary", "arbitrary"), vmem_limit_bytes=VMEM_LIMIT),
        name="ln_inproj",
    )(x2d, g, b, w_main, w_dt)


def _rotate(t, tab):
    w = t.shape[-1]
    half = ROPE_DIM // 2
    return (t * tab[0]
            + pltpu.roll(t, w - half, 1) * tab[1]
            + pltpu.roll(t, half, 1) * tab[2])


def _attention_body(sink_ref, q_ref, kvc_ref, kvp_ref, kvm_ref, tabc_ref, tabp_ref, tabm_ref,
                    g_ref, o_ref):
    j = pl.program_id(1)
    tabc = tabc_ref[...]
    kc = _rotate(kvc_ref[:, :KV_WIDTH].astype(f32), tabc[:, :, :KV_WIDTH]).astype(bf16)
    kp = _rotate(kvp_ref[:, :KV_WIDTH].astype(f32), tabp_ref[:, :, :KV_WIDTH]).astype(bf16)
    km = _rotate(kvm_ref[:, :KV_WIDTH].astype(f32), tabm_ref[:, :, :KV_WIDTH]).astype(bf16)
    v_all = jnp.concatenate([kvp_ref[:, KV_WIDTH:], kvc_ref[:, KV_WIDTH:], kvm_ref[:, KV_WIDTH:]], axis=0)

    rows = ATT_GROUP * BLOCK
    r = lax.broadcasted_iota(i32, (rows, BLOCK), 0) % BLOCK
    c = lax.broadcasted_iota(i32, (rows, BLOCK), 1)
    lower = c <= r
    prev_ok = (c > r) & (j > 0)
    meta_ok = c < N_META
    head_of_row = lax.broadcasted_iota(i32, (rows, 1), 0) // BLOCK
    nt_dims = (((1,), (1,)), ((), ()))

    outs = []
    for g in range(ATT_KV_HEADS):
        gw = ATT_GROUP * HEAD_DIM
        hs = slice(g * HEAD_DIM, (g + 1) * HEAD_DIM)
        qg = _rotate(q_ref[:, g * gw:(g + 1) * gw].astype(f32), tabc) * (HEAD_DIM ** -0.5)
        qs = jnp.concatenate([qg[:, h * HEAD_DIM:(h + 1) * HEAD_DIM] for h in range(ATT_GROUP)],
                             axis=0).astype(bf16)
        s_cur = lax.dot_general(qs, kc[:, hs], nt_dims, preferred_element_type=f32)
        s_prev = lax.dot_general(qs, kp[:, hs], nt_dims, preferred_element_type=f32)
        s_meta = lax.dot_general(qs, km[:, hs], nt_dims, preferred_element_type=f32)
        s_band = jnp.where(lower, s_cur, jnp.where(prev_ok, s_prev, NEG_INF))
        s_meta = jnp.where(meta_ok, s_meta, NEG_INF)
        sink = jnp.zeros((rows, 1), f32)
        for h in range(ATT_GROUP):
            sink = jnp.where(head_of_row == h, sink_ref[g * ATT_GROUP + h], sink)
        m = jnp.maximum(jnp.max(jnp.maximum(s_band, s_meta), axis=1, keepdims=True), sink)
        p_band = jnp.exp(s_band - m)
        p_meta = jnp.exp(s_meta - m)
        denom = jnp.sum(p_band + p_meta, axis=1, keepdims=True) + jnp.exp(sink - m)
        p_cur = jnp.where(lower, p_band, 0.0)
        p = jnp.concatenate([p_band - p_cur, p_cur, p_meta], axis=1).astype(bf16)
        o = jnp.dot(p, v_all[:, hs], preferred_element_type=f32) / denom
        outs.append(jnp.concatenate([o[h * BLOCK:(h + 1) * BLOCK] for h in range(ATT_GROUP)], axis=1))
    att = jnp.concatenate(outs, axis=1)
    att = att * lax.rsqrt(jnp.mean(att * att, axis=-1, keepdims=True) + RMS_EPS) * g_ref[...]
    o_ref[...] = att.astype(bf16)


def _attention(proj, kv_meta, tab, tab_meta, sinks, att_g, *, batch, nblk):
    m = proj.shape[0]
    kvb = ATT_WIDTH // (2 * KV_WIDTH)
    gw = ATT_GROUP * HEAD_DIM
    return pl.pallas_call(
        _attention_body,
        grid=(batch, nblk),
        in_specs=[
            pl.BlockSpec(memory_space=pltpu.SMEM),
            pl.BlockSpec((BLOCK, ATT_WIDTH), lambda b, j: (b * nblk + j, 0)),
            pl.BlockSpec((BLOCK, 2 * KV_WIDTH), lambda b, j: (b * nblk + j, kvb)),
            pl.BlockSpec((BLOCK, 2 * KV_WIDTH), lambda b, j: (b * nblk + jnp.maximum(j - 1, 0), kvb)),
            pl.BlockSpec((BLOCK, 2 * KV_WIDTH), lambda b, j: (0, 0)),
            pl.BlockSpec((3, BLOCK, gw), lambda b, j: (0, j, 0)),
            pl.BlockSpec((3, BLOCK, gw), lambda b, j: (0, jnp.maximum(j - 1, 0), 0)),
            pl.BlockSpec((3, BLOCK, gw), lambda b, j: (0, 0, 0)),
            pl.BlockSpec((1, ATT_WIDTH), lambda b, j: (0, 0)),
        ],
        out_specs=pl.BlockSpec((BLOCK, ATT_WIDTH), lambda b, j: (b * nblk + j, 0)),
        out_shape=jax.ShapeDtypeStruct((m, ATT_WIDTH), bf16),
        compiler_params=pltpu.CompilerParams(
            dimension_semantics=("arbitrary", "arbitrary"), vmem_limit_bytes=VMEM_LIMIT),
        name="swa_attention",
    )(sinks, proj, proj, proj, kv_meta, tab, tab, tab_meta, att_g)


def _cumsum_rows(x):
    n = x.shape[0]
    row = lax.broadcasted_iota(i32, x.shape, 0)
    k = 1
    while k < n:
        x = x + jnp.where(row >= k, pltpu.roll(x, k, 0), 0.0)
        k *= 2
    return x


def _expand_heads(col):
    lane = lax.broadcasted_iota(i32, col.shape, 1) // HEAD_DIM
    per_tile = LANES // HEAD_DIM
    return jnp.concatenate(
        [jnp.take_along_axis(col, lane + per_tile * j, axis=1) for j in range(SSM_GROUP_WIDTH // LANES)], axis=1)


def _ssd_body(proj_ref, dt_ref, halo_ref, s0_ref, cw_ref, cb_ref, dtb_ref, alog_ref, dskip_ref, ng_ref,
              *rest, vstart, emit_state):
    if emit_state:
        y_ref, sout_ref, state, halo, work = rest
    else:
        y_ref, state, halo, work = rest

    @pl.when(pl.program_id(1) == 0)
    def _():
        state[...] = s0_ref[...]
        halo[...] = halo_ref[...]

    if vstart == 0:
        keep_valid = lambda v: v
    else:
        valid = lax.broadcasted_iota(i32, (CHUNK, 1), 0) >= vstart
        keep_valid = lambda v: jnp.where(valid, v, 0.0)
    lane = lax.broadcasted_iota(i32, (CHUNK, LANES), 1)
    causal =(lax.broadcasted_iota(i32, (CHUNK, CHUNK), 0) >= lax.broadcasted_iota(i32, (CHUNK, CHUNK), 1))

    def conv(col, width):
        u = proj_ref[:, pl.ds(pl.multiple_of(XBC_COL + col, LANES), width)].astype(f32)
        cs = pl.ds(pl.multiple_of(col, LANES), width)
        work[0:8, 0:width] = halo[:, cs]
        work[8:8 + CHUNK, 0:width] = u
        w = cw_ref[:, cs]
        acc = cb_ref[:, cs] + u * w[CONV_WIDTH - 1:CONV_WIDTH]
        for jj in range(CONV_WIDTH - 1):
            acc = acc + work[5 + jj:5 + jj + CHUNK, 0:width] * w[jj:jj + 1]
        halo[:, cs] = u[CHUNK - 8:]
        return keep_valid(_silu(acc))

    def group(g, carry):
        gs = pl.ds(pl.multiple_of(g * SSM_GROUP_WIDTH, SSM_GROUP_WIDTH), SSM_GROUP_WIDTH)
        gl = pl.ds(pl.multiple_of(g * LANES, LANES), LANES)
        xs = conv(g * SSM_GROUP_WIDTH, SSM_GROUP_WIDTH)
        bm = conv(SSM_WIDTH + g * SSM_STATE, SSM_STATE)
        cm = conv(SSM_WIDTH + SSM_GROUPS * SSM_STATE + g * SSM_STATE, SSM_STATE)

        dt_raw = jnp.take_along_axis(dt_ref[...], (lane + g * SSM_HEADS_PER_GROUP) % LANES, axis=1)
        dt = keep_valid(_softplus(dt_raw + dtb_ref[:, gl]))
        a_cs = _cumsum_rows(dt * -jnp.exp(alog_ref[:, gl]))
        a_cs_t = a_cs.T
        a_last = a_cs[CHUNK - 1:CHUNK]

        xdt = xs * _expand_heads(dt)
        bmb = bm.astype(bf16)
        cmb = cm.astype(bf16)
        cb = lax.dot_general(cmb, bmb, (((1,), (1,)), ((), ())), preferred_element_type=f32)
        xdtb = xdt.astype(bf16)
        ys = []
        for i in range(SSM_HEADS_PER_GROUP):
            seg = a_cs[:, i:i + 1] - a_cs_t[i:i + 1, :]
            lmat = (cb * jnp.exp(jnp.where(causal, seg, -jnp.inf))).astype(bf16)
            ys.append(jnp.dot(lmat, xdtb[:, i * HEAD_DIM:(i + 1) * HEAD_DIM], preferred_element_type=f32))
        y = jnp.concatenate(ys, axis=1)

        st = state[g]
        y_off = jnp.dot(cmb, st.astype(bf16), preferred_element_type=f32)
        y = y + y_off * _expand_heads(jnp.exp(a_cs))

        x_end = (xdt * _expand_heads(jnp.exp(a_last - a_cs))).astype(bf16)
        upd = lax.dot_general(bmb, x_end, (((0,), (0,)), ((), ())), preferred_element_type=f32)
        keep = _expand_heads(jnp.exp(a_cs[CHUNK - SUBLANES:]))[SUBLANES - 1:]
        state[g] = st * keep + upd

        y = y + dskip_ref[:, gs] * xs
        y = y * _silu(proj_ref[:, pl.ds(pl.multiple_of(Z_COL + g * SSM_GROUP_WIDTH, SSM_GROUP_WIDTH),
                                       SSM_GROUP_WIDTH)].astype(f32))
        y = y * lax.rsqrt(jnp.mean(y * y, axis=-1, keepdims=True) + RMS_EPS) * ng_ref[:, gs]
        y_ref[:, gs] = y.astype(bf16)
        return carry

    lax.fori_loop(0, SSM_GROUPS, group, 0)
    if emit_state:
        sout_ref[...] = state[...]


def _ssd(proj, dtx, halo, s0, conv_w, conv_b, dt_bias_x, a_log_x, d_skip_ch, norm_g,
         *, batch, nchunk, vstart, emit_state):
    m = proj.shape[0]
    full = lambda shape: pl.BlockSpec(shape, lambda b, c: (0,) * len(shape))
    state_shape = (SSM_GROUPS, SSM_STATE, SSM_GROUP_WIDTH)
    in_specs = [
        pl.BlockSpec((CHUNK, MAIN_WIDTH), lambda b, c: (b * nchunk + c, 0)),
        pl.BlockSpec((CHUNK, LANES), lambda b, c: (b * nchunk + c, 0)),
        full((8, CONV_CH)),
        full(state_shape),
        full((CONV_WIDTH, CONV_CH)),
        full((1, CONV_CH)),
        full((1, DT_WIDTH)),
        full((1, DT_WIDTH)),
        full((1, SSM_WIDTH)),
        full((1, SSM_WIDTH)),
    ]
    out_specs = [pl.BlockSpec((CHUNK, SSM_WIDTH), lambda b, c: (b * nchunk + c, 0))]
    out_shape = [jax.ShapeDtypeStruct((m, SSM_WIDTH), bf16)]
    if emit_state:
        out_specs.append(full(state_shape))
        out_shape.append(jax.ShapeDtypeStruct(state_shape, f32))
    return pl.pallas_call(
        functools.partial(_ssd_body, vstart=vstart, emit_state=emit_state),
        grid=(batch, nchunk),
        in_specs=in_specs,
        out_specs=out_specs,
        out_shape=out_shape,
        scratch_shapes=[
            pltpu.VMEM(state_shape, f32),
            pltpu.VMEM((8, CONV_CH), f32),
            pltpu.VMEM((8 + CHUNK, SSM_GROUP_WIDTH), f32),
        ],
        compiler_params=pltpu.CompilerParams(
            dimension_semantics=("arbitrary", "arbitrary"), vmem_limit_bytes=VMEM_LIMIT),
        name="ssd_state" if emit_state else "ssd_scan",
    )(proj, dtx, halo, s0, conv_w, conv_b, dt_bias_x, a_log_x, d_skip_ch, norm_g)


def _outproj_body(att_ref, ssm_ref, w_ref, x_ref, gi_ref, bi_ref, g1_ref, b1_ref, h_ref, acc,
                  *, n_att_k):
    k = pl.program_id(1)

    @pl.when(k == 0)
    def _():
        acc[...] = jnp.zeros_like(acc)

    @pl.when(k < n_att_k)
    def _():
        acc[...] += jnp.dot(att_ref[...], w_ref[...], preferred_element_type=f32)

    @pl.when(k >= n_att_k)
    def _():
        acc[...] += jnp.dot(ssm_ref[...], w_ref[...], preferred_element_type=f32)

    @pl.when(k == pl.num_programs(1) - 1)
    def _():
        h0 = _layer_norm(x_ref[...], gi_ref[...], bi_ref[...])
        h_ref[...] = _layer_norm(DEEPNORM_ALPHA * h0 + acc[...], g1_ref[...], b1_ref[...])


def _outproj(att, ssm, w_out, x2d, gi, bi, g1, b1, *, tm, tk):
    m = att.shape[0]
    n_att_k = ATT_WIDTH // tk
    nk = (ATT_WIDTH + SSM_WIDTH) // tk
    return pl.pallas_call(
        functools.partial(_outproj_body, n_att_k=n_att_k),
        grid=(m // tm, nk),
        in_specs=[
            pl.BlockSpec((tm, tk), lambda i, k: (i, jnp.minimum(k, n_att_k - 1))),
            pl.BlockSpec((tm, tk), lambda i, k: (i, jnp.maximum(k - n_att_k, 0))),
            pl.BlockSpec((tk, D_MODEL), lambda i, k: (k, 0)),
            pl.BlockSpec((tm, D_MODEL), lambda i, k: (i, 0)),
            pl.BlockSpec((1, D_MODEL), lambda i, k: (0, 0)),
            pl.BlockSpec((1, D_MODEL), lambda i, k: (0, 0)),
            pl.BlockSpec((1, D_MODEL), lambda i, k: (0, 0)),
            pl.BlockSpec((1, D_MODEL), lambda i, k: (0, 0)),
        ],
        out_specs=pl.BlockSpec((tm, D_MODEL), lambda i, k: (i, 0)),
        out_shape=jax.ShapeDtypeStruct((m, D_MODEL), f32),
        scratch_shapes=[pltpu.VMEM((tm, D_MODEL), f32)],
        compiler_params=pltpu.CompilerParams(
            dimension_semantics=("arbitrary", "arbitrary"), vmem_limit_bytes=VMEM_LIMIT),
        name="outproj_ln1",
    )(att, ssm, w_out, x2d, gi, bi, g1, b1)


def _max01(v):
    return jnp.max(jnp.max(v, axis=1, keepdims=True), axis=0, keepdims=True)


def _router_body(h_ref, wr_ref, br_ref, tri_ref, eidx_ref, gate_ref, rank_ref, cnt_ref, running):
    tt = h_ref.shape[0]
    shape3 = (N_EXPERT_GROUPS, EXPERTS_PER_GROUP, tt)

    @pl.when(pl.program_id(0) == 0)
    def _():
        running[...] = jnp.zeros_like(running)

    logits = lax.dot_general(wr_ref[...], h_ref[...], (((1,), (1,)), ((), ())),
                             precision=lax.Precision.HIGHEST, preferred_element_type=f32)
    scores = (1.0 / (1.0 + jnp.exp(-logits)))
    sel3 = (scores + br_ref[...]).reshape(shape3)
    scores3 = scores.reshape(shape3)
    within = lax.broadcasted_iota(i32, shape3, 1).astype(f32)
    m1 = jnp.max(sel3, axis=1, keepdims=True)
    i1 = jnp.min(jnp.where(sel3 == m1, within, float(EXPERTS_PER_GROUP)), axis=1, keepdims=True)
    m2 = jnp.max(jnp.where(within == i1, -jnp.inf, sel3), axis=1, keepdims=True)
    gs = m1 + m2
    giota = lax.broadcasted_iota(i32, gs.shape, 0).astype(f32)
    gmask = jnp.zeros(gs.shape, f32)
    for _ in range(TOPK_GROUPS):
        gm = jnp.max(gs, axis=0, keepdims=True)
        gi = jnp.min(jnp.where(gs == gm, giota, float(N_EXPERT_GROUPS)), axis=0, keepdims=True)
        hit = giota == gi
        gmask = jnp.where(hit, 1.0, gmask)
        gs = jnp.where(hit, -jnp.inf, gs)
    selm = jnp.where(gmask > 0.0, sel3, -jnp.inf)
    eiota = (lax.broadcasted_iota(i32, shape3, 0) * EXPERTS_PER_GROUP
             + lax.broadcasted_iota(i32, shape3, 1)).astype(f32)
    eidx, gates, hits = [], [], []
    member = jnp.zeros(shape3, f32)
    gsum = jnp.zeros((1, 1, tt), f32)
    for _ in range(TOP_K):
        m = _max01(selm)
        ei = -_max01(-jnp.where(selm == m, eiota, float(N_EXPERTS)))
        hit = eiota == ei
        gk = jnp.sum(jnp.sum(jnp.where(hit, scores3, 0.0), axis=1, keepdims=True), axis=0, keepdims=True)
        eidx.append(ei)
        gates.append(gk)
        hits.append(hit)
        gsum = gsum + gk
        member = jnp.where(hit, 1.0, member)
        selm = jnp.where(hit, -jnp.inf, selm)
    member2 = member.reshape(N_EXPERTS, tt)
    incl = jnp.dot(member2.astype(bf16), tri_ref[...], preferred_element_type=f32)
    base = (running[...] + (incl - member2)).reshape(shape3)
    for k in range(TOP_K):
        rk = jnp.sum(jnp.sum(jnp.where(hits[k], base, 0.0), axis=1, keepdims=True), axis=0, keepdims=True)
        eidx_ref[k:k + 1, :] = eidx[k].reshape(1, tt).astype(i32)
        gate_ref[k:k + 1, :] = (gates[k] / gsum * ROUTED_SCALE).reshape(1, tt)
        rank_ref[k:k + 1, :] = rk.reshape(1, tt).astype(i32)
    total = running[...] + incl[:, tt - 1:tt]
    running[...] = total
    cnt_ref[...] = total.astype(i32)


def _router(h1, w_router_t, b_router_col, tri, *, tt):
    m = h1.shape[0]
    return pl.pallas_call(
        _router_body,
        grid=(m // tt,),
        in_specs=[
            pl.BlockSpec((tt, D_MODEL), lambda i: (i, 0)),
            pl.BlockSpec((N_EXPERTS, D_MODEL), lambda i: (0, 0)),
            pl.BlockSpec((N_EXPERTS, 1), lambda i: (0, 0)),
            pl.BlockSpec((tt, tt), lambda i: (0, 0)),
        ],
        out_specs=[
            pl.BlockSpec((TOP_K, tt), lambda i: (0, i)),
            pl.BlockSpec((TOP_K, tt), lambda i: (0, i)),
            pl.BlockSpec((TOP_K, tt), lambda i: (0, i)),
            pl.BlockSpec((N_EXPERTS, 1), lambda i: (0, 0)),
        ],
        out_shape=[
            jax.ShapeDtypeStruct((TOP_K, m), i32),
            jax.ShapeDtypeStruct((TOP_K, m), f32),
            jax.ShapeDtypeStruct((TOP_K, m), i32),
            jax.ShapeDtypeStruct((N_EXPERTS, 1), i32),
        ],
        scratch_shapes=[pltpu.VMEM((N_EXPERTS, 1), f32)],
        compiler_params=pltpu.CompilerParams(
            dimension_semantics=("arbitrary",), vmem_limit_bytes=VMEM_LIMIT),
        name="router",
    )(h1, w_router_t, b_router_col, tri)


FILL_SIZES = tuple(MOE_BLOCK >> s for s in range(1, MOE_BLOCK.bit_length()))
PACKED_WIDTH = D_MODEL // 2
ROW_TILE = PACKED_WIDTH // LANES
HIGH_HALF = 0xFFFF0000
assert ROW_TILE == SUBLANES


def _row_copy(src_ref, src_row, dst_ref, dst_row, sem):
    src = src_ref.at[pl.ds(pl.multiple_of(src_row * ROW_TILE, ROW_TILE), ROW_TILE)]
    dst = dst_ref.at[pl.ds(pl.multiple_of(dst_row * ROW_TILE, ROW_TILE), ROW_TILE)]
    return pltpu.make_async_copy(src, dst, sem)


def _pack_bf16_pairs(v):
    w = v.shape[1] // 2
    bits = lambda t: lax.bitcast_convert_type(t.astype(bf16).astype(f32), jnp.uint32)
    return (bits(v[:, :w]) >> 16) | (bits(v[:, w:]) & jnp.uint32(HIGH_HALF))


def _unpack_bf16_pairs(p):
    lo = lax.bitcast_convert_type(p << 16, f32)
    hi = lax.bitcast_convert_type(p & jnp.uint32(HIGH_HALF), f32)
    return jnp.concatenate([lo, hi], axis=1)


def _store_tile_rows(ref, packed):
    n = packed.shape[0]
    for j in range(ROW_TILE):
        ref[pl.ds(j, n, stride=ROW_TILE), :] = packed[:, j * LANES:(j + 1) * LANES]


def _load_tile_rows(ref, n):
    return jnp.concatenate([ref[pl.ds(j, n, stride=ROW_TILE), :] for j in range(ROW_TILE)], axis=1)


def _dispatch_body(dest_ref, cnt_ref, pstart_ref, padded_ref, h_ref, xs_ref, packed, zbuf, sem, zsem, *, tt):
    i = pl.program_id(0)
    _store_tile_rows(packed, _pack_bf16_pairs(h_ref[...]))

    @pl.when(i == 0)
    def _():
        zbuf[...] = jnp.zeros_like(zbuf)

        def fill(e, wait):
            cnt = cnt_ref[e]
            first = pstart_ref[e] + cnt
            filler = padded_ref[e] - cnt
            for size in FILL_SIZES:
                @pl.when((filler & size) != 0)
                def _():
                    start = pl.multiple_of((first + (filler & (-2 * size))) * ROW_TILE, ROW_TILE)
                    cp = pltpu.make_async_copy(zbuf.at[pl.ds(0, size * ROW_TILE)],
                                               xs_ref.at[pl.ds(start, size * ROW_TILE)], zsem)
                    if wait:
                        cp.wait()
                    else:
                        cp.start()

        def start_fill(e, c):
            fill(e, False)
            return c

        def wait_fill(e, c):
            fill(e, True)
            return c

        lax.fori_loop(0, N_EXPERTS, start_fill, 0)
        lax.fori_loop(0, N_EXPERTS, wait_fill, 0)

    def issue(t, c):
        for k in range(TOP_K):
            _row_copy(packed, t, xs_ref, dest_ref[0, k, t], sem).start(priority=k % 2)
        return c

    def drain(t, c):
        for k in range(TOP_K):
            _row_copy(packed, 0, xs_ref, 0, sem).wait()
        return c

    lax.fori_loop(0, tt, issue, 0, unroll=8)
    lax.fori_loop(0, tt, drain, 0, unroll=8)


def _dispatch(dest3, counts, pstart, padded, h1, *, rows, tt):
    m = h1.shape[0]
    smem = pl.BlockSpec(memory_space=pltpu.SMEM)
    return pl.pallas_call(
        functools.partial(_dispatch_body, tt=tt),
        grid=(m // tt,),
        in_specs=[
            pl.BlockSpec((1, TOP_K, tt), lambda i: (i, 0, 0), memory_space=pltpu.SMEM),
            smem, smem, smem,
            pl.BlockSpec((tt, D_MODEL), lambda i: (i, 0)),
        ],
        out_specs=pl.BlockSpec(memory_space=pl.ANY),
        out_shape=jax.ShapeDtypeStruct((rows * ROW_TILE, LANES), jnp.uint32),
        scratch_shapes=[
            pltpu.VMEM((tt * ROW_TILE, LANES), jnp.uint32),
            pltpu.VMEM((FILL_SIZES[0] * ROW_TILE, LANES), jnp.uint32),
            pltpu.SemaphoreType.DMA(()),
            pltpu.SemaphoreType.DMA(()),
        ],
        compiler_params=pltpu.CompilerParams(
            dimension_semantics=("arbitrary",), vmem_limit_bytes=VMEM_LIMIT, has_side_effects=True),
        name="moe_dispatch",
    )(dest3, counts, pstart, padded, h1)


def _experts_body(blk_e_ref, nused_ref, first_ref, slot_ref, next_e_ref, x_ref, wg_hbm, wu_hbm, wd_hbm, y_ref,
                  rawg, rawu, rawd, wgb, wub, wdb, sem):
    i = pl.program_id(0)

    def weight_copies(e, s):
        return (pltpu.make_async_copy(wg_hbm.at[e], rawg.at[s], sem.at[s, 0]),
                pltpu.make_async_copy(wu_hbm.at[e], rawu.at[s], sem.at[s, 1]),
                pltpu.make_async_copy(wd_hbm.at[e], rawd.at[s], sem.at[s, 2]))

    @pl.when(i < nused_ref[0])
    def _():
        e = blk_e_ref[i]
        s = slot_ref[i]

        @pl.when(first_ref[i] == 1)
        def _():
            @pl.when(i == 0)
            def _():
                for cp in weight_copies(e, s):
                    cp.start()

            for cp in weight_copies(e, s):
                cp.wait()
            wgb[...] = rawg[s].astype(bf16)
            wub[...] = rawu[s].astype(bf16)
            wdb[...] = rawd[s].astype(bf16)

            @pl.when(next_e_ref[i] >= 0)
            def _():
                for cp in weight_copies(next_e_ref[i], 1 - s):
                    cp.start()

        x = _unpack_bf16_pairs(_load_tile_rows(x_ref, MOE_BLOCK)).astype(bf16)
        hg = jnp.dot(x, wgb[...], preferred_element_type=f32)
        hu = jnp.dot(x, wub[...], preferred_element_type=f32)
        hb = (_silu(hg) * hu).astype(bf16)
        _store_tile_rows(y_ref, _pack_bf16_pairs(jnp.dot(hb, wdb[...], preferred_element_type=f32)))


def _experts(blk_e, nused, first, slot, next_e, xs, w_gate, w_up, w_down):
    rows = xs.shape[0] // ROW_TILE
    nblk = rows // MOE_BLOCK
    blk = lambda i, be, nu, *_: (jnp.minimum(i, nu[0] - 1), 0)
    hbm = pl.BlockSpec(memory_space=pl.ANY)
    grid_spec = pltpu.PrefetchScalarGridSpec(
        num_scalar_prefetch=5,
        grid=(nblk,),
        in_specs=[pl.BlockSpec((MOE_BLOCK * ROW_TILE, LANES), blk), hbm, hbm, hbm],
        out_specs=pl.BlockSpec((MOE_BLOCK * ROW_TILE, LANES), blk),
        scratch_shapes=[
            pltpu.VMEM((2, D_MODEL, EXPERT_DIM), f32),
            pltpu.VMEM((2, D_MODEL, EXPERT_DIM), f32),
            pltpu.VMEM((2, EXPERT_DIM, D_MODEL), f32),
            pltpu.VMEM((D_MODEL, EXPERT_DIM), bf16),
            pltpu.VMEM((D_MODEL, EXPERT_DIM), bf16),
            pltpu.VMEM((EXPERT_DIM, D_MODEL), bf16),
            pltpu.SemaphoreType.DMA((2, 3)),
        ],
    )
    return pl.pallas_call(
        _experts_body,
        grid_spec=grid_spec,
        out_shape=jax.ShapeDtypeStruct((rows * ROW_TILE, LANES), jnp.uint32),
        compiler_params=pltpu.CompilerParams(
            dimension_semantics=("arbitrary",), vmem_limit_bytes=VMEM_LIMIT),
        name="routed_experts",
    )(blk_e, nused, first, slot, next_e, xs, w_gate, w_up, w_down)


def _combine_body(d0_ref, d1_ref, d2_ref, h_ref, gate_ref, wsg_ref, wsu_ref, wsd_ref, g2_ref, b2_ref, ys_ref,
                  o_ref, buf0, buf1, sem, *, tm):
    i = pl.program_id(0)

    def issue(d_ref, buf, s):
        for t in range(tm):
            for k in range(TOP_K):
                _row_copy(ys_ref, d_ref[0, k, t], buf.at[k], t, sem.at[s]).start(priority=k % 2)

    def drain(buf, s):
        def body(t, c):
            for k in range(TOP_K):
                _row_copy(ys_ref, 0, buf.at[k], 0, sem.at[s]).wait()
            return c
        lax.fori_loop(0, tm, body, 0, unroll=8)

    def tile(rows, buf):
        h = h_ref[rows]
        hb = h.astype(bf16)
        sg = jnp.dot(hb, wsg_ref[...], preferred_element_type=f32)
        su = jnp.dot(hb, wsu_ref[...], preferred_element_type=f32)
        ffn = jnp.dot((_silu(sg) * su).astype(bf16), wsd_ref[...], preferred_element_type=f32)
        gate = gate_ref[rows]
        for k in range(TOP_K):
            ffn = ffn + gate[:, k:k + 1] * _unpack_bf16_pairs(_load_tile_rows(buf.at[k], tm))
        o_ref[rows] = _layer_norm(DEEPNORM_ALPHA * h + ffn, g2_ref[...], b2_ref[...])

    @pl.when(i == 0)
    def _():
        issue(d0_ref, buf0, 0)

    drain(buf0, 0)
    issue(d1_ref, buf1, 1)
    tile(slice(0, tm), buf0)
    drain(buf1, 1)
    issue(d2_ref, buf0, 0)
    tile(slice(tm, 2 * tm), buf1)

    @pl.when(i == pl.num_programs(0) - 1)
    def _():
        drain(buf0, 0)


def _combine(dest3, h1, ys, gate_tok, wsg, wsu, wsd, g2, b2, *, tm):
    m = h1.shape[0]
    nt = m // tm
    dest_tile = lambda f: pl.BlockSpec((1, TOP_K, tm), lambda i: (f(i), 0, 0), memory_space=pltpu.SMEM)
    return pl.pallas_call(
        functools.partial(_combine_body, tm=tm),
        grid=(nt // 2,),
        in_specs=[
            dest_tile(lambda i: 2 * i),
            dest_tile(lambda i: 2 * i + 1),
            dest_tile(lambda i: jnp.minimum(2 * i + 2, nt - 1)),
            pl.BlockSpec((2 * tm, D_MODEL), lambda i: (i, 0)),
            pl.BlockSpec((2 * tm, TOP_K), lambda i: (i, 0)),
            pl.BlockSpec((D_MODEL, EXPERT_DIM), lambda i: (0, 0)),
            pl.BlockSpec((D_MODEL, EXPERT_DIM), lambda i: (0, 0)),
            pl.BlockSpec((EXPERT_DIM, D_MODEL), lambda i: (0, 0)),
            pl.BlockSpec((1, D_MODEL), lambda i: (0, 0)),
            pl.BlockSpec((1, D_MODEL), lambda i: (0, 0)),
            pl.BlockSpec(memory_space=pl.ANY),
        ],
        out_specs=pl.BlockSpec((2 * tm, D_MODEL), lambda i: (i, 0)),
        out_shape=jax.ShapeDtypeStruct((m, D_MODEL), f32),
        scratch_shapes=[
            pltpu.VMEM((TOP_K, tm * ROW_TILE, LANES), jnp.uint32),
            pltpu.VMEM((TOP_K, tm * ROW_TILE, LANES), jnp.uint32),
            pltpu.SemaphoreType.DMA((2,)),
        ],
        compiler_params=pltpu.CompilerParams(
            dimension_semantics=("arbitrary",), vmem_limit_bytes=VMEM_LIMIT),
        name="combine_ln2",
    )(dest3, dest3, dest3, h1, gate_tok, wsg, wsu, wsd, g2, b2, ys)


def _rope_tables(pos, width):
    half = ROPE_DIM // 2
    inv_freq = jnp.power(ROPE_THETA, -jnp.arange(0, ROPE_DIM, 2, dtype=f32) / ROPE_DIM)
    ang = pos.astype(f32)[:, None] * inv_freq[None, :]
    cos, sin = jnp.cos(ang), jnp.sin(ang)
    n = pos.shape[0]
    pad = jnp.zeros((n, HEAD_DIM - ROPE_DIM), f32)
    zero = jnp.zeros((n, half), f32)
    c = jnp.concatenate([cos, cos, pad + 1.0], axis=1)
    s1 = jnp.concatenate([-sin, zero, pad], axis=1)
    s2 = jnp.concatenate([zero, sin, pad], axis=1)
    tab = jnp.stack([c, s1, s2])
    return jnp.tile(tab, (1, 1, width // HEAD_DIM))


def _group_lanes(v):
    v = v.reshape(SSM_GROUPS, SSM_HEADS_PER_GROUP)
    return jnp.pad(v, ((0, 0), (0, LANES - SSM_HEADS_PER_GROUP))).reshape(1, DT_WIDTH)


def kernel(x, meta_tokens, ln_in_g, ln_in_b, w_in, conv_w, conv_b, dt_bias, a_log, d_skip, ssm_norm_g, att_norm_g, attn_sinks, w_out, ln1_g, ln1_b, w_router, b_router, w_gate, w_up, w_down, ws_gate, ws_up, ws_down, ln2_g, ln2_b):
    batch, seq, d = x.shape
    assert d == D_MODEL and seq % BLOCK == 0 and meta_tokens.shape == (N_META, D_MODEL)
    assert w_in.shape[0] == 1, "single layer"
    n_tok = batch * seq
    nblk = seq // BLOCK
    row = lambda v: v.reshape(1, -1).astype(f32)

    x2d = x.reshape(n_tok, D_MODEL)
    gi, bi = row(ln_in_g), row(ln_in_b)
    w_main = w_in[0].astype(bf16)
    w_dt = jnp.pad(w_in[0, :, MAIN_WIDTH:], ((0, 0), (0, LANES - SSM_HEADS))).astype(bf16)

    proj, dtx = _ln_inproj(x2d, gi, bi, w_main, w_dt, tm=1024, tn=1280)
    proj_m, dtx_m = _ln_inproj(meta_tokens.astype(f32), gi, bi, w_main, w_dt, tm=N_META, tn=512)
    proj_m = jnp.pad(proj_m, ((BLOCK - N_META, 0), (0, 0)))
    dtx_m = jnp.pad(dtx_m, ((CHUNK - N_META, 0), (0, 0)))

    gw = ATT_GROUP * HEAD_DIM
    tab = _rope_tables(N_META + jnp.arange(seq), gw)
    tab_meta = _rope_tables(jnp.arange(BLOCK), gw)
    kv_meta = jnp.roll(proj_m[:, ATT_WIDTH:ATT_WIDTH + 2 * KV_WIDTH], N_META, axis=0)
    att = _attention(proj, kv_meta, tab, tab_meta, attn_sinks[0].astype(f32), row(att_norm_g[0]),
                     batch=batch, nblk=nblk)

    conv_w0 = conv_w[0].astype(f32)
    conv_b0 = row(conv_b[0])
    dtb_x = _group_lanes(dt_bias[0].astype(f32))
    alog_x = _group_lanes(a_log[0].astype(f32))
    dskip_ch = jnp.repeat(d_skip[0].astype(f32), HEAD_DIM).reshape(1, SSM_WIDTH)
    ng = row(ssm_norm_g[0])
    zeros_halo = jnp.zeros((8, CONV_CH), f32)
    zeros_state = jnp.zeros((SSM_GROUPS, SSM_STATE, SSM_GROUP_WIDTH), f32)
    _, s_meta = _ssd(proj_m, dtx_m, zeros_halo, zeros_state, conv_w0, conv_b0, dtb_x, alog_x, dskip_ch, ng,
                     batch=1, nchunk=1, vstart=CHUNK - N_META, emit_state=True)
    halo = proj_m[CHUNK - 8:, XBC_COL:].astype(f32)
    (ssm,) = _ssd(proj, dtx, halo, s_meta, conv_w0, conv_b0, dtb_x, alog_x, dskip_ch, ng,
                  batch=batch, nchunk=nblk, vstart=0, emit_state=False)

    h1 = _outproj(att, ssm, w_out[0].astype(bf16), x2d, gi, bi, row(ln1_g[0]), row(ln1_b[0]),
                  tm=512, tk=2048)

    tt = 512
    tri = (jnp.arange(tt)[:, None] <= jnp.arange(tt)[None, :]).astype(bf16)
    eidx, gate, rank, counts = _router(h1, w_router[0].T.astype(f32), b_router[0].reshape(N_EXPERTS, 1).astype(f32),
                                       tri, tt=tt)
    counts = counts.reshape(N_EXPERTS)
    padded = (counts + MOE_BLOCK - 1) // MOE_BLOCK * MOE_BLOCK
    pend = jnp.cumsum(padded)
    pstart = pend - padded
    first_row = jnp.sum(jnp.where(eidx[..., None] == jnp.arange(N_EXPERTS, dtype=i32), pstart.astype(i32), 0), axis=-1)
    dest = first_row + rank
    n_blocks = n_tok * TOP_K // MOE_BLOCK + N_EXPERTS
    blk_first = jnp.arange(n_blocks, dtype=i32) * MOE_BLOCK
    blk_e = jnp.minimum(jnp.sum(pend[None, :] <= blk_first[:, None], axis=1), N_EXPERTS - 1).astype(i32)
    nused = (pend[-1] // MOE_BLOCK).astype(i32).reshape(1)
    rows = n_blocks * MOE_BLOCK
    tiles = lambda t: dest.reshape(TOP_K, n_tok // t, t).transpose(1, 0, 2)

    td, tc = 512, 128
    xs = _dispatch(tiles(td), counts, pstart.astype(i32), padded.astype(i32), h1, rows=rows, tt=td)
    eids = jnp.arange(N_EXPERTS, dtype=i32)
    nonempty = counts > 0
    ordinal = jnp.cumsum(nonempty.astype(i32)) - nonempty.astype(i32)
    later = (eids[None, :] > eids[:, None]) & nonempty[None, :]
    next_nonempty = jnp.min(jnp.where(later, eids[None, :], N_EXPERTS), axis=1)
    next_nonempty = jnp.where(next_nonempty == N_EXPERTS, -1, next_nonempty).astype(i32)
    onehot_e = blk_e[:, None] == eids[None, :]
    pick = lambda table: jnp.sum(jnp.where(onehot_e, table[None, :], 0), axis=1).astype(i32)
    first = jnp.concatenate([jnp.ones((1,), i32), (blk_e[1:] != blk_e[:-1]).astype(i32)])
    ys = _experts(blk_e, nused, first, pick(ordinal) % 2, pick(next_nonempty), xs, w_gate[0], w_up[0], w_down[0])
    out = _combine(tiles(tc), h1, ys, gate.T, ws_gate[0].astype(bf16), ws_up[0].astype(bf16),
                   ws_down[0].astype(bf16), row(ln2_g[0]), row(ln2_b[0]), tm=tc)
    return out.reshape(batch, seq, D_MODEL)
```
